```python
import jax
import jax.numpy as jnp
from jax import lax
import numpy as np

D_MODEL = 1024
BATCH = 1
SEQ = 16384
DEPTH = 4

CHUNK = 64
PLE_DIM = 256
N_MIXERS = 3
LN_EPS = 1e-5
DEEPNORM_ALPHA = (2 * DEPTH) ** 0.25
DEEPNORM_BETA = (8 * DEPTH) ** -0.25
A_HEADS = 4
A_DK = D_MODEL // 8
A_DV = D_MODEL // 4
A_CONV = 4
B_HEADS = 4
B_DK = D_MODEL // 8
B_DV = D_MODEL // 4
B_RANK = 16
B_TAU = 16.0
C_EXPAND = 128
C_HEADS = D_MODEL // C_EXPAND
C_DV = D_MODEL // C_HEADS
N_EXPERTS = 16
N_GROUPS = 4
EXPERTS_PER_GROUP = N_EXPERTS // N_GROUPS
TOP_K = 2
D_EXPERT = D_MODEL // 2
N_A = (DEPTH + N_MIXERS - 1) // N_MIXERS
N_B = (DEPTH + N_MIXERS - 2) // N_MIXERS
N_C = DEPTH // N_MIXERS
A_PROJ = 2 * A_HEADS * A_DK + 2 * A_HEADS * A_DV + 2 * A_HEADS
B_PROJ = 2 * B_HEADS * B_DK + 2 * B_HEADS * B_DV + B_RANK
C_PROJ = 2 * C_HEADS * C_EXPAND + 2 * C_HEADS * C_DV

kernel_name = 'hybrid_mlstm_gla_hgrn2_moe_trunk'


def layer_norm(x, g, b):
    xf = x.astype(jnp.float32)
    mu = xf.mean(-1, keepdims=True)
    var = jnp.square(xf - mu).mean(-1, keepdims=True)
    y = (xf - mu) * lax.rsqrt(var + LN_EPS) * g.astype(jnp.float32) + b.astype(jnp.float32)
    return y.astype(x.dtype)


def head_rms_norm(h, g):
    bsz, seq, nh, e = h.shape
    hn = h * lax.rsqrt(jnp.mean(h * h, -1, keepdims=True) + LN_EPS)
    return hn.reshape(bsz, seq, nh * e) * g.astype(jnp.float32)


def causal_depthwise_conv(u, w):
    width, ch = w.shape
    return lax.conv_general_dilated(u, w[:, None, :].astype(u.dtype), window_strides=(1,),
                                    padding=[(width - 1, 0)],
                                    dimension_numbers=('NWC', 'WIO', 'NWC'),
                                    feature_group_count=ch)


def to_chunks(t):
    bsz, seq, nh, d = t.shape
    return t.reshape(bsz, seq // CHUNK, CHUNK, nh, d).transpose(1, 0, 3, 2, 4)


def from_chunks(t):
    nc, bsz, nh, l, d = t.shape
    return t.transpose(1, 0, 3, 2, 4).reshape(bsz, nc * l, nh, d)


def mlstm_chunkwise(q, k, v, log_i, log_f):
    bsz, _, nh, dk = q.shape
    dv = v.shape[-1]
    tri = jnp.tril(jnp.ones((CHUNK, CHUNK), dtype=bool))

    def step(carry, xs):
        c_st, n_st, m_st = carry
        qj, ks, vs, li, lf = xs
        a = jnp.cumsum(lf, axis=-1)
        a_tot = a[..., -1]
        d_log = jnp.where(tri, a[..., :, None] - a[..., None, :] + li[..., None, :], -jnp.inf)
        e_log = a + m_st[..., None]
        m_row = jnp.maximum(e_log, d_log.max(-1))
        w_intra = jnp.exp(d_log - m_row[..., None])
        w_inter = jnp.exp(e_log - m_row)
        qk = jnp.einsum('bhjd,bhsd->bhjs', qj, ks) * w_intra
        num = (w_inter[..., None] * jnp.einsum('bhjd,bhde->bhje', qj, c_st)
               + jnp.einsum('bhjs,bhse->bhje', qk, vs))
        den = w_inter * jnp.einsum('bhjd,bhd->bhj', qj, n_st) + qk.sum(-1)
        h = num / jnp.maximum(jnp.abs(den), jnp.exp(-m_row))[..., None]
        g_log = a_tot[..., None] - a + li
        m_new = jnp.maximum(a_tot + m_st, g_log.max(-1))
        decay = jnp.exp(a_tot + m_st - m_new)
        wk = jnp.exp(g_log - m_new[..., None])[..., None] * ks
        c_st = decay[..., None, None] * c_st + jnp.einsum('bhsd,bhse->bhde', wk, vs)
        n_st = decay[..., None] * n_st + wk.sum(-2)
        return (c_st, n_st, m_new), h

    init = (jnp.zeros((bsz, nh, dk, dv), jnp.float32),
            jnp.zeros((bsz, nh, dk), jnp.float32),
            jnp.zeros((bsz, nh), jnp.float32))
    xs = (to_chunks(q), to_chunks(k), to_chunks(v),
          to_chunks(log_i[..., None])[..., 0], to_chunks(log_f[..., None])[..., 0])
    _, h = lax.scan(step, init, xs)
    return from_chunks(h)


def gated_linear_attention_chunkwise(q, k, v, log_a):
    bsz, _, nh, dk = q.shape
    dv = v.shape[-1]
    tri = jnp.tril(jnp.ones((CHUNK, CHUNK), dtype=bool))[:, :, None]

    def step(s_st, xs):
        qj, ks, vs, la = xs
        b = jnp.cumsum(la, axis=-2)
        b_last = b[..., -1:, :]
        inter = jnp.einsum('bhjd,bhde->bhje', qj * jnp.exp(b), s_st)
        pair = jnp.exp(jnp.where(tri, b[..., :, None, :] - b[..., None, :, :], -jnp.inf))
        att = jnp.einsum('bhjsd,bhsd->bhjs', qj[..., :, None, :] * pair, ks)
        o = inter + jnp.einsum('bhjs,bhse->bhje', att, vs)
        s_st = (jnp.exp(b_last[..., 0, :])[..., None] * s_st
                + jnp.einsum('bhsd,bhse->bhde', ks * jnp.exp(b_last - b), vs))
        return s_st, o

    init = jnp.zeros((bsz, nh, dk, dv), jnp.float32)
    _, o = lax.scan(step, init, (to_chunks(q), to_chunks(k), to_chunks(v), to_chunks(log_a)))
    return from_chunks(o)


def mlstm_mixer(x, w_in, conv_w, gate_b, norm_g, w_out):
    bsz, seq, _ = x.shape
    nqk, nv = A_HEADS * A_DK, A_HEADS * A_DV
    z = x @ w_in
    qk, v, o, gates = jnp.split(z, [2 * nqk, 2 * nqk + nv, 2 * nqk + 2 * nv], axis=-1)
    qk = jax.nn.silu(causal_depthwise_conv(qk, conv_w)).astype(jnp.float32)
    q = qk[..., :nqk].reshape(bsz, seq, A_HEADS, A_DK)
    k = qk[..., nqk:].reshape(bsz, seq, A_HEADS, A_DK) * (A_DK ** -0.5)
    v = v.astype(jnp.float32).reshape(bsz, seq, A_HEADS, A_DV)
    gates = gates.astype(jnp.float32) + gate_b.astype(jnp.float32)
    log_i = gates[..., :A_HEADS]
    log_f = jax.nn.log_sigmoid(gates[..., A_HEADS:])
    h = mlstm_chunkwise(q, k, v, log_i, log_f)
    y = head_rms_norm(h, norm_g) * jax.nn.sigmoid(o.astype(jnp.float32))
    return y.astype(x.dtype) @ w_out


def gla_mixer(x, w_in, w_a2, b_a, norm_g, w_out):
    bsz, seq, _ = x.shape
    nk, nv = B_HEADS * B_DK, B_HEADS * B_DV
    z = x @ w_in
    q, k, v, g, a_lr = jnp.split(z, [nk, 2 * nk, 2 * nk + nv, 2 * nk + 2 * nv], axis=-1)
    log_a = jax.nn.log_sigmoid((a_lr @ w_a2 + b_a).astype(jnp.float32)) / B_TAU
    shp = (bsz, seq, B_HEADS, B_DK)
    o = gated_linear_attention_chunkwise(q.astype(jnp.float32).reshape(shp) * (B_DK ** -0.5),
                                         k.astype(jnp.float32).reshape(shp),
                                         v.astype(jnp.float32).reshape(bsz, seq, B_HEADS, B_DV),
                                         log_a.reshape(shp))
    y = head_rms_norm(o, norm_g) * jax.nn.silu(g.astype(jnp.float32))
    return y.astype(x.dtype) @ w_out


def hgrn2_mixer(x, w_in, lower_bound, norm_g, w_out):
    bsz, seq, _ = x.shape
    nk = C_HEADS * C_EXPAND
    z = x @ w_in
    q, f, i, g = jnp.split(z, [nk, 2 * nk, 2 * nk + C_HEADS * C_DV], axis=-1)
    shp = (bsz, seq, C_HEADS, C_EXPAND)
    lb = lower_bound.reshape(C_HEADS, C_EXPAND)
    fg = lb + (1.0 - lb) * jax.nn.sigmoid(f.astype(jnp.float32).reshape(shp))
    o = gated_linear_attention_chunkwise(jax.nn.silu(q.astype(jnp.float32)).reshape(shp),
                                         1.0 - fg,
                                         i.astype(jnp.float32).reshape(bsz, seq, C_HEADS, C_DV),
                                         jnp.log(fg))
    y = head_rms_norm(o, norm_g) * jax.nn.sigmoid(g.astype(jnp.float32))
    return y.astype(x.dtype) @ w_out


def moe_channel_mixer(x, router_w, router_b, w_gate, w_up, w_down):
    bsz, seq, d = x.shape
    t = x.reshape(-1, d)
    scores = jax.nn.softmax((t @ router_w).astype(jnp.float32), axis=-1)
    sel = scores + router_b.astype(jnp.float32)
    grp_score = lax.top_k(sel.reshape(-1, N_GROUPS, EXPERTS_PER_GROUP), TOP_K)[0].sum(-1)
    g_idx = jnp.argmax(grp_score, axis=-1)
    in_group = (jnp.arange(N_EXPERTS) // EXPERTS_PER_GROUP)[None, :] == g_idx[:, None]
    _, e_idx = lax.top_k(jnp.where(in_group, sel, -jnp.inf), TOP_K)
    w = jnp.take_along_axis(scores, e_idx, axis=-1)
    w = w / w.sum(-1, keepdims=True)
    gate = (jax.nn.one_hot(e_idx, N_EXPERTS, dtype=jnp.float32) * w[..., None]).sum(1)
    y = jnp.zeros(t.shape, jnp.float32)
    for e in range(N_EXPERTS):
        h = jax.nn.silu(t @ w_gate[e]) * (t @ w_up[e])
        y = y + gate[:, e:e + 1] * (h @ w_down[e]).astype(jnp.float32)
    return y.astype(x.dtype).reshape(bsz, seq, d)


def setup_inputs(seed: int = 0) -> dict:
    key = jax.random.key(seed)
    ks = jax.random.split(key, 32)
    nrm = lambda k, shape, scale: scale * jax.random.normal(k, shape, jnp.float32)
    d = D_MODEL
    return {
        'x': nrm(ks[0], (BATCH, SEQ, d), 1.0),
        'p': nrm(ks[1], (DEPTH, BATCH, SEQ, PLE_DIM), 1.0),
        'ln1_g': 1.0 + nrm(ks[2], (DEPTH, d), 0.02),
        'ln1_b': nrm(ks[3], (DEPTH, d), 0.02),
        'ln2_g': 1.0 + nrm(ks[4], (DEPTH, d), 0.02),
        'ln2_b': nrm(ks[5], (DEPTH, d), 0.02),
        'mlstm_w_in': nrm(ks[6], (N_A, d, A_PROJ), d ** -0.5),
        'mlstm_conv': nrm(ks[7], (N_A, A_CONV, 2 * A_HEADS * A_DK), A_CONV ** -0.5),
        'mlstm_gate_b': jnp.concatenate([nrm(ks[8], (N_A, A_HEADS), 0.1),
                                         3.0 + nrm(ks[9], (N_A, A_HEADS), 0.5)], axis=-1),
        'mlstm_norm_g': 1.0 + nrm(ks[10], (N_A, A_HEADS * A_DV), 0.02),
        'mlstm_w_out': nrm(ks[11], (N_A, A_HEADS * A_DV, d), DEEPNORM_BETA * (A_HEADS * A_DV) ** -0.5),
        'gla_w_in': nrm(ks[12], (N_B, d, B_PROJ), d ** -0.5),
        'gla_w_a2': nrm(ks[13], (N_B, B_RANK, B_HEADS * B_DK), B_RANK ** -0.5),
        'gla_b_a': nrm(ks[14], (N_B, B_HEADS * B_DK), 0.1),
        'gla_norm_g': 1.0 + nrm(ks[15], (N_B, B_HEADS * B_DV), 0.02),
        'gla_w_out': nrm(ks[16], (N_B, B_HEADS * B_DV, d), DEEPNORM_BETA * (B_HEADS * B_DV) ** -0.5),
        'hgrn_w_in': nrm(ks[17], (N_C, d, C_PROJ), d ** -0.5),
        'hgrn_lb': nrm(ks[18], (DEPTH, C_HEADS * C_EXPAND), 0.5),
        'hgrn_norm_g': 1.0 + nrm(ks[19], (N_C, C_HEADS * C_DV), 0.02),
        'hgrn_w_out': nrm(ks[20], (N_C, C_HEADS * C_DV, d), DEEPNORM_BETA * (C_HEADS * C_DV) ** -0.5),
        'router_w': nrm(ks[21], (d, N_EXPERTS), d ** -0.5),
        'router_b': nrm(ks[22], (N_EXPERTS,), 0.01),
        'exp_w_gate': nrm(ks[23], (DEPTH, N_EXPERTS, d, D_EXPERT), d ** -0.5),
        'exp_w_up': nrm(ks[24], (DEPTH, N_EXPERTS, d, D_EXPERT), d ** -0.5),
        'exp_w_down': nrm(ks[25], (DEPTH, N_EXPERTS, D_EXPERT, d), DEEPNORM_BETA * D_EXPERT ** -0.5),
        'ple_w_proj': nrm(ks[26], (DEPTH, PLE_DIM, d), PLE_DIM ** -0.5),
        'ple_w_gate': nrm(ks[27], (DEPTH, d, d), d ** -0.5),
    }


def reference(x, p, ln1_g, ln1_b, ln2_g, ln2_b,
              mlstm_w_in, mlstm_conv, mlstm_gate_b, mlstm_norm_g, mlstm_w_out,
              gla_w_in, gla_w_a2, gla_b_a, gla_norm_g, gla_w_out,
              hgrn_w_in, hgrn_lb, hgrn_norm_g, hgrn_w_out,
              router_w, router_b, exp_w_gate, exp_w_up, exp_w_down,
              ple_w_proj, ple_w_gate):
    lbs = jax.nn.softmax(hgrn_lb.astype(jnp.float32), axis=0)
    lbs = jnp.cumsum(lbs, axis=0) - lbs[0]
    for i in range(DEPTH):
        kind, slot = i % N_MIXERS, i // N_MIXERS
        if kind == 0:
            mix = mlstm_mixer(x, mlstm_w_in[slot], mlstm_conv[slot], mlstm_gate_b[slot],
                              mlstm_norm_g[slot], mlstm_w_out[slot])
        elif kind == 1:
            mix = gla_mixer(x, gla_w_in[slot], gla_w_a2[slot], gla_b_a[slot],
                            gla_norm_g[slot], gla_w_out[slot])
        else:
            mix = hgrn2_mixer(x, hgrn_w_in[slot], lbs[i], hgrn_norm_g[slot], hgrn_w_out[slot])
        x = layer_norm(DEEPNORM_ALPHA * x + mix, ln1_g[i], ln1_b[i])
        ffn = moe_channel_mixer(x, router_w, router_b, exp_w_gate[i], exp_w_up[i], exp_w_down[i])
        x = layer_norm(DEEPNORM_ALPHA * x + ffn, ln2_g[i], ln2_b[i])
        x = x + jax.nn.sigmoid(x @ ple_w_gate[i]) * (p[i] @ ple_w_proj[i])
    return x
```

```python
import functools

import jax
import jax.numpy as jnp
from jax import lax
from jax.experimental import pallas as pl
from jax.experimental.pallas import tpu as pltpu

F32 = jnp.float32
BF16 = jnp.bfloat16

D_MODEL = 1024
DEPTH = 4
CHUNK = 64
PLE_DIM = 256
N_MIXERS = 3
LN_EPS = 1e-5
DEEPNORM_ALPHA = (2 * DEPTH) ** 0.25
A_HEADS, A_DK, A_DV = 4, 128, 256
B_HEADS, B_DK, B_DV = 4, 128, 256
B_RANK = 16
B_TAU = 16.0
C_HEADS, C_DK, C_DV = 8, 128, 128
N_EXPERTS = 16
N_GROUPS = 4
EXPERTS_PER_GROUP = 4
D_EXPERT = 512

LANES = 128
SUB_BLOCK = 16
A_MAIN = 2 * A_HEADS * A_DK + 2 * A_HEADS * A_DV
A_PROJ_PAD = A_MAIN + LANES
B_MAIN = 2 * B_HEADS * B_DK + 2 * B_HEADS * B_DV
B_PROJ_PAD = B_MAIN + LANES
C_PROJ = 2 * C_HEADS * C_DK + 2 * C_HEADS * C_DV
VMEM_LIMIT = 56 * 1024 * 1024

NT_DIMS = (((1,), (1,)), ((), ()))
TN_DIMS = (((0,), (0,)), ((), ()))


def _cparams(*sem):
    return pltpu.CompilerParams(dimension_semantics=sem, vmem_limit_bytes=VMEM_LIMIT)


def _sigmoid(x):
    return 1.0 / (1.0 + jnp.exp(-x))


def _silu(x):
    return x * _sigmoid(x)


def _log_sigmoid(x):
    return jnp.minimum(x, 0.0) - jnp.log(1.0 + jnp.exp(-jnp.abs(x)))


def _layer_norm(u, g, b):
    mu = jnp.mean(u, axis=-1, keepdims=True)
    d = u - mu
    var = jnp.mean(d * d, axis=-1, keepdims=True)
    return d * lax.rsqrt(var + LN_EPS) * g + b


def _tri(n, lower):
    r = lax.broadcasted_iota(jnp.int32, (n, n), 0)
    c = lax.broadcasted_iota(jnp.int32, (n, n), 1)
    return (r >= c) if lower else (r <= c)


def _bdot(a, b):
    return jnp.dot(a.astype(BF16), b.astype(BF16), preferred_element_type=F32)


def _bdot_nt(a, b):
    return lax.dot_general(a.astype(BF16), b.astype(BF16), NT_DIMS, preferred_element_type=F32)


def _bdot_tn(a, b):
    return lax.dot_general(a.astype(BF16), b.astype(BF16), TN_DIMS, preferred_element_type=F32)


def _const_spec(shape):
    return pl.BlockSpec(shape, lambda i: (0,) * len(shape))


def _inproj_kernel(x_ref, w_ref, z_ref):
    z_ref[...] = jnp.dot(x_ref[...].astype(BF16), w_ref[...], preferred_element_type=F32)


def _inproj_gates_kernel(x_ref, w_ref, wgt_ref, z_ref, gt_ref):
    xb = x_ref[...].astype(BF16)
    z_ref[...] = jnp.dot(xb, w_ref[...], preferred_element_type=F32)
    for c in range(gt_ref.shape[0]):
        gt_ref[c] = lax.dot_general(wgt_ref[...], xb[c * CHUNK:(c + 1) * CHUNK], NT_DIMS,
                                    preferred_element_type=F32)


def _inproj(x, w, wgt=None, tm=512):
    t, d = x.shape
    n = w.shape[1]
    grid = (t // tm,)
    x_spec = pl.BlockSpec((tm, d), lambda i: (i, 0))
    z_spec = pl.BlockSpec((tm, n), lambda i: (i, 0))
    if wgt is None:
        return pl.pallas_call(
            _inproj_kernel, grid=grid, in_specs=[x_spec, _const_spec((d, n))], out_specs=z_spec,
            out_shape=jax.ShapeDtypeStruct((t, n), F32), compiler_params=_cparams("parallel"),
            name="inproj")(x, w)
    return pl.pallas_call(
        _inproj_gates_kernel, grid=grid,
        in_specs=[x_spec, _const_spec((d, n)), _const_spec(wgt.shape)],
        out_specs=[z_spec, pl.BlockSpec((tm // CHUNK, 16, CHUNK), lambda i: (i, 0, 0))],
        out_shape=[jax.ShapeDtypeStruct((t, n), F32),
                   jax.ShapeDtypeStruct((t // CHUNK, 16, CHUNK), F32)],
        compiler_params=_cparams("parallel"), name="inproj_gates")(x, w, wgt)


def _mlstm_kernel(z_ref, gt_ref, conv_ref, gbr_ref, gbc_ref, ng_ref, y_ref,
                  ext_ref, ct_ref, n_ref, m_ref):
    tt = z_ref.shape[0]
    nqk = A_HEADS * A_DK

    @pl.when(pl.program_id(0) == 0)
    def _():
        ext_ref[0:8, :] = jnp.zeros((8, 2 * nqk), F32)
        ct_ref[...] = jnp.zeros(ct_ref.shape, F32)
        n_ref[...] = jnp.zeros(n_ref.shape, F32)
        m_ref[...] = jnp.zeros(m_ref.shape, F32)

    qk_raw = z_ref[:, 0:2 * nqk]
    ext_ref[8:8 + tt, :] = qk_raw
    acc = conv_ref[3:4, :] * qk_raw
    for s in range(1, 4):
        acc = acc + conv_ref[3 - s:4 - s, :] * ext_ref[pl.ds(8 - s, tt), :]
    ext_ref[0:8, :] = qk_raw[tt - 8:tt, :]
    qkc = _silu(acc)

    tri_lo = _tri(CHUNK, True)
    tri_lo_f = tri_lo.astype(F32)
    tri_up_f = _tri(CHUNK, False).astype(F32)

    for c in range(tt // CHUNK):
        r0 = c * CHUNK
        gc = z_ref[r0:r0 + CHUNK, A_MAIN:A_MAIN + LANES] + gbr_ref[...]
        a_col = jnp.dot(tri_lo_f, _log_sigmoid(gc), preferred_element_type=F32,
                        precision=lax.Precision.HIGHEST)
        gr = gt_ref[c] + gbc_ref[:, 0:1]
        a_row = jnp.dot(_log_sigmoid(gr), tri_up_f, preferred_element_type=F32,
                        precision=lax.Precision.HIGHEST)
        for h in range(A_HEADS):
            ac = a_col[:, A_HEADS + h:A_HEADS + h + 1]
            lic = gc[:, h:h + 1]
            ar = a_row[A_HEADS + h:A_HEADS + h + 1, :]
            lir = gr[h:h + 1, :]
            m_st = m_ref[h:h + 1, 0:1]
            qh = qkc[r0:r0 + CHUNK, h * A_DK:(h + 1) * A_DK]
            kh = qkc[r0:r0 + CHUNK, nqk + h * A_DK:nqk + (h + 1) * A_DK] * (A_DK ** -0.5)
            vh = z_ref[r0:r0 + CHUNK, 2 * nqk + h * A_DV:2 * nqk + (h + 1) * A_DV]
            oh = z_ref[r0:r0 + CHUNK, 2 * nqk + A_HEADS * A_DV + h * A_DV:
                       2 * nqk + A_HEADS * A_DV + (h + 1) * A_DV]

            d_log = jnp.where(tri_lo, ac - ar + lir, -jnp.inf)
            e_log = ac + m_st
            m_row = jnp.maximum(e_log, jnp.max(d_log, axis=-1, keepdims=True))
            w_intra = jnp.exp(d_log - m_row)
            w_inter = jnp.exp(e_log - m_row)
            s = _bdot_nt(qh, kh) * w_intra
            num = w_inter * _bdot_nt(qh, ct_ref[h]) + _bdot(s, vh)
            den = (w_inter * jnp.sum(qh * n_ref[h:h + 1, :], axis=-1, keepdims=True)
                   + jnp.sum(s, axis=-1, keepdims=True))
            hout = num / jnp.maximum(jnp.abs(den), jnp.exp(-m_row))

            a_tot = ar[:, CHUNK - 1:CHUNK]
            g_row = a_tot - ar + lir
            g_col = a_tot - ac + lic
            m_new = jnp.maximum(a_tot + m_st, jnp.max(g_row, axis=-1, keepdims=True))
            decay = jnp.exp(a_tot + m_st - m_new)
            wk = jnp.exp(g_col - m_new) * kh
            ct_ref[h] = decay * ct_ref[h] + _bdot_tn(vh, wk)
            n_ref[h:h + 1, :] = decay * n_ref[h:h + 1, :] + jnp.sum(wk, axis=0, keepdims=True)
            m_ref[h:h + 1, :] = jnp.broadcast_to(m_new, (1, LANES))

            hn = hout * lax.rsqrt(jnp.mean(hout * hout, axis=-1, keepdims=True) + LN_EPS)
            y_ref[r0:r0 + CHUNK, h * A_DV:(h + 1) * A_DV] = (
                hn * ng_ref[:, h * A_DV:(h + 1) * A_DV] * _sigmoid(oh))


def _mlstm_mix(z, gt, conv_w, gate_b, norm_g, tt=128):
    t = z.shape[0]
    nqk2 = 2 * A_HEADS * A_DK
    conv_p = jnp.zeros((8, nqk2), F32).at[0:4].set(conv_w.astype(F32))
    gb = gate_b.astype(F32)
    gb_row = jnp.zeros((1, LANES), F32).at[0, 0:2 * A_HEADS].set(gb)
    gb_col = jnp.zeros((16, LANES), F32).at[0:2 * A_HEADS, :].set(jnp.broadcast_to(gb[:, None], (2 * A_HEADS, LANES)))
    ng = norm_g.astype(F32).reshape(1, A_HEADS * A_DV)
    return pl.pallas_call(
        _mlstm_kernel, grid=(t // tt,),
        in_specs=[pl.BlockSpec((tt, A_PROJ_PAD), lambda i: (i, 0)),
                  pl.BlockSpec((tt // CHUNK, 16, CHUNK), lambda i: (i, 0, 0)),
                  _const_spec((8, nqk2)), _const_spec((1, LANES)), _const_spec((16, LANES)),
                  _const_spec((1, A_HEADS * A_DV))],
        out_specs=pl.BlockSpec((tt, A_HEADS * A_DV), lambda i: (i, 0)),
        out_shape=jax.ShapeDtypeStruct((t, A_HEADS * A_DV), F32),
        scratch_shapes=[pltpu.VMEM((tt + 8, nqk2), F32),
                        pltpu.VMEM((A_HEADS, A_DV, A_DK), F32),
                        pltpu.VMEM((8, A_DK), F32),
                        pltpu.VMEM((8, LANES), F32)],
        compiler_params=_cparams("arbitrary"), name="mlstm_mix")(z, gt, conv_p, gb_row, gb_col, ng)


def _decay_attention_chunk(q, k, v, la, st_ref, heads, dv):
    dk = LANES
    b = jnp.dot(_tri(CHUNK, True).astype(F32), la, preferred_element_type=F32,
                precision=lax.Precision.HIGHEST)
    b_last = b[CHUNK - 1:CHUNK, :]
    qe = q * jnp.exp(b)
    kd = k * jnp.exp(b_last - b)
    row_in_block = lax.broadcasted_iota(jnp.int32, (CHUNK, 1), 0) % SUB_BLOCK

    outs = [_bdot_nt(qe[:, h * dk:(h + 1) * dk], st_ref[h]) for h in range(heads)]

    for delta in range(SUB_BLOCK):
        if delta == 0:
            p = q * k
            vs = v
        else:
            valid = row_in_block >= delta
            arg = jnp.where(valid, b - pltpu.roll(b, delta, 0), 0.0)
            p = q * pltpu.roll(k, delta, 0) * jnp.exp(arg)
            vs = pltpu.roll(v, delta, 0)
        for h in range(heads):
            w = jnp.sum(p[:, h * dk:(h + 1) * dk], axis=-1, keepdims=True)
            if delta:
                w = jnp.where(valid, w, 0.0)
            outs[h] = outs[h] + w * vs[:, h * dv:(h + 1) * dv]

    for h in range(heads):
        pieces = [jnp.zeros((SUB_BLOCK, dv), F32)]
        for j in range(1, CHUNK // SUB_BLOCK):
            lo = j * SUB_BLOCK
            ref_row = b[lo - 1:lo, h * dk:(h + 1) * dk]
            qj = q[lo:lo + SUB_BLOCK, h * dk:(h + 1) * dk] * jnp.exp(b[lo:lo + SUB_BLOCK, h * dk:(h + 1) * dk] - ref_row)
            kj = k[0:lo, h * dk:(h + 1) * dk] * jnp.exp(ref_row - b[0:lo, h * dk:(h + 1) * dk])
            att = _bdot_nt(qj, kj)
            pieces.append(_bdot(att, v[0:lo, h * dv:(h + 1) * dv]))
        outs[h] = outs[h] + jnp.concatenate(pieces, axis=0)
        st_ref[h] = (st_ref[h] * jnp.exp(b_last[:, h * dk:(h + 1) * dk])
                     + _bdot_tn(v[:, h * dv:(h + 1) * dv], kd[:, h * dk:(h + 1) * dk]))
    return outs


def _head_norm_store(outs, y_ref, r0, ng_ref, gate_act, dv):
    for h, o in enumerate(outs):
        hn = o * lax.rsqrt(jnp.mean(o * o, axis=-1, keepdims=True) + LN_EPS)
        y_ref[r0:r0 + CHUNK, h * dv:(h + 1) * dv] = (
            hn * ng_ref[:, h * dv:(h + 1) * dv] * gate_act[:, h * dv:(h + 1) * dv])


def _gla_kernel(z_ref, wa2_ref, ba_ref, ng_ref, y_ref, st_ref):
    @pl.when(pl.program_id(0) == 0)
    def _():
        st_ref[...] = jnp.zeros(st_ref.shape, F32)

    nk, nv = B_HEADS * B_DK, B_HEADS * B_DV
    for c in range(z_ref.shape[0] // CHUNK):
        r0 = c * CHUNK
        q = z_ref[r0:r0 + CHUNK, 0:nk] * (B_DK ** -0.5)
        k = z_ref[r0:r0 + CHUNK, nk:2 * nk]
        v = z_ref[r0:r0 + CHUNK, 2 * nk:2 * nk + nv]
        g = z_ref[r0:r0 + CHUNK, 2 * nk + nv:2 * nk + 2 * nv]
        a_lr = z_ref[r0:r0 + CHUNK, B_MAIN:B_MAIN + LANES]
        la = _log_sigmoid(jnp.dot(a_lr.astype(BF16), wa2_ref[...], preferred_element_type=F32)
                          + ba_ref[...]) / B_TAU
        outs = _decay_attention_chunk(q, k, v, la, st_ref, B_HEADS, B_DV)
        _head_norm_store(outs, y_ref, r0, ng_ref, _silu(g), B_DV)


def _gla_mix(z, w_a2, b_a, norm_g, tt=128):
    t = z.shape[0]
    nk, nv = B_HEADS * B_DK, B_HEADS * B_DV
    wa2 = jnp.zeros((LANES, nk), BF16).at[0:B_RANK].set(w_a2.astype(BF16))
    return pl.pallas_call(
        _gla_kernel, grid=(t // tt,),
        in_specs=[pl.BlockSpec((tt, B_PROJ_PAD), lambda i: (i, 0)),
                  _const_spec((LANES, nk)), _const_spec((1, nk)), _const_spec((1, nv))],
        out_specs=pl.BlockSpec((tt, nv), lambda i: (i, 0)),
        out_shape=jax.ShapeDtypeStruct((t, nv), F32),
        scratch_shapes=[pltpu.VMEM((B_HEADS, B_DV, B_DK), F32)],
        compiler_params=_cparams("arbitrary"), name="gla_mix")(
            z, wa2, b_a.astype(F32).reshape(1, nk), norm_g.astype(F32).reshape(1, nv))


def _hgrn_kernel(layer, z_ref, lbp_ref, ng_ref, y_ref, st_ref):
    @pl.when(pl.program_id(0) == 0)
    def _():
        st_ref[...] = jnp.zeros(st_ref.shape, F32)

    lbp = lbp_ref[0:DEPTH, :]
    e = jnp.exp(lbp - jnp.max(lbp, axis=0, keepdims=True))
    sm = e / jnp.sum(e, axis=0, keepdims=True)
    lb = jnp.zeros((1, sm.shape[1]), F32)
    for r in range(layer + 1):
        lb = lb + sm[r:r + 1, :]
    lb = lb - sm[0:1, :]

    nk, nv = C_HEADS * C_DK, C_HEADS * C_DV
    for c in range(z_ref.shape[0] // CHUNK):
        r0 = c * CHUNK
        q = _silu(z_ref[r0:r0 + CHUNK, 0:nk])
        fg = lb + (1.0 - lb) * _sigmoid(z_ref[r0:r0 + CHUNK, nk:2 * nk])
        v = z_ref[r0:r0 + CHUNK, 2 * nk:2 * nk + nv]
        g = z_ref[r0:r0 + CHUNK, 2 * nk + nv:2 * nk + 2 * nv]
        outs = _decay_attention_chunk(q, 1.0 - fg, v, jnp.log(fg), st_ref, C_HEADS, C_DV)
        _head_norm_store(outs, y_ref, r0, ng_ref, _sigmoid(g), C_DV)


def _hgrn_mix(z, hgrn_lb, layer, norm_g, tt=128):
    t = z.shape[0]
    nk, nv = C_HEADS * C_DK, C_HEADS * C_DV
    lbp = jnp.zeros((8, nk), F32).at[0:DEPTH].set(hgrn_lb.astype(F32))
    return pl.pallas_call(
        functools.partial(_hgrn_kernel, layer), grid=(t // tt,),
        in_specs=[pl.BlockSpec((tt, C_PROJ), lambda i: (i, 0)),
                  _const_spec((8, nk)), _const_spec((1, nv))],
        out_specs=pl.BlockSpec((tt, nv), lambda i: (i, 0)),
        out_shape=jax.ShapeDtypeStruct((t, nv), F32),
        scratch_shapes=[pltpu.VMEM((C_HEADS, C_DV, C_DK), F32)],
        compiler_params=_cparams("arbitrary"), name="hgrn_mix")(
            z, lbp, norm_g.astype(F32).reshape(1, nv))


def _route(lt, rb_ref, gt_scr):
    rows = [lt[e:e + 1, :] for e in range(N_EXPERTS)]
    mx = rows[0]
    for r in rows[1:]:
        mx = jnp.maximum(mx, r)
    ex = [jnp.exp(r - mx) for r in rows]
    tot = ex[0]
    for r in ex[1:]:
        tot = tot + r
    scores = [r / tot for r in ex]
    sel = [scores[e] + rb_ref[e:e + 1, 0:1] for e in range(N_EXPERTS)]

    gscore = []
    for g in range(N_GROUPS):
        m = sel[g * EXPERTS_PER_GROUP:(g + 1) * EXPERTS_PER_GROUP]
        best = None
        for a in range(EXPERTS_PER_GROUP):
            for b2 in range(a + 1, EXPERTS_PER_GROUP):
                pair = m[a] + m[b2]
                best = pair if best is None else jnp.maximum(best, pair)
        gscore.append(best)
    g_best = gscore[0]
    g_idx = jnp.zeros_like(g_best, dtype=jnp.int32)
    for g in range(1, N_GROUPS):
        better = gscore[g] > g_best
        g_best = jnp.where(better, gscore[g], g_best)
        g_idx = jnp.where(better, g, g_idx)

    masked = [jnp.where(g_idx == (e // EXPERTS_PER_GROUP), sel[e], -jnp.inf) for e in range(N_EXPERTS)]

    def first_argmax(vals):
        best_v, best_i = vals[0], jnp.zeros_like(g_idx)
        for e in range(1, N_EXPERTS):
            better = vals[e] > best_v
            best_v = jnp.where(better, vals[e], best_v)
            best_i = jnp.where(better, e, best_i)
        return best_i

    i1 = first_argmax(masked)
    i2 = first_argmax([jnp.where(i1 == e, -jnp.inf, masked[e]) for e in range(N_EXPERTS)])
    w1 = jnp.zeros_like(mx)
    w2 = jnp.zeros_like(mx)
    for e in range(N_EXPERTS):
        w1 = jnp.where(i1 == e, scores[e], w1)
        w2 = jnp.where(i2 == e, scores[e], w2)
    wsum = w1 + w2
    for e in range(N_EXPERTS):
        gt_scr[e:e + 1, :] = (jnp.where(i1 == e, w1, 0.0) + jnp.where(i2 == e, w2, 0.0)) / wsum


def _outproj_kernel(y_ref, x_ref, w_ref, g_ref, b_ref, rw_ref, rb_ref, x1_ref, gate_ref, gt_scr):
    mix = jnp.dot(y_ref[...].astype(BF16), w_ref[...], preferred_element_type=F32)
    x1 = _layer_norm(DEEPNORM_ALPHA * x_ref[...] + mix, g_ref[...], b_ref[...])
    x1_ref[...] = x1
    logits = jnp.dot(x1, rw_ref[...], preferred_element_type=F32, precision=lax.Precision.HIGHEST)
    gt_scr[...] = jnp.zeros(gt_scr.shape, F32)
    _route(logits.T, rb_ref, gt_scr)
    gate_ref[...] = gt_scr[...].T


def _outproj_ln_route(y, x, w_out, ln_g, ln_b, router_w, router_b, tm=512):
    t, d = x.shape
    rw = jnp.zeros((d, LANES), F32).at[:, 0:N_EXPERTS].set(router_w.astype(F32))
    rb = jnp.broadcast_to(router_b.astype(F32)[:, None], (N_EXPERTS, LANES))
    tile = pl.BlockSpec((tm, d), lambda i: (i, 0))
    return pl.pallas_call(
        _outproj_kernel, grid=(t // tm,),
        in_specs=[pl.BlockSpec((tm, y.shape[1]), lambda i: (i, 0)), tile,
                  _const_spec(w_out.shape), _const_spec((1, d)), _const_spec((1, d)),
                  _const_spec((d, LANES)), _const_spec((N_EXPERTS, LANES))],
        out_specs=[tile, pl.BlockSpec((tm, LANES), lambda i: (i, 0))],
        out_shape=[jax.ShapeDtypeStruct((t, d), F32), jax.ShapeDtypeStruct((t, LANES), F32)],
        scratch_shapes=[pltpu.VMEM((LANES, tm), F32)],
        compiler_params=_cparams("parallel"), name="outproj_ln_route")(
            y, x, w_out, ln_g.astype(F32).reshape(1, d), ln_b.astype(F32).reshape(1, d), rw, rb)


def _moe_kernel(x1_ref, gate_ref, wg_ref, wu_ref, wd_ref, g_ref, b_ref, x2_ref, acc_ref, xb_ref):
    e = pl.program_id(1)

    @pl.when(e == 0)
    def _():
        acc_ref[...] = jnp.zeros(acc_ref.shape, F32)
        xb_ref[...] = x1_ref[...].astype(BF16)

    xb = xb_ref[...]
    hg = jnp.dot(xb, wg_ref[0], preferred_element_type=F32)
    hu = jnp.dot(xb, wu_ref[0], preferred_element_type=F32)
    ye = jnp.dot((_silu(hg) * hu).astype(BF16), wd_ref[0], preferred_element_type=F32)
    lane = lax.broadcasted_iota(jnp.int32, gate_ref.shape, 1)
    col = jnp.sum(jnp.where(lane == e, gate_ref[...], 0.0), axis=-1, keepdims=True)
    acc_ref[...] += col * ye

    @pl.when(e == N_EXPERTS - 1)
    def _():
        x2_ref[...] = _layer_norm(DEEPNORM_ALPHA * x1_ref[...] + acc_ref[...], g_ref[...], b_ref[...])


def _moe_ln(x1, gate, wg, wu, wd, ln_g, ln_b, tm=1024):
    t, d = x1.shape
    tile = pl.BlockSpec((tm, d), lambda i, e: (i, 0))
    vec = pl.BlockSpec((1, d), lambda i, e: (0, 0))
    return pl.pallas_call(
        _moe_kernel, grid=(t // tm, N_EXPERTS),
        in_specs=[tile, pl.BlockSpec((tm, LANES), lambda i, e: (i, 0)),
                  pl.BlockSpec((1, d, D_EXPERT), lambda i, e: (e, 0, 0)),
                  pl.BlockSpec((1, d, D_EXPERT), lambda i, e: (e, 0, 0)),
                  pl.BlockSpec((1, D_EXPERT, d), lambda i, e: (e, 0, 0)), vec, vec],
        out_specs=tile, out_shape=jax.ShapeDtypeStruct((t, d), F32),
        scratch_shapes=[pltpu.VMEM((tm, d), F32), pltpu.VMEM((tm, d), BF16)],
        compiler_params=_cparams("parallel", "arbitrary"), name="moe_ln")(
            x1, gate, wg, wu, wd, ln_g.astype(F32).reshape(1, d), ln_b.astype(F32).reshape(1, d))


def _ple_kernel(x_ref, p_ref, wg_ref, wp_ref, o_ref):
    x = x_ref[...]
    gate = _sigmoid(jnp.dot(x.astype(BF16), wg_ref[...], preferred_element_type=F32))
    o_ref[...] = x + gate * jnp.dot(p_ref[...].astype(BF16), wp_ref[...], preferred_element_type=F32)


def _ple(x, p, w_gate, w_proj, tm=512):
    t, d = x.shape
    tile = pl.BlockSpec((tm, d), lambda i: (i, 0))
    return pl.pallas_call(
        _ple_kernel, grid=(t // tm,),
        in_specs=[tile, pl.BlockSpec((tm, PLE_DIM), lambda i: (i, 0)),
                  _const_spec((d, d)), _const_spec((PLE_DIM, d))],
        out_specs=tile, out_shape=jax.ShapeDtypeStruct((t, d), F32),
        compiler_params=_cparams("parallel"), name="ple")(x, p, w_gate, w_proj)


def _pad_cols(w, n):
    return jnp.zeros((w.shape[0], n), w.dtype).at[:, 0:w.shape[1]].set(w)


def kernel(x, p, ln1_g, ln1_b, ln2_g, ln2_b, mlstm_w_in, mlstm_conv, mlstm_gate_b, mlstm_norm_g, mlstm_w_out, gla_w_in, gla_w_a2, gla_b_a, gla_norm_g, gla_w_out, hgrn_w_in, hgrn_lb, hgrn_norm_g, hgrn_w_out, router_w, router_b, exp_w_gate, exp_w_up, exp_w_down, ple_w_proj, ple_w_gate):
    bsz, seq, d = x.shape
    assert bsz == 1 and d == D_MODEL and seq % 1024 == 0
    xt = x.reshape(seq, d).astype(F32)
    for i in range(DEPTH):
        kind, slot = i % N_MIXERS, i // N_MIXERS
        if kind == 0:
            w_in = mlstm_w_in[slot]
            w = _pad_cols(w_in.astype(BF16), A_PROJ_PAD)
            wgt = jnp.zeros((16, d), BF16).at[0:2 * A_HEADS].set(w_in[:, A_MAIN:].T.astype(BF16))
            z, gt = _inproj(xt, w, wgt)
            y = _mlstm_mix(z, gt, mlstm_conv[slot], mlstm_gate_b[slot], mlstm_norm_g[slot])
            w_out = mlstm_w_out[slot]
        elif kind == 1:
            z = _inproj(xt, _pad_cols(gla_w_in[slot].astype(BF16), B_PROJ_PAD))
            y = _gla_mix(z, gla_w_a2[slot], gla_b_a[slot], gla_norm_g[slot])
            w_out = gla_w_out[slot]
        else:
            z = _inproj(xt, hgrn_w_in[slot].astype(BF16))
            y = _hgrn_mix(z, hgrn_lb, i, hgrn_norm_g[slot])
            w_out = hgrn_w_out[slot]
        x1, gate = _outproj_ln_route(y, xt, w_out.astype(BF16), ln1_g[i], ln1_b[i], router_w, router_b)
        x2 = _moe_ln(x1, gate, exp_w_gate[i].astype(BF16), exp_w_up[i].astype(BF16),
                     exp_w_down[i].astype(BF16), ln2_g[i], ln2_b[i])
        xt = _ple(x2, p[i].reshape(seq, PLE_DIM), ple_w_gate[i].astype(BF16), ple_w_proj[i].astype(BF16))
    return xt.reshape(bsz, seq, d)
```

```python
import functools

import jax
import jax.numpy as jnp
from jax import lax
from jax.experimental import pallas as pl
from jax.experimental.pallas import tpu as pltpu

F32 = jnp.float32
BF16 = jnp.bfloat16

D_MODEL = 1024
DEPTH = 4
CHUNK = 64
PLE_DIM = 256
N_MIXERS = 3
LN_EPS = 1e-5
DEEPNORM_ALPHA = (2 * DEPTH) ** 0.25
A_HEADS, A_DK, A_DV = 4, 128, 256
B_HEADS, B_DK, B_DV = 4, 128, 256
B_RANK = 16
B_TAU = 16.0
C_HEADS, C_DK, C_DV = 8, 128, 128
N_EXPERTS = 16
N_GROUPS = 4
EXPERTS_PER_GROUP = 4
D_EXPERT = 512
PAIRS_PER_GROUP = EXPERTS_PER_GROUP * (EXPERTS_PER_GROUP - 1) // 2
N_BUCKETS = N_GROUPS * PAIRS_PER_GROUP
BUCKET_ROWS = 32
MOE_TILE = 256

LANES = 128
SUB_BLOCK = 16
A_MAIN = 2 * A_HEADS * A_DK + 2 * A_HEADS * A_DV
A_PROJ_PAD = A_MAIN + LANES
B_MAIN = 2 * B_HEADS * B_DK + 2 * B_HEADS * B_DV
B_PROJ_PAD = B_MAIN + LANES
C_PROJ = 2 * C_HEADS * C_DK + 2 * C_HEADS * C_DV
VMEM_LIMIT = 56 * 1024 * 1024

NT_DIMS = (((1,), (1,)), ((), ()))
TN_DIMS = (((0,), (0,)), ((), ()))


def _cparams(*sem):
    return pltpu.CompilerParams(dimension_semantics=sem, vmem_limit_bytes=VMEM_LIMIT)


def _sigmoid(x):
    return 1.0 / (1.0 + jnp.exp(-x))


def _silu(x):
    return x * _sigmoid(x)


def _log_sigmoid(x):
    return jnp.minimum(x, 0.0) - jnp.log(1.0 + jnp.exp(-jnp.abs(x)))


def _layer_norm(u, g, b):
    mu = jnp.mean(u, axis=-1, keepdims=True)
    d = u - mu
    var = jnp.mean(d * d, axis=-1, keepdims=True)
    return d * lax.rsqrt(var + LN_EPS) * g + b


def _tri(n, lower):
    r = lax.broadcasted_iota(jnp.int32, (n, n), 0)
    c = lax.broadcasted_iota(jnp.int32, (n, n), 1)
    return (r >= c) if lower else (r <= c)


def _bdot(a, b):
    return jnp.dot(a.astype(BF16), b.astype(BF16), preferred_element_type=F32)


def _bdot_nt(a, b):
    return lax.dot_general(a.astype(BF16), b.astype(BF16), NT_DIMS, preferred_element_type=F32)


def _bdot_tn(a, b):
    return lax.dot_general(a.astype(BF16), b.astype(BF16), TN_DIMS, preferred_element_type=F32)


def _const_spec(shape):
    return pl.BlockSpec(shape, lambda i: (0,) * len(shape))


def _inproj_kernel(x_ref, w_ref, z_ref):
    z_ref[...] = jnp.dot(x_ref[...].astype(BF16), w_ref[...], preferred_element_type=F32)


def _inproj_gates_kernel(x_ref, w_ref, wgt_ref, z_ref, gt_ref):
    xb = x_ref[...].astype(BF16)
    z_ref[...] = jnp.dot(xb, w_ref[...], preferred_element_type=F32)
    for c in range(gt_ref.shape[0]):
        gt_ref[c] = lax.dot_general(wgt_ref[...], xb[c * CHUNK:(c + 1) * CHUNK], NT_DIMS,
                                    preferred_element_type=F32)


def _inproj(x, w, wgt=None, tm=512):
    t, d = x.shape
    n = w.shape[1]
    grid = (t // tm,)
    x_spec = pl.BlockSpec((tm, d), lambda i: (i, 0))
    z_spec = pl.BlockSpec((tm, n), lambda i: (i, 0))
    if wgt is None:
        return pl.pallas_call(
            _inproj_kernel, grid=grid, in_specs=[x_spec, _const_spec((d, n))], out_specs=z_spec,
            out_shape=jax.ShapeDtypeStruct((t, n), F32), compiler_params=_cparams("parallel"),
            name="inproj")(x, w)
    return pl.pallas_call(
        _inproj_gates_kernel, grid=grid,
        in_specs=[x_spec, _const_spec((d, n)), _const_spec(wgt.shape)],
        out_specs=[z_spec, pl.BlockSpec((tm // CHUNK, 16, CHUNK), lambda i: (i, 0, 0))],
        out_shape=[jax.ShapeDtypeStruct((t, n), F32),
                   jax.ShapeDtypeStruct((t // CHUNK, 16, CHUNK), F32)],
        compiler_params=_cparams("parallel"), name="inproj_gates")(x, w, wgt)


def _mlstm_kernel(z_ref, gt_ref, conv_ref, gbr_ref, gbc_ref, ng_ref, y_ref,
                  ext_ref, ct_ref, n_ref, m_ref):
    tt = z_ref.shape[0]
    nqk = A_HEADS * A_DK

    @pl.when(pl.program_id(0) == 0)
    def _():
        ext_ref[0:8, :] = jnp.zeros((8, 2 * nqk), F32)
        ct_ref[...] = jnp.zeros(ct_ref.shape, F32)
        n_ref[...] = jnp.zeros(n_ref.shape, F32)
        m_ref[...] = jnp.zeros(m_ref.shape, F32)

    qk_raw = z_ref[:, 0:2 * nqk]
    ext_ref[8:8 + tt, :] = qk_raw
    acc = conv_ref[3:4, :] * qk_raw
    for s in range(1, 4):
        acc = acc + conv_ref[3 - s:4 - s, :] * ext_ref[pl.ds(8 - s, tt), :]
    ext_ref[0:8, :] = qk_raw[tt - 8:tt, :]
    qkc = _silu(acc)

    tri_lo = _tri(CHUNK, True)
    tri_lo_f = tri_lo.astype(F32)
    tri_up_f = _tri(CHUNK, False).astype(F32)

    for c in range(tt // CHUNK):
        r0 = c * CHUNK
        gc = z_ref[r0:r0 + CHUNK, A_MAIN:A_MAIN + LANES] + gbr_ref[...]
        a_col = jnp.dot(tri_lo_f, _log_sigmoid(gc), preferred_element_type=F32,
                        precision=lax.Precision.HIGHEST)
        gr = gt_ref[c] + gbc_ref[:, 0:1]
        a_row = jnp.dot(_log_sigmoid(gr), tri_up_f, preferred_element_type=F32,
                        precision=lax.Precision.HIGHEST)
        for h in range(A_HEADS):
            ac = a_col[:, A_HEADS + h:A_HEADS + h + 1]
            lic = gc[:, h:h + 1]
            ar = a_row[A_HEADS + h:A_HEADS + h + 1, :]
            lir = gr[h:h + 1, :]
            m_st = m_ref[h:h + 1, 0:1]
            qh = qkc[r0:r0 + CHUNK, h * A_DK:(h + 1) * A_DK]
            kh = qkc[r0:r0 + CHUNK, nqk + h * A_DK:nqk + (h + 1) * A_DK] * (A_DK ** -0.5)
            vh = z_ref[r0:r0 + CHUNK, 2 * nqk + h * A_DV:2 * nqk + (h + 1) * A_DV]
            oh = z_ref[r0:r0 + CHUNK, 2 * nqk + A_HEADS * A_DV + h * A_DV:
                       2 * nqk + A_HEADS * A_DV + (h + 1) * A_DV]

            d_log = jnp.where(tri_lo, ac - ar + lir, -jnp.inf)
            e_log = ac + m_st
            m_row = jnp.maximum(e_log, jnp.max(d_log, axis=-1, keepdims=True))
            w_intra = jnp.exp(d_log - m_row)
            w_inter = jnp.exp(e_log - m_row)
            s = _bdot_nt(qh, kh) * w_intra
            num = w_inter * _bdot_nt(qh, ct_ref[h]) + _bdot(s, vh)
            den = (w_inter * jnp.sum(qh * n_ref[h:h + 1, :], axis=-1, keepdims=True)
                   + jnp.sum(s, axis=-1, keepdims=True))
            hout = num / jnp.maximum(jnp.abs(den), jnp.exp(-m_row))

            a_tot = ar[:, CHUNK - 1:CHUNK]
            g_row = a_tot - ar + lir
            g_col = a_tot - ac + lic
            m_new = jnp.maximum(a_tot + m_st, jnp.max(g_row, axis=-1, keepdims=True))
            decay = jnp.exp(a_tot + m_st - m_new)
            wk = jnp.exp(g_col - m_new) * kh
            ct_ref[h] = decay * ct_ref[h] + _bdot_tn(vh, wk)
            n_ref[h:h + 1, :] = decay * n_ref[h:h + 1, :] + jnp.sum(wk, axis=0, keepdims=True)
            m_ref[h:h + 1, :] = jnp.broadcast_to(m_new, (1, LANES))

            hn = hout * lax.rsqrt(jnp.mean(hout * hout, axis=-1, keepdims=True) + LN_EPS)
            y_ref[r0:r0 + CHUNK, h * A_DV:(h + 1) * A_DV] = (
                hn * ng_ref[:, h * A_DV:(h + 1) * A_DV] * _sigmoid(oh))


def _mlstm_mix(z, gt, conv_w, gate_b, norm_g, tt=128):
    t = z.shape[0]
    nqk2 = 2 * A_HEADS * A_DK
    conv_p = jnp.zeros((8, nqk2), F32).at[0:4].set(conv_w.astype(F32))
    gb = gate_b.astype(F32)
    gb_row = jnp.zeros((1, LANES), F32).at[0, 0:2 * A_HEADS].set(gb)
    gb_col = jnp.zeros((16, LANES), F32).at[0:2 * A_HEADS, :].set(jnp.broadcast_to(gb[:, None], (2 * A_HEADS, LANES)))
    ng = norm_g.astype(F32).reshape(1, A_HEADS * A_DV)
    return pl.pallas_call(
        _mlstm_kernel, grid=(t // tt,),
        in_specs=[pl.BlockSpec((tt, A_PROJ_PAD), lambda i: (i, 0)),
                  pl.BlockSpec((tt // CHUNK, 16, CHUNK), lambda i: (i, 0, 0)),
                  _const_spec((8, nqk2)), _const_spec((1, LANES)), _const_spec((16, LANES)),
                  _const_spec((1, A_HEADS * A_DV))],
        out_specs=pl.BlockSpec((tt, A_HEADS * A_DV), lambda i: (i, 0)),
        out_shape=jax.ShapeDtypeStruct((t, A_HEADS * A_DV), F32),
        scratch_shapes=[pltpu.VMEM((tt + 8, nqk2), F32),
                        pltpu.VMEM((A_HEADS, A_DV, A_DK), F32),
                        pltpu.VMEM((8, A_DK), F32),
                        pltpu.VMEM((8, LANES), F32)],
        compiler_params=_cparams("arbitrary"), name="mlstm_mix")(z, gt, conv_p, gb_row, gb_col, ng)


def _decay_attention_chunk(q, k, v, la, st_ref, heads, dv):
    dk = LANES
    b = jnp.dot(_tri(CHUNK, True).astype(F32), la, preferred_element_type=F32,
                precision=lax.Precision.HIGHEST)
    b_last = b[CHUNK - 1:CHUNK, :]
    qe = q * jnp.exp(b)
    kd = k * jnp.exp(b_last - b)
    row_in_block = lax.broadcasted_iota(jnp.int32, (CHUNK, 1), 0) % SUB_BLOCK

    outs = [_bdot_nt(qe[:, h * dk:(h + 1) * dk], st_ref[h]) for h in range(heads)]

    for delta in range(SUB_BLOCK):
        if delta == 0:
            p = q * k
            vs = v
        else:
            valid = row_in_block >= delta
            arg = jnp.where(valid, b - pltpu.roll(b, delta, 0), 0.0)
            p = q * pltpu.roll(k, delta, 0) * jnp.exp(arg)
            vs = pltpu.roll(v, delta, 0)
        for h in range(heads):
            w = jnp.sum(p[:, h * dk:(h + 1) * dk], axis=-1, keepdims=True)
            if delta:
                w = jnp.where(valid, w, 0.0)
            outs[h] = outs[h] + w * vs[:, h * dv:(h + 1) * dv]

    for h in range(heads):
        pieces = [jnp.zeros((SUB_BLOCK, dv), F32)]
        for j in range(1, CHUNK // SUB_BLOCK):
            lo = j * SUB_BLOCK
            ref_row = b[lo - 1:lo, h * dk:(h + 1) * dk]
            qj = q[lo:lo + SUB_BLOCK, h * dk:(h + 1) * dk] * jnp.exp(b[lo:lo + SUB_BLOCK, h * dk:(h + 1) * dk] - ref_row)
            kj = k[0:lo, h * dk:(h + 1) * dk] * jnp.exp(ref_row - b[0:lo, h * dk:(h + 1) * dk])
            att = _bdot_nt(qj, kj)
            pieces.append(_bdot(att, v[0:lo, h * dv:(h + 1) * dv]))
        outs[h] = outs[h] + jnp.concatenate(pieces, axis=0)
        st_ref[h] = (st_ref[h] * jnp.exp(b_last[:, h * dk:(h + 1) * dk])
                     + _bdot_tn(v[:, h * dv:(h + 1) * dv], kd[:, h * dk:(h + 1) * dk]))
    return outs


def _head_norm_store(outs, y_ref, r0, ng_ref, gate_act, dv):
    for h, o in enumerate(outs):
        hn = o * lax.rsqrt(jnp.mean(o * o, axis=-1, keepdims=True) + LN_EPS)
        y_ref[r0:r0 + CHUNK, h * dv:(h + 1) * dv] = (
            hn * ng_ref[:, h * dv:(h + 1) * dv] * gate_act[:, h * dv:(h + 1) * dv])


def _gla_kernel(z_ref, wa2_ref, ba_ref, ng_ref, y_ref, st_ref):
    @pl.when(pl.program_id(0) == 0)
    def _():
        st_ref[...] = jnp.zeros(st_ref.shape, F32)

    nk, nv = B_HEADS * B_DK, B_HEADS * B_DV
    for c in range(z_ref.shape[0] // CHUNK):
        r0 = c * CHUNK
        q = z_ref[r0:r0 + CHUNK, 0:nk] * (B_DK ** -0.5)
        k = z_ref[r0:r0 + CHUNK, nk:2 * nk]
        v = z_ref[r0:r0 + CHUNK, 2 * nk:2 * nk + nv]
        g = z_ref[r0:r0 + CHUNK, 2 * nk + nv:2 * nk + 2 * nv]
        a_lr = z_ref[r0:r0 + CHUNK, B_MAIN:B_MAIN + LANES]
        la = _log_sigmoid(jnp.dot(a_lr.astype(BF16), wa2_ref[...], preferred_element_type=F32)
                          + ba_ref[...]) / B_TAU
        outs = _decay_attention_chunk(q, k, v, la, st_ref, B_HEADS, B_DV)
        _head_norm_store(outs, y_ref, r0, ng_ref, _silu(g), B_DV)


def _gla_mix(z, w_a2, b_a, norm_g, tt=128):
    t = z.shape[0]
    nk, nv = B_HEADS * B_DK, B_HEADS * B_DV
    wa2 = jnp.zeros((LANES, nk), BF16).at[0:B_RANK].set(w_a2.astype(BF16))
    return pl.pallas_call(
        _gla_kernel, grid=(t // tt,),
        in_specs=[pl.BlockSpec((tt, B_PROJ_PAD), lambda i: (i, 0)),
                  _const_spec((LANES, nk)), _const_spec((1, nk)), _const_spec((1, nv))],
        out_specs=pl.BlockSpec((tt, nv), lambda i: (i, 0)),
        out_shape=jax.ShapeDtypeStruct((t, nv), F32),
        scratch_shapes=[pltpu.VMEM((B_HEADS, B_DV, B_DK), F32)],
        compiler_params=_cparams("arbitrary"), name="gla_mix")(
            z, wa2, b_a.astype(F32).reshape(1, nk), norm_g.astype(F32).reshape(1, nv))


def _hgrn_kernel(layer, z_ref, lbp_ref, ng_ref, y_ref, st_ref):
    @pl.when(pl.program_id(0) == 0)
    def _():
        st_ref[...] = jnp.zeros(st_ref.shape, F32)

    lbp = lbp_ref[0:DEPTH, :]
    e = jnp.exp(lbp - jnp.max(lbp, axis=0, keepdims=True))
    sm = e / jnp.sum(e, axis=0, keepdims=True)
    lb = jnp.zeros((1, sm.shape[1]), F32)
    for r in range(layer + 1):
        lb = lb + sm[r:r + 1, :]
    lb = lb - sm[0:1, :]

    nk, nv = C_HEADS * C_DK, C_HEADS * C_DV
    for c in range(z_ref.shape[0] // CHUNK):
        r0 = c * CHUNK
        q = _silu(z_ref[r0:r0 + CHUNK, 0:nk])
        fg = lb + (1.0 - lb) * _sigmoid(z_ref[r0:r0 + CHUNK, nk:2 * nk])
        v = z_ref[r0:r0 + CHUNK, 2 * nk:2 * nk + nv]
        g = z_ref[r0:r0 + CHUNK, 2 * nk + nv:2 * nk + 2 * nv]
        outs = _decay_attention_chunk(q, 1.0 - fg, v, jnp.log(fg), st_ref, C_HEADS, C_DV)
        _head_norm_store(outs, y_ref, r0, ng_ref, _sigmoid(g), C_DV)


def _hgrn_mix(z, hgrn_lb, layer, norm_g, tt=128):
    t = z.shape[0]
    nk, nv = C_HEADS * C_DK, C_HEADS * C_DV
    lbp = jnp.zeros((8, nk), F32).at[0:DEPTH].set(hgrn_lb.astype(F32))
    return pl.pallas_call(
        functools.partial(_hgrn_kernel, layer), grid=(t // tt,),
        in_specs=[pl.BlockSpec((tt, C_PROJ), lambda i: (i, 0)),
                  _const_spec((8, nk)), _const_spec((1, nv))],
        out_specs=pl.BlockSpec((tt, nv), lambda i: (i, 0)),
        out_shape=jax.ShapeDtypeStruct((t, nv), F32),
        scratch_shapes=[pltpu.VMEM((C_HEADS, C_DV, C_DK), F32)],
        compiler_params=_cparams("arbitrary"), name="hgrn_mix")(
            z, lbp, norm_g.astype(F32).reshape(1, nv))


def _route(lt, rb_ref):
    rows = [lt[e:e + 1, :] for e in range(N_EXPERTS)]
    mx = rows[0]
    for r in rows[1:]:
        mx = jnp.maximum(mx, r)
    ex = [jnp.exp(r - mx) for r in rows]
    tot = ex[0]
    for r in ex[1:]:
        tot = tot + r
    scores = [r / tot for r in ex]
    sel = [scores[e] + rb_ref[e:e + 1, 0:1] for e in range(N_EXPERTS)]

    gscore = []
    for g in range(N_GROUPS):
        m = sel[g * EXPERTS_PER_GROUP:(g + 1) * EXPERTS_PER_GROUP]
        best = None
        for a in range(EXPERTS_PER_GROUP):
            for b2 in range(a + 1, EXPERTS_PER_GROUP):
                pair = m[a] + m[b2]
                best = pair if best is None else jnp.maximum(best, pair)
        gscore.append(best)
    g_best = gscore[0]
    g_idx = jnp.zeros_like(g_best, dtype=jnp.int32)
    for g in range(1, N_GROUPS):
        better = gscore[g] > g_best
        g_best = jnp.where(better, gscore[g], g_best)
        g_idx = jnp.where(better, g, g_idx)

    masked = [jnp.where(g_idx == (e // EXPERTS_PER_GROUP), sel[e], -jnp.inf) for e in range(N_EXPERTS)]

    def first_argmax(vals):
        best_v, best_i = vals[0], jnp.zeros_like(g_idx)
        for e in range(1, N_EXPERTS):
            better = vals[e] > best_v
            best_v = jnp.where(better, vals[e], best_v)
            best_i = jnp.where(better, e, best_i)
        return best_i

    i1 = first_argmax(masked)
    i2 = first_argmax([jnp.where(i1 == e, -jnp.inf, masked[e]) for e in range(N_EXPERTS)])
    w1 = jnp.zeros_like(mx)
    w2 = jnp.zeros_like(mx)
    for e in range(N_EXPERTS):
        w1 = jnp.where(i1 == e, scores[e], w1)
        w2 = jnp.where(i2 == e, scores[e], w2)
    wsum = w1 + w2
    return i1, i2, w1 / wsum, w2 / wsum


def _outproj_kernel(y_ref, x_ref, w_ref, g_ref, b_ref, rw_ref, rb_ref,
                    x1_ref, gate_ref, meta_ref, cnt_ref, gt_scr, carry_ref):
    tm = x_ref.shape[0]

    @pl.when(pl.program_id(0) == 0)
    def _():
        carry_ref[...] = jnp.zeros(carry_ref.shape, F32)

    mix = jnp.dot(y_ref[...].astype(BF16), w_ref[...], preferred_element_type=F32)
    x1 = _layer_norm(DEEPNORM_ALPHA * x_ref[...] + mix, g_ref[...], b_ref[...])
    x1_ref[...] = x1
    logits = jnp.dot(x1, rw_ref[...], preferred_element_type=F32, precision=lax.Precision.HIGHEST)
    i1, i2, w1, w2 = _route(logits.T, rb_ref)

    lo = jnp.minimum(i1, i2)
    hi = jnp.maximum(i1, i2)
    a = lo & (EXPERTS_PER_GROUP - 1)
    b2 = hi & (EXPERTS_PER_GROUP - 1)
    bucket = (lo >> 2) * PAIRS_PER_GROUP + ((a * (7 - a)) >> 1) + (b2 - a - 1)

    onehot = lax.broadcasted_iota(jnp.int32, (BUCKET_ROWS, tm), 0) == bucket
    r = lax.broadcasted_iota(jnp.int32, (tm, tm), 0)
    c = lax.broadcasted_iota(jnp.int32, (tm, tm), 1)
    before = jnp.dot(onehot.astype(BF16), (r < c).astype(BF16), preferred_element_type=F32)
    rank = jnp.sum(jnp.where(onehot, before + carry_ref[:, 0:1], 0.0), axis=0, keepdims=True)
    carry_ref[...] = carry_ref[...] + jnp.sum(onehot.astype(F32), axis=1, keepdims=True)
    cnt_ref[...] = carry_ref[...].astype(jnp.int32)

    meta_ref[...] = jnp.zeros(meta_ref.shape, jnp.int32)
    meta_ref[0:1, :] = bucket
    meta_ref[1:2, :] = rank.astype(jnp.int32)

    gt_scr[...] = jnp.zeros(gt_scr.shape, F32)
    gt_scr[0:1, :] = jnp.where(i1 == lo, w1, w2)
    gt_scr[1:2, :] = jnp.where(i1 == lo, w2, w1)
    gate_ref[...] = gt_scr[...].T


def _outproj_ln_route(y, x, w_out, ln_g, ln_b, router_w, router_b, tm=512):
    t, d = x.shape
    rw = jnp.zeros((d, LANES), F32).at[:, 0:N_EXPERTS].set(router_w.astype(F32))
    rb = jnp.broadcast_to(router_b.astype(F32)[:, None], (N_EXPERTS, LANES))
    tile = pl.BlockSpec((tm, d), lambda i: (i, 0))
    return pl.pallas_call(
        _outproj_kernel, grid=(t // tm,),
        in_specs=[pl.BlockSpec((tm, y.shape[1]), lambda i: (i, 0)), tile,
                  _const_spec(w_out.shape), _const_spec((1, d)), _const_spec((1, d)),
                  _const_spec((d, LANES)), _const_spec((N_EXPERTS, LANES))],
        out_specs=[tile, pl.BlockSpec((tm, LANES), lambda i: (i, 0)),
                   pl.BlockSpec((8, tm), lambda i: (0, i)), _const_spec((BUCKET_ROWS, LANES))],
        out_shape=[jax.ShapeDtypeStruct((t, d), F32), jax.ShapeDtypeStruct((t, LANES), F32),
                   jax.ShapeDtypeStruct((8, t), jnp.int32),
                   jax.ShapeDtypeStruct((BUCKET_ROWS, LANES), jnp.int32)],
        scratch_shapes=[pltpu.VMEM((LANES, tm), F32), pltpu.VMEM((BUCKET_ROWS, LANES), F32)],
        compiler_params=_cparams("arbitrary"), name="outproj_ln_route")(
            y, x, w_out, ln_g.astype(F32).reshape(1, d), ln_b.astype(F32).reshape(1, d), rw, rb)


def _moe_plan(meta, cnt, tile_rows, n_tiles):
    counts = cnt[0:N_BUCKETS, 0]
    tiles = (counts + tile_rows - 1) // tile_rows
    tile_end = jnp.cumsum(tiles)
    offs = (tile_end - tiles) * tile_rows
    dest = (offs[meta[0]] + meta[1]).astype(jnp.int32)
    n_used = tile_end[-1]
    j = jnp.minimum(jnp.arange(n_tiles, dtype=jnp.int32), n_used - 1)
    bucket_of_tile = jnp.minimum(jnp.searchsorted(tile_end, j, side="right"), N_BUCKETS - 1)
    pairs = [(a, b) for a in range(EXPERTS_PER_GROUP) for b in range(a + 1, EXPERTS_PER_GROUP)]
    lo_tab = jnp.array([g * EXPERTS_PER_GROUP + a for g in range(N_GROUPS) for a, _ in pairs], jnp.int32)
    hi_tab = jnp.array([g * EXPERTS_PER_GROUP + b for g in range(N_GROUPS) for _, b in pairs], jnp.int32)
    return dest, lo_tab[bucket_of_tile], hi_tab[bucket_of_tile], n_used.astype(jnp.int32).reshape(1)


def _dispatch_kernel(dest_ref, x1_ref, gate_ref, xs_in_ref, xs_ref, comb_ref, sem):
    del xs_in_ref
    tm = x1_ref.shape[0]
    base = pl.program_id(0) * tm
    comb_ref[:, 0:D_MODEL] = x1_ref[...]
    comb_ref[:, D_MODEL:] = gate_ref[...]

    def issue(r, carry):
        pltpu.make_async_copy(comb_ref.at[pl.ds(r, 1)], xs_ref.at[pl.ds(dest_ref[base + r], 1)], sem).start()
        return carry

    lax.fori_loop(0, tm, issue, 0, unroll=8)

    def drain(r, carry):
        pltpu.make_async_copy(comb_ref.at[pl.ds(r, 1)], xs_ref.at[pl.ds(dest_ref[base + r], 1)], sem).wait()
        return carry

    lax.fori_loop(0, tm, drain, 0, unroll=8)


def _dispatch(dest, x1, gate, xs, tm=256):
    t, d = x1.shape
    return pl.pallas_call(
        _dispatch_kernel,
        grid_spec=pltpu.PrefetchScalarGridSpec(
            num_scalar_prefetch=1, grid=(t // tm,),
            in_specs=[pl.BlockSpec((tm, d), lambda i, dest: (i, 0)),
                      pl.BlockSpec((tm, LANES), lambda i, dest: (i, 0)),
                      pl.BlockSpec(memory_space=pl.ANY)],
            out_specs=pl.BlockSpec(memory_space=pl.ANY),
            scratch_shapes=[pltpu.VMEM((tm, d + LANES), F32), pltpu.SemaphoreType.DMA(())]),
        out_shape=jax.ShapeDtypeStruct(xs.shape, F32), input_output_aliases={3: 0},
        compiler_params=_cparams("arbitrary"), name="moe_dispatch")(dest, x1, gate, xs)


def _expert(x, wg_ref, wu_ref, wd_ref):
    hg = jnp.dot(x, wg_ref[0], preferred_element_type=F32)
    hu = jnp.dot(x, wu_ref[0], preferred_element_type=F32)
    return jnp.dot((_silu(hg) * hu).astype(BF16), wd_ref[0], preferred_element_type=F32)


def _experts_kernel(lo_ref, hi_ref, nused_ref, xs_ref, wg0, wu0, wd0, wg1, wu1, wd1, ys_ref):
    del lo_ref, hi_ref

    @pl.when(pl.program_id(0) < nused_ref[0])
    def _():
        x = xs_ref[:, 0:D_MODEL].astype(BF16)
        g = xs_ref[:, D_MODEL:]
        ys_ref[...] = g[:, 0:1] * _expert(x, wg0, wu0, wd0) + g[:, 1:2] * _expert(x, wg1, wu1, wd1)

    @pl.when(pl.program_id(0) >= nused_ref[0])
    def _():
        ys_ref[...] = jnp.zeros(ys_ref.shape, F32)


def _experts(tile_lo, tile_hi, n_used, xs, wg, wu, wd, tm):
    rows, cols = xs.shape
    d = D_MODEL
    rows_map = lambda j, lo, hi, n: (jnp.minimum(j, n[0] - 1), 0)
    w_in = lambda which: pl.BlockSpec((1, d, D_EXPERT), lambda j, lo, hi, n: ((lo, hi)[which][j], 0, 0))
    w_out = lambda which: pl.BlockSpec((1, D_EXPERT, d), lambda j, lo, hi, n: ((lo, hi)[which][j], 0, 0))
    return pl.pallas_call(
        _experts_kernel,
        grid_spec=pltpu.PrefetchScalarGridSpec(
            num_scalar_prefetch=3, grid=(rows // tm,),
            in_specs=[pl.BlockSpec((tm, cols), rows_map),
                      w_in(0), w_in(0), w_out(0), w_in(1), w_in(1), w_out(1)],
            out_specs=pl.BlockSpec((tm, d), lambda j, lo, hi, n: (j, 0))),
        out_shape=jax.ShapeDtypeStruct((rows, d), F32),
        compiler_params=_cparams("arbitrary"), name="moe_experts")(
            tile_lo, tile_hi, n_used, xs, wg, wu, wd, wg, wu, wd)


def _combine_kernel(dest_ref, x1_ref, p_ref, ys_ref, g_ref, b_ref, wpg_ref, wpp_ref, o_ref, ybuf, sems):
    i = pl.program_id(0)
    tm = x1_ref.shape[0]
    slot = i % 2

    def row_copy(tile, s, r):
        return pltpu.make_async_copy(ys_ref.at[pl.ds(dest_ref[tile * tm + r], 1)],
                                     ybuf.at[s].at[pl.ds(r, 1)], sems.at[s])

    def start_tile(tile, s):
        def issue(r, carry):
            row_copy(tile, s, r).start()
            return carry
        lax.fori_loop(0, tm, issue, 0, unroll=8)

    @pl.when(i == 0)
    def _():
        start_tile(0, 0)

    @pl.when(i + 1 < pl.num_programs(0))
    def _():
        start_tile(i + 1, 1 - slot)

    def drain(r, carry):
        row_copy(i, slot, r).wait()
        return carry

    lax.fori_loop(0, tm, drain, 0, unroll=8)
    x2 = _layer_norm(DEEPNORM_ALPHA * x1_ref[...] + ybuf[slot], g_ref[...], b_ref[...])
    gate = _sigmoid(jnp.dot(x2.astype(BF16), wpg_ref[...], preferred_element_type=F32))
    o_ref[...] = x2 + gate * jnp.dot(p_ref[...].astype(BF16), wpp_ref[...], preferred_element_type=F32)


def _combine_ln_ple(dest, x1, p, ys, ln_g, ln_b, w_gate, w_proj, tm=256):
    t, d = x1.shape
    const = lambda shape: pl.BlockSpec(shape, lambda i, dest: (0,) * len(shape))
    tile = pl.BlockSpec((tm, d), lambda i, dest: (i, 0))
    return pl.pallas_call(
        _combine_kernel,
        grid_spec=pltpu.PrefetchScalarGridSpec(
            num_scalar_prefetch=1, grid=(t // tm,),
            in_specs=[tile, pl.BlockSpec((tm, PLE_DIM), lambda i, dest: (i, 0)),
                      pl.BlockSpec(memory_space=pl.ANY), const((1, d)), const((1, d)),
                      const((d, d)), const((PLE_DIM, d))],
            out_specs=tile,
            scratch_shapes=[pltpu.VMEM((2, tm, d), F32), pltpu.SemaphoreType.DMA((2,))]),
        out_shape=jax.ShapeDtypeStruct((t, d), F32),
        compiler_params=_cparams("arbitrary"), name="moe_combine_ln_ple")(
            dest, x1, p, ys, ln_g.astype(F32).reshape(1, d), ln_b.astype(F32).reshape(1, d), w_gate, w_proj)


def _pad_cols(w, n):
    return jnp.zeros((w.shape[0], n), w.dtype).at[:, 0:w.shape[1]].set(w)


def kernel(x, p, ln1_g, ln1_b, ln2_g, ln2_b, mlstm_w_in, mlstm_conv, mlstm_gate_b, mlstm_norm_g, mlstm_w_out, gla_w_in, gla_w_a2, gla_b_a, gla_norm_g, gla_w_out, hgrn_w_in, hgrn_lb, hgrn_norm_g, hgrn_w_out, router_w, router_b, exp_w_gate, exp_w_up, exp_w_down, ple_w_proj, ple_w_gate):
    bsz, seq, d = x.shape
    assert bsz == 1 and d == D_MODEL and seq % 1024 == 0
    xt = x.reshape(seq, d).astype(F32)
    n_tiles = seq // MOE_TILE + N_BUCKETS
    xs = jnp.zeros((n_tiles * MOE_TILE, d + LANES), F32)
    for i in range(DEPTH):
        kind, slot = i % N_MIXERS, i // N_MIXERS
        if kind == 0:
            w_in = mlstm_w_in[slot]
            w = _pad_cols(w_in.astype(BF16), A_PROJ_PAD)
            wgt = jnp.zeros((16, d), BF16).at[0:2 * A_HEADS].set(w_in[:, A_MAIN:].T.astype(BF16))
            z, gt = _inproj(xt, w, wgt)
            y = _mlstm_mix(z, gt, mlstm_conv[slot], mlstm_gate_b[slot], mlstm_norm_g[slot])
            w_out = mlstm_w_out[slot]
        elif kind == 1:
            z = _inproj(xt, _pad_cols(gla_w_in[slot].astype(BF16), B_PROJ_PAD))
            y = _gla_mix(z, gla_w_a2[slot], gla_b_a[slot], gla_norm_g[slot])
            w_out = gla_w_out[slot]
        else:
            z = _inproj(xt, hgrn_w_in[slot].astype(BF16))
            y = _hgrn_mix(z, hgrn_lb, i, hgrn_norm_g[slot])
            w_out = hgrn_w_out[slot]
        x1, gate, meta, cnt = _outproj_ln_route(y, xt, w_out.astype(BF16), ln1_g[i], ln1_b[i],
                                                router_w, router_b)
        dest, tile_lo, tile_hi, n_used = _moe_plan(meta, cnt, MOE_TILE, n_tiles)
        xs = _dispatch(dest, x1, gate, xs)
        ys = _experts(tile_lo, tile_hi, n_used, xs, exp_w_gate[i].astype(BF16), exp_w_up[i].astype(BF16),
                      exp_w_down[i].astype(BF16), MOE_TILE)
        xt = _combine_ln_ple(dest, x1, p[i].reshape(seq, PLE_DIM), ys, ln2_g[i], ln2_b[i],
                             ple_w_gate[i].astype(BF16), ple_w_proj[i].astype(BF16))
    return xt.reshape(bsz, seq, d)
```

```python
import functools

import jax
import jax.numpy as jnp
from jax import lax
from jax.experimental import pallas as pl
from jax.experimental.pallas import tpu as pltpu

F32 = jnp.float32
BF16 = jnp.bfloat16

D_MODEL = 1024
DEPTH = 4
CHUNK = 64
PLE_DIM = 256
N_MIXERS = 3
LN_EPS = 1e-5
DEEPNORM_ALPHA = (2 * DEPTH) ** 0.25
A_HEADS, A_DK, A_DV = 4, 128, 256
B_HEADS, B_DK, B_DV = 4, 128, 256
B_RANK = 16
B_TAU = 16.0
C_HEADS, C_DK, C_DV = 8, 128, 128
N_EXPERTS = 16
N_GROUPS = 4
EXPERTS_PER_GROUP = 4
D_EXPERT = 512
PAIRS_PER_GROUP = EXPERTS_PER_GROUP * (EXPERTS_PER_GROUP - 1) // 2
N_BUCKETS = N_GROUPS * PAIRS_PER_GROUP
BUCKET_ROWS = 32
MOE_TILE = 256

LANES = 128
SLOT = 8
N_SLOTS = CHUNK // SLOT
A_MAIN = 2 * A_HEADS * A_DK + 2 * A_HEADS * A_DV
A_PROJ_PAD = A_MAIN + LANES
B_MAIN = 2 * B_HEADS * B_DK + 2 * B_HEADS * B_DV
B_PROJ_PAD = B_MAIN + LANES
C_PROJ = 2 * C_HEADS * C_DK + 2 * C_HEADS * C_DV
VMEM_LIMIT = 56 * 1024 * 1024

NT_DIMS = (((1,), (1,)), ((), ()))
TN_DIMS = (((0,), (0,)), ((), ()))


def _cparams(*sem):
    return pltpu.CompilerParams(dimension_semantics=sem, vmem_limit_bytes=VMEM_LIMIT)


def _sigmoid(x):
    return 1.0 / (1.0 + jnp.exp(-x))


def _silu(x):
    return x * _sigmoid(x)


def _log_sigmoid(x):
    return jnp.minimum(x, 0.0) - jnp.log(1.0 + jnp.exp(-jnp.abs(x)))


def _layer_norm(u, g, b):
    mu = jnp.mean(u, axis=-1, keepdims=True)
    d = u - mu
    var = jnp.mean(d * d, axis=-1, keepdims=True)
    return d * lax.rsqrt(var + LN_EPS) * g + b


def _tri(n, lower):
    r = lax.broadcasted_iota(jnp.int32, (n, n), 0)
    c = lax.broadcasted_iota(jnp.int32, (n, n), 1)
    return (r >= c) if lower else (r <= c)


def _bdot(a, b):
    return jnp.dot(a.astype(BF16), b.astype(BF16), preferred_element_type=F32)


def _bdot_nt(a, b):
    return lax.dot_general(a.astype(BF16), b.astype(BF16), NT_DIMS, preferred_element_type=F32)


def _bdot_tn(a, b):
    return lax.dot_general(a.astype(BF16), b.astype(BF16), TN_DIMS, preferred_element_type=F32)


def _const_spec(shape):
    return pl.BlockSpec(shape, lambda i: (0,) * len(shape))


def _inproj_kernel(x_ref, w_ref, z_ref):
    z_ref[...] = jnp.dot(x_ref[...].astype(BF16), w_ref[...], preferred_element_type=F32)


def _inproj_gates_kernel(x_ref, w_ref, wgt_ref, z_ref, gt_ref):
    xb = x_ref[...].astype(BF16)
    z_ref[...] = jnp.dot(xb, w_ref[...], preferred_element_type=F32)
    for c in range(gt_ref.shape[0]):
        gt_ref[c] = lax.dot_general(wgt_ref[...], xb[c * CHUNK:(c + 1) * CHUNK], NT_DIMS,
                                    preferred_element_type=F32)


def _inproj(x, w, wgt=None, tm=512):
    t, d = x.shape
    n = w.shape[1]
    grid = (t // tm,)
    x_spec = pl.BlockSpec((tm, d), lambda i: (i, 0))
    z_spec = pl.BlockSpec((tm, n), lambda i: (i, 0))
    if wgt is None:
        return pl.pallas_call(
            _inproj_kernel, grid=grid, in_specs=[x_spec, _const_spec((d, n))], out_specs=z_spec,
            out_shape=jax.ShapeDtypeStruct((t, n), F32), compiler_params=_cparams("parallel"),
            name="inproj")(x, w)
    return pl.pallas_call(
        _inproj_gates_kernel, grid=grid,
        in_specs=[x_spec, _const_spec((d, n)), _const_spec(wgt.shape)],
        out_specs=[z_spec, pl.BlockSpec((tm // CHUNK, 16, CHUNK), lambda i: (i, 0, 0))],
        out_shape=[jax.ShapeDtypeStruct((t, n), F32),
                   jax.ShapeDtypeStruct((t // CHUNK, 16, CHUNK), F32)],
        compiler_params=_cparams("parallel"), name="inproj_gates")(x, w, wgt)


def _mlstm_kernel(z_ref, gt_ref, conv_ref, gbr_ref, gbc_ref, ng_ref, y_ref,
                  ext_ref, ct_ref, n_ref, m_ref):
    tt = z_ref.shape[0]
    nqk = A_HEADS * A_DK

    @pl.when(pl.program_id(0) == 0)
    def _():
        ext_ref[0:8, :] = jnp.zeros((8, 2 * nqk), F32)
        ct_ref[...] = jnp.zeros(ct_ref.shape, F32)
        n_ref[...] = jnp.zeros(n_ref.shape, F32)
        m_ref[...] = jnp.zeros(m_ref.shape, F32)

    qk_raw = z_ref[:, 0:2 * nqk]
    ext_ref[8:8 + tt, :] = qk_raw
    acc = conv_ref[3:4, :] * qk_raw
    for s in range(1, 4):
        acc = acc + conv_ref[3 - s:4 - s, :] * ext_ref[pl.ds(8 - s, tt), :]
    ext_ref[0:8, :] = qk_raw[tt - 8:tt, :]
    qkc = _silu(acc)

    tri_lo = _tri(CHUNK, True)
    tri_lo_f = tri_lo.astype(F32)
    tri_up_f = _tri(CHUNK, False).astype(F32)

    for c in range(tt // CHUNK):
        r0 = c * CHUNK
        gc = z_ref[r0:r0 + CHUNK, A_MAIN:A_MAIN + LANES] + gbr_ref[...]
        a_col = jnp.dot(tri_lo_f, _log_sigmoid(gc), preferred_element_type=F32,
                        precision=lax.Precision.HIGHEST)
        gr = gt_ref[c] + gbc_ref[:, 0:1]
        a_row = jnp.dot(_log_sigmoid(gr), tri_up_f, preferred_element_type=F32,
                        precision=lax.Precision.HIGHEST)
        for h in range(A_HEADS):
            ac = a_col[:, A_HEADS + h:A_HEADS + h + 1]
            lic = gc[:, h:h + 1]
            ar = a_row[A_HEADS + h:A_HEADS + h + 1, :]
            lir = gr[h:h + 1, :]
            m_st = m_ref[h:h + 1, 0:1]
            qh = qkc[r0:r0 + CHUNK, h * A_DK:(h + 1) * A_DK]
            kh = qkc[r0:r0 + CHUNK, nqk + h * A_DK:nqk + (h + 1) * A_DK] * (A_DK ** -0.5)
            vh = z_ref[r0:r0 + CHUNK, 2 * nqk + h * A_DV:2 * nqk + (h + 1) * A_DV]
            oh = z_ref[r0:r0 + CHUNK, 2 * nqk + A_HEADS * A_DV + h * A_DV:
                       2 * nqk + A_HEADS * A_DV + (h + 1) * A_DV]

            d_log = jnp.where(tri_lo, ac - ar + lir, -jnp.inf)
            d_max = jnp.max(d_log, axis=-1, keepdims=True)
            s = _bdot_nt(qh, kh) * jnp.exp(d_log - d_max)
            sv = _bdot(s, vh)
            s_sum = jnp.sum(s, axis=-1, keepdims=True)
            a_tot = ar[:, CHUNK - 1:CHUNK]
            g_max = jnp.max(a_tot - ar + lir, axis=-1, keepdims=True)
            wk = jnp.exp(a_tot - ac + lic - g_max) * kh
            c_inc = _bdot_tn(vh, wk)
            n_inc = jnp.sum(wk, axis=0, keepdims=True)

            e_log = ac + m_st
            m_row = jnp.maximum(e_log, d_max)
            w_inter = jnp.exp(e_log - m_row)
            w_intra = jnp.exp(d_max - m_row)
            num = w_inter * _bdot_nt(qh, ct_ref[h]) + w_intra * sv
            den = (w_inter * jnp.sum(qh * n_ref[h:h + 1, :], axis=-1, keepdims=True) + w_intra * s_sum)
            hout = num / jnp.maximum(jnp.abs(den), jnp.exp(-m_row))
            m_new = jnp.maximum(a_tot + m_st, g_max)
            decay = jnp.exp(a_tot + m_st - m_new)
            inc_scale = jnp.exp(g_max - m_new)
            ct_ref[h] = decay * ct_ref[h] + inc_scale * c_inc
            n_ref[h:h + 1, :] = decay * n_ref[h:h + 1, :] + inc_scale * n_inc
            m_ref[h:h + 1, :] = jnp.broadcast_to(m_new, (1, LANES))

            hn = hout * lax.rsqrt(jnp.mean(hout * hout, axis=-1, keepdims=True) + LN_EPS)
            y_ref[r0:r0 + CHUNK, h * A_DV:(h + 1) * A_DV] = (
                hn * ng_ref[:, h * A_DV:(h + 1) * A_DV] * _sigmoid(oh))


def _mlstm_mix(z, gt, conv_w, gate_b, norm_g, tt=256):
    t = z.shape[0]
    nqk2 = 2 * A_HEADS * A_DK
    conv_p = jnp.zeros((8, nqk2), F32).at[0:4].set(conv_w.astype(F32))
    gb = gate_b.astype(F32)
    gb_row = jnp.zeros((1, LANES), F32).at[0, 0:2 * A_HEADS].set(gb)
    gb_col = jnp.zeros((16, LANES), F32).at[0:2 * A_HEADS, :].set(jnp.broadcast_to(gb[:, None], (2 * A_HEADS, LANES)))
    ng = norm_g.astype(F32).reshape(1, A_HEADS * A_DV)
    return pl.pallas_call(
        _mlstm_kernel, grid=(t // tt,),
        in_specs=[pl.BlockSpec((tt, A_PROJ_PAD), lambda i: (i, 0)),
                  pl.BlockSpec((tt // CHUNK, 16, CHUNK), lambda i: (i, 0, 0)),
                  _const_spec((8, nqk2)), _const_spec((1, LANES)), _const_spec((16, LANES)),
                  _const_spec((1, A_HEADS * A_DV))],
        out_specs=pl.BlockSpec((tt, A_HEADS * A_DV), lambda i: (i, 0)),
        out_shape=jax.ShapeDtypeStruct((t, A_HEADS * A_DV), F32),
        scratch_shapes=[pltpu.VMEM((tt + 8, nqk2), F32),
                        pltpu.VMEM((A_HEADS, A_DV, A_DK), F32),
                        pltpu.VMEM((8, A_DK), F32),
                        pltpu.VMEM((8, LANES), F32)],
        compiler_params=_cparams("arbitrary"), name="mlstm_mix")(z, gt, conv_p, gb_row, gb_col, ng)


def _store_lane_tiles(ref, x):
    for t in range(ref.shape[0]):
        ref[t] = x[:, t * LANES:(t + 1) * LANES]


def _interleaved(ref):
    return jnp.concatenate(
        [jnp.concatenate([ref[t, pl.ds(c, N_SLOTS, stride=SLOT), :] for t in range(ref.shape[0])], axis=1)
         for c in range(SLOT)], axis=0)


def _pair_factors(q, k, b, ref_of_slot):
    zero = jnp.zeros((SLOT, q.shape[1]), F32)
    factors = []
    for half in (1, 2, 4):
        qs, ks = [], []
        for u in range(N_SLOTS):
            r = ref_of_slot((u // (2 * half)) * 2 * half + half - 1)
            rows = slice(u * SLOT, (u + 1) * SLOT)
            if u % (2 * half) >= half:
                qs.append(q[rows] * jnp.exp(b[rows] - r))
                ks.append(zero)
            else:
                qs.append(zero)
                ks.append(k[rows] * jnp.exp(r - b[rows]))
        factors.append((jnp.concatenate(qs, axis=0), jnp.concatenate(ks, axis=0)))
    return factors


def _pair_masks():
    r = lax.broadcasted_iota(jnp.int32, (CHUNK, CHUNK), 0)
    c = lax.broadcasted_iota(jnp.int32, (CHUNK, CHUNK), 1)
    sr, sc = r >> 3, c >> 3
    same_sublane = (r & (SLOT - 1)) == (c & (SLOT - 1))
    natural, interleaved = [], []
    for shift, half in ((1, 1), (2, 2), (3, 4)):
        m = jnp.logical_and((sr >> shift) == (sc >> shift),
                            jnp.logical_and((sr & half) != 0, (sc & half) == 0))
        natural.append(m)
        interleaved.append(jnp.logical_and(m, same_sublane))
    return r == c, natural, interleaved


def _decay_attention_chunk(q, k, v, la, st_ref, qkb_ref, vo_ref, heads, dv):
    dk = LANES
    b = jnp.dot(_tri(CHUNK, True).astype(F32), la, preferred_element_type=F32,
                precision=lax.Precision.HIGHEST)
    b_last = b[CHUNK - 1:CHUNK, :]
    qe = q * jnp.exp(b)
    kd = k * jnp.exp(b_last - b)

    _store_lane_tiles(qkb_ref.at[0], q)
    _store_lane_tiles(qkb_ref.at[1], k)
    _store_lane_tiles(qkb_ref.at[2], b)
    _store_lane_tiles(vo_ref.at[0], v)
    q_i, k_i, b_i = _interleaved(qkb_ref.at[0]), _interleaved(qkb_ref.at[1]), _interleaved(qkb_ref.at[2])
    v_i = _interleaved(vo_ref.at[0])

    eye, masks_n, masks_i = _pair_masks()
    fact_n = _pair_factors(q, k, b, lambda s: b[s * SLOT + SLOT - 1:(s + 1) * SLOT, :])
    fact_i = _pair_factors(q_i, k_i, b_i, lambda s: b_i[s * SLOT:(s + 1) * SLOT, :])

    outs = []
    for h in range(heads):
        kc = slice(h * dk, (h + 1) * dk)
        vc = slice(h * dv, (h + 1) * dv)
        att = jnp.where(eye, _bdot_nt(q[:, kc], k[:, kc]), 0.0)
        for (qf, kf), m in zip(fact_n, masks_n):
            att = att + jnp.where(m, _bdot_nt(qf[:, kc], kf[:, kc]), 0.0)
        att_i = jnp.zeros((CHUNK, CHUNK), F32)
        for (qf, kf), m in zip(fact_i, masks_i):
            att_i = att_i + jnp.where(m, _bdot_nt(qf[:, kc], kf[:, kc]), 0.0)
        outs.append(_bdot_nt(qe[:, kc], st_ref[h]) + _bdot(att, v[:, vc]))
        _store_lane_tiles(vo_ref.at[1, h * dv // LANES:(h + 1) * dv // LANES], _bdot(att_i, v_i[:, vc]))
        st_ref[h] = st_ref[h] * jnp.exp(b_last[:, kc]) + _bdot_tn(v[:, vc], kd[:, kc])
    o_i = _interleaved(vo_ref.at[1])
    return [o + o_i[:, h * dv:(h + 1) * dv] for h, o in enumerate(outs)]


def _head_norm_store(outs, y_ref, r0, ng_ref, gate_act, dv):
    for h, o in enumerate(outs):
        hn = o * lax.rsqrt(jnp.mean(o * o, axis=-1, keepdims=True) + LN_EPS)
        y_ref[r0:r0 + CHUNK, h * dv:(h + 1) * dv] = (
            hn * ng_ref[:, h * dv:(h + 1) * dv] * gate_act[:, h * dv:(h + 1) * dv])


def _gla_kernel(z_ref, wa2_ref, ba_ref, ng_ref, y_ref, st_ref, qkb_ref, vo_ref):
    @pl.when(pl.program_id(0) == 0)
    def _():
        st_ref[...] = jnp.zeros(st_ref.shape, F32)

    nk, nv = B_HEADS * B_DK, B_HEADS * B_DV
    for c in range(z_ref.shape[0] // CHUNK):
        r0 = c * CHUNK
        q = z_ref[r0:r0 + CHUNK, 0:nk] * (B_DK ** -0.5)
        k = z_ref[r0:r0 + CHUNK, nk:2 * nk]
        v = z_ref[r0:r0 + CHUNK, 2 * nk:2 * nk + nv]
        g = z_ref[r0:r0 + CHUNK, 2 * nk + nv:2 * nk + 2 * nv]
        a_lr = z_ref[r0:r0 + CHUNK, B_MAIN:B_MAIN + LANES]
        la = _log_sigmoid(jnp.dot(a_lr.astype(BF16), wa2_ref[...], preferred_element_type=F32)
                          + ba_ref[...]) / B_TAU
        outs = _decay_attention_chunk(q, k, v, la, st_ref, qkb_ref.at[c], vo_ref.at[c], B_HEADS, B_DV)
        _head_norm_store(outs, y_ref, r0, ng_ref, _silu(g), B_DV)


def _gla_mix(z, w_a2, b_a, norm_g, tt=128):
    t = z.shape[0]
    nk, nv = B_HEADS * B_DK, B_HEADS * B_DV
    wa2 = jnp.zeros((LANES, nk), BF16).at[0:B_RANK].set(w_a2.astype(BF16))
    return pl.pallas_call(
        _gla_kernel, grid=(t // tt,),
        in_specs=[pl.BlockSpec((tt, B_PROJ_PAD), lambda i: (i, 0)),
                  _const_spec((LANES, nk)), _const_spec((1, nk)), _const_spec((1, nv))],
        out_specs=pl.BlockSpec((tt, nv), lambda i: (i, 0)),
        out_shape=jax.ShapeDtypeStruct((t, nv), F32),
        scratch_shapes=[pltpu.VMEM((B_HEADS, B_DV, B_DK), F32),
                        pltpu.VMEM((tt // CHUNK, 3, nk // LANES, CHUNK, LANES), F32),
                        pltpu.VMEM((tt // CHUNK, 2, nv // LANES, CHUNK, LANES), F32)],
        compiler_params=_cparams("arbitrary"), name="gla_mix")(
            z, wa2, b_a.astype(F32).reshape(1, nk), norm_g.astype(F32).reshape(1, nv))


def _hgrn_kernel(layer, z_ref, lbp_ref, ng_ref, y_ref, st_ref, qkb_ref, vo_ref):
    @pl.when(pl.program_id(0) == 0)
    def _():
        st_ref[...] = jnp.zeros(st_ref.shape, F32)

    lbp = lbp_ref[0:DEPTH, :]
    e = jnp.exp(lbp - jnp.max(lbp, axis=0, keepdims=True))
    sm = e / jnp.sum(e, axis=0, keepdims=True)
    lb = jnp.zeros((1, sm.shape[1]), F32)
    for r in range(layer + 1):
        lb = lb + sm[r:r + 1, :]
    lb = lb - sm[0:1, :]

    nk, nv = C_HEADS * C_DK, C_HEADS * C_DV
    for c in range(z_ref.shape[0] // CHUNK):
        r0 = c * CHUNK
        q = _silu(z_ref[r0:r0 + CHUNK, 0:nk])
        fg = lb + (1.0 - lb) * _sigmoid(z_ref[r0:r0 + CHUNK, nk:2 * nk])
        v = z_ref[r0:r0 + CHUNK, 2 * nk:2 * nk + nv]
        g = z_ref[r0:r0 + CHUNK, 2 * nk + nv:2 * nk + 2 * nv]
        outs = _decay_attention_chunk(q, 1.0 - fg, v, jnp.log(fg), st_ref, qkb_ref.at[c], vo_ref.at[c],
                                      C_HEADS, C_DV)
        _head_norm_store(outs, y_ref, r0, ng_ref, _sigmoid(g), C_DV)


def _hgrn_mix(z, hgrn_lb, layer, norm_g, tt=128):
    t = z.shape[0]
    nk, nv = C_HEADS * C_DK, C_HEADS * C_DV
    lbp = jnp.zeros((8, nk), F32).at[0:DEPTH].set(hgrn_lb.astype(F32))
    return pl.pallas_call(
        functools.partial(_hgrn_kernel, layer), grid=(t // tt,),
        in_specs=[pl.BlockSpec((tt, C_PROJ), lambda i: (i, 0)),
                  _const_spec((8, nk)), _const_spec((1, nv))],
        out_specs=pl.BlockSpec((tt, nv), lambda i: (i, 0)),
        out_shape=jax.ShapeDtypeStruct((t, nv), F32),
        scratch_shapes=[pltpu.VMEM((C_HEADS, C_DV, C_DK), F32),
                        pltpu.VMEM((tt // CHUNK, 3, nk // LANES, CHUNK, LANES), F32),
                        pltpu.VMEM((tt // CHUNK, 2, nv // LANES, CHUNK, LANES), F32)],
        compiler_params=_cparams("arbitrary"), name="hgrn_mix")(
            z, lbp, norm_g.astype(F32).reshape(1, nv))


def _route(lt, rb_ref):
    rows = [lt[e:e + 1, :] for e in range(N_EXPERTS)]
    mx = rows[0]
    for r in rows[1:]:
        mx = jnp.maximum(mx, r)
    ex = [jnp.exp(r - mx) for r in rows]
    tot = ex[0]
    for r in ex[1:]:
        tot = tot + r
    scores = [r / tot for r in ex]
    sel = [scores[e] + rb_ref[e:e + 1, 0:1] for e in range(N_EXPERTS)]

    gscore = []
    for g in range(N_GROUPS):
        m = sel[g * EXPERTS_PER_GROUP:(g + 1) * EXPERTS_PER_GROUP]
        best = None
        for a in range(EXPERTS_PER_GROUP):
            for b2 in range(a + 1, EXPERTS_PER_GROUP):
                pair = m[a] + m[b2]
                best = pair if best is None else jnp.maximum(best, pair)
        gscore.append(best)
    g_best = gscore[0]
    g_idx = jnp.zeros_like(g_best, dtype=jnp.int32)
    for g in range(1, N_GROUPS):
        better = gscore[g] > g_best
        g_best = jnp.where(better, gscore[g], g_best)
        g_idx = jnp.where(better, g, g_idx)

    masked = [jnp.where(g_idx == (e // EXPERTS_PER_GROUP), sel[e], -jnp.inf) for e in range(N_EXPERTS)]

    def first_argmax(vals):
        best_v, best_i = vals[0], jnp.zeros_like(g_idx)
        for e in range(1, N_EXPERTS):
            better = vals[e] > best_v
            best_v = jnp.where(better, vals[e], best_v)
            best_i = jnp.where(better, e, best_i)
        return best_i

    i1 = first_argmax(masked)
    i2 = first_argmax([jnp.where(i1 == e, -jnp.inf, masked[e]) for e in range(N_EXPERTS)])
    w1 = jnp.zeros_like(mx)
    w2 = jnp.zeros_like(mx)
    for e in range(N_EXPERTS):
        w1 = jnp.where(i1 == e, scores[e], w1)
        w2 = jnp.where(i2 == e, scores[e], w2)
    wsum = w1 + w2
    return i1, i2, w1 / wsum, w2 / wsum


def _outproj_kernel(y_ref, x_ref, w_ref, g_ref, b_ref, rw_ref, rb_ref,
                    x1_ref, gate_ref, meta_ref, cnt_ref, gt_scr, carry_ref):
    tm = x_ref.shape[0]

    @pl.when(pl.program_id(0) == 0)
    def _():
        carry_ref[...] = jnp.zeros(carry_ref.shape, F32)

    mix = jnp.dot(y_ref[...].astype(BF16), w_ref[...], preferred_element_type=F32)
    x1 = _layer_norm(DEEPNORM_ALPHA * x_ref[...] + mix, g_ref[...], b_ref[...])
    x1_ref[...] = x1
    logits = jnp.dot(x1, rw_ref[...], preferred_element_type=F32, precision=lax.Precision.HIGHEST)
    i1, i2, w1, w2 = _route(logits.T, rb_ref)

    lo = jnp.minimum(i1, i2)
    hi = jnp.maximum(i1, i2)
    a = lo & (EXPERTS_PER_GROUP - 1)
    b2 = hi & (EXPERTS_PER_GROUP - 1)
    bucket = (lo >> 2) * PAIRS_PER_GROUP + ((a * (7 - a)) >> 1) + (b2 - a - 1)

    onehot = lax.broadcasted_iota(jnp.int32, (BUCKET_ROWS, tm), 0) == bucket
    r = lax.broadcasted_iota(jnp.int32, (tm, tm), 0)
    c = lax.broadcasted_iota(jnp.int32, (tm, tm), 1)
    before = jnp.dot(onehot.astype(BF16), (r < c).astype(BF16), preferred_element_type=F32)
    rank = jnp.sum(jnp.where(onehot, before + carry_ref[:, 0:1], 0.0), axis=0, keepdims=True)
    carry_ref[...] = carry_ref[...] + jnp.sum(onehot.astype(F32), axis=1, keepdims=True)
    cnt_ref[...] = carry_ref[...].astype(jnp.int32)

    meta_ref[...] = jnp.zeros(meta_ref.shape, jnp.int32)
    meta_ref[0:1, :] = bucket
    meta_ref[1:2, :] = rank.astype(jnp.int32)

    gt_scr[...] = jnp.zeros(gt_scr.shape, F32)
    gt_scr[0:1, :] = jnp.where(i1 == lo, w1, w2)
    gt_scr[1:2, :] = jnp.where(i1 == lo, w2, w1)
    gate_ref[...] = gt_scr[...].T


def _outproj_ln_route(y, x, w_out, ln_g, ln_b, router_w, router_b, tm=512):
    t, d = x.shape
    rw = jnp.zeros((d, LANES), F32).at[:, 0:N_EXPERTS].set(router_w.astype(F32))
    rb = jnp.broadcast_to(router_b.astype(F32)[:, None], (N_EXPERTS, LANES))
    tile = pl.BlockSpec((tm, d), lambda i: (i, 0))
    return pl.pallas_call(
        _outproj_kernel, grid=(t // tm,),
        in_specs=[pl.BlockSpec((tm, y.shape[1]), lambda i: (i, 0)), tile,
                  _const_spec(w_out.shape), _const_spec((1, d)), _const_spec((1, d)),
                  _const_spec((d, LANES)), _const_spec((N_EXPERTS, LANES))],
        out_specs=[tile, pl.BlockSpec((tm, LANES), lambda i: (i, 0)),
                   pl.BlockSpec((8, tm), lambda i: (0, i)), _const_spec((BUCKET_ROWS, LANES))],
        out_shape=[jax.ShapeDtypeStruct((t, d), F32), jax.ShapeDtypeStruct((t, LANES), F32),
                   jax.ShapeDtypeStruct((8, t), jnp.int32),
                   jax.ShapeDtypeStruct((BUCKET_ROWS, LANES), jnp.int32)],
        scratch_shapes=[pltpu.VMEM((LANES, tm), F32), pltpu.VMEM((BUCKET_ROWS, LANES), F32)],
        compiler_params=_cparams("arbitrary"), name="outproj_ln_route")(
            y, x, w_out, ln_g.astype(F32).reshape(1, d), ln_b.astype(F32).reshape(1, d), rw, rb)


def _moe_plan(meta, cnt, tile_rows, n_tiles):
    counts = cnt[0:N_BUCKETS, 0]
    tiles = (counts + tile_rows - 1) // tile_rows
    tile_end = jnp.cumsum(tiles)
    offs = (tile_end - tiles) * tile_rows
    dest = (offs[meta[0]] + meta[1]).astype(jnp.int32)
    n_used = tile_end[-1]
    j = jnp.minimum(jnp.arange(n_tiles, dtype=jnp.int32), n_used - 1)
    bucket_of_tile = jnp.minimum(jnp.sum(tile_end[None, :] <= j[:, None], axis=1), N_BUCKETS - 1)
    pairs = [(a, b) for a in range(EXPERTS_PER_GROUP) for b in range(a + 1, EXPERTS_PER_GROUP)]
    lo_tab = jnp.array([g * EXPERTS_PER_GROUP + a for g in range(N_GROUPS) for a, _ in pairs], jnp.int32)
    hi_tab = jnp.array([g * EXPERTS_PER_GROUP + b for g in range(N_GROUPS) for _, b in pairs], jnp.int32)
    return dest, lo_tab[bucket_of_tile], hi_tab[bucket_of_tile], n_used.astype(jnp.int32).reshape(1)


def _dispatch_kernel(dest_ref, x1_ref, gate_ref, xs_in_ref, xs_ref, comb_ref, sem):
    del xs_in_ref
    tm = x1_ref.shape[0]
    base = pl.program_id(0) * tm
    comb_ref[:, 0:D_MODEL] = x1_ref[...]
    comb_ref[:, D_MODEL:] = gate_ref[...]

    def issue(r, carry):
        pltpu.make_async_copy(comb_ref.at[pl.ds(r, 1)], xs_ref.at[pl.ds(dest_ref[base + r], 1)], sem).start()
        return carry

    lax.fori_loop(0, tm, issue, 0, unroll=8)

    def drain(r, carry):
        pltpu.make_async_copy(comb_ref.at[pl.ds(r, 1)], xs_ref.at[pl.ds(dest_ref[base + r], 1)], sem).wait()
        return carry

    lax.fori_loop(0, tm, drain, 0, unroll=8)


def _dispatch(dest, x1, gate, xs, tm=256):
    t, d = x1.shape
    return pl.pallas_call(
        _dispatch_kernel,
        grid_spec=pltpu.PrefetchScalarGridSpec(
            num_scalar_prefetch=1, grid=(t // tm,),
            in_specs=[pl.BlockSpec((tm, d), lambda i, dest: (i, 0)),
                      pl.BlockSpec((tm, LANES), lambda i, dest: (i, 0)),
                      pl.BlockSpec(memory_space=pl.ANY)],
            out_specs=pl.BlockSpec(memory_space=pl.ANY),
            scratch_shapes=[pltpu.VMEM((tm, d + LANES), F32), pltpu.SemaphoreType.DMA(())]),
        out_shape=jax.ShapeDtypeStruct(xs.shape, F32), input_output_aliases={3: 0},
        compiler_params=_cparams("arbitrary"), name="moe_dispatch")(dest, x1, gate, xs)


def _expert(x, wg_ref, wu_ref, wd_ref):
    hg = jnp.dot(x, wg_ref[0], preferred_element_type=F32)
    hu = jnp.dot(x, wu_ref[0], preferred_element_type=F32)
    return jnp.dot((_silu(hg) * hu).astype(BF16), wd_ref[0], preferred_element_type=F32)


def _experts_kernel(lo_ref, hi_ref, nused_ref, xs_ref, wg0, wu0, wd0, wg1, wu1, wd1, ys_ref):
    del lo_ref, hi_ref

    @pl.when(pl.program_id(0) < nused_ref[0])
    def _():
        x = xs_ref[:, 0:D_MODEL].astype(BF16)
        g = xs_ref[:, D_MODEL:]
        ys_ref[...] = g[:, 0:1] * _expert(x, wg0, wu0, wd0) + g[:, 1:2] * _expert(x, wg1, wu1, wd1)

    @pl.when(pl.program_id(0) >= nused_ref[0])
    def _():
        ys_ref[...] = jnp.zeros(ys_ref.shape, F32)


def _experts(tile_lo, tile_hi, n_used, xs, wg, wu, wd, tm):
    rows, cols = xs.shape
    d = D_MODEL
    rows_map = lambda j, lo, hi, n: (jnp.maximum(jnp.minimum(j, n[0] - 1), 0), 0)
    w_in = lambda which: pl.BlockSpec((1, d, D_EXPERT), lambda j, lo, hi, n: ((lo, hi)[which][j], 0, 0))
    w_out = lambda which: pl.BlockSpec((1, D_EXPERT, d), lambda j, lo, hi, n: ((lo, hi)[which][j], 0, 0))
    return pl.pallas_call(
        _experts_kernel,
        grid_spec=pltpu.PrefetchScalarGridSpec(
            num_scalar_prefetch=3, grid=(rows // tm,),
            in_specs=[pl.BlockSpec((tm, cols), rows_map),
                      w_in(0), w_in(0), w_out(0), w_in(1), w_in(1), w_out(1)],
            out_specs=pl.BlockSpec((tm, d), lambda j, lo, hi, n: (j, 0))),
        out_shape=jax.ShapeDtypeStruct((rows, d), F32),
        compiler_params=_cparams("arbitrary"), name="moe_experts")(
            tile_lo, tile_hi, n_used, xs, wg, wu, wd, wg, wu, wd)


def _combine_kernel(dest_ref, x1_ref, p_ref, ys_ref, g_ref, b_ref, wpg_ref, wpp_ref, o_ref, ybuf, sems):
    i = pl.program_id(0)
    tm = x1_ref.shape[0]
    slot = i % 2

    def row_copy(tile, s, r):
        return pltpu.make_async_copy(ys_ref.at[pl.ds(dest_ref[tile * tm + r], 1)],
                                     ybuf.at[s].at[pl.ds(r, 1)], sems.at[s])

    def start_tile(tile, s):
        def issue(r, carry):
            row_copy(tile, s, r).start()
            return carry
        lax.fori_loop(0, tm, issue, 0, unroll=8)

    @pl.when(i == 0)
    def _():
        start_tile(0, 0)

    @pl.when(i + 1 < pl.num_programs(0))
    def _():
        start_tile(i + 1, 1 - slot)

    def drain(r, carry):
        row_copy(i, slot, r).wait()
        return carry

    lax.fori_loop(0, tm, drain, 0, unroll=8)
    x2 = _layer_norm(DEEPNORM_ALPHA * x1_ref[...] + ybuf[slot], g_ref[...], b_ref[...])
    gate = _sigmoid(jnp.dot(x2.astype(BF16), wpg_ref[...], preferred_element_type=F32))
    o_ref[...] = x2 + gate * jnp.dot(p_ref[...].astype(BF16), wpp_ref[...], preferred_element_type=F32)


def _combine_ln_ple(dest, x1, p, ys, ln_g, ln_b, w_gate, w_proj, tm=256):
    t, d = x1.shape
    const = lambda shape: pl.BlockSpec(shape, lambda i, dest: (0,) * len(shape))
    tile = pl.BlockSpec((tm, d), lambda i, dest: (i, 0))
    return pl.pallas_call(
        _combine_kernel,
        grid_spec=pltpu.PrefetchScalarGridSpec(
            num_scalar_prefetch=1, grid=(t // tm,),
            in_specs=[tile, pl.BlockSpec((tm, PLE_DIM), lambda i, dest: (i, 0)),
                      pl.BlockSpec(memory_space=pl.ANY), const((1, d)), const((1, d)),
                      const((d, d)), const((PLE_DIM, d))],
            out_specs=tile,
            scratch_shapes=[pltpu.VMEM((2, tm, d), F32), pltpu.SemaphoreType.DMA((2,))]),
        out_shape=jax.ShapeDtypeStruct((t, d), F32),
        compiler_params=_cparams("arbitrary"), name="moe_combine_ln_ple")(
            dest, x1, p, ys, ln_g.astype(F32).reshape(1, d), ln_b.astype(F32).reshape(1, d), w_gate, w_proj)


def _pad_cols(w, n):
    return jnp.zeros((w.shape[0], n), w.dtype).at[:, 0:w.shape[1]].set(w)


def kernel(x, p, ln1_g, ln1_b, ln2_g, ln2_b, mlstm_w_in, mlstm_conv, mlstm_gate_b, mlstm_norm_g, mlstm_w_out, gla_w_in, gla_w_a2, gla_b_a, gla_norm_g, gla_w_out, hgrn_w_in, hgrn_lb, hgrn_norm_g, hgrn_w_out, router_w, router_b, exp_w_gate, exp_w_up, exp_w_down, ple_w_proj, ple_w_gate):
    bsz, seq, d = x.shape
    assert bsz == 1 and d == D_MODEL and seq % 1024 == 0
    xt = x.reshape(seq, d).astype(F32)
    n_tiles = seq // MOE_TILE + N_BUCKETS
    xs = jnp.zeros((n_tiles * MOE_TILE, d + LANES), F32)
    for i in range(DEPTH):
        kind, slot = i % N_MIXERS, i // N_MIXERS
        if kind == 0:
            w_in = mlstm_w_in[slot]
            w = _pad_cols(w_in.astype(BF16), A_PROJ_PAD)
            wgt = jnp.zeros((16, d), BF16).at[0:2 * A_HEADS].set(w_in[:, A_MAIN:].T.astype(BF16))
            z, gt = _inproj(xt, w, wgt)
            y = _mlstm_mix(z, gt, mlstm_conv[slot], mlstm_gate_b[slot], mlstm_norm_g[slot])
            w_out = mlstm_w_out[slot]
        elif kind == 1:
            z = _inproj(xt, _pad_cols(gla_w_in[slot].astype(BF16), B_PROJ_PAD))
            y = _gla_mix(z, gla_w_a2[slot], gla_b_a[slot], gla_norm_g[slot])
            w_out = gla_w_out[slot]
        else:
            z = _inproj(xt, hgrn_w_in[slot].astype(BF16))
            y = _hgrn_mix(z, hgrn_lb, i, hgrn_norm_g[slot])
            w_out = hgrn_w_out[slot]
        x1, gate, meta, cnt = _outproj_ln_route(y, xt, w_out.astype(BF16), ln1_g[i], ln1_b[i],
                                                router_w, router_b)
        dest, tile_lo, tile_hi, n_used = _moe_plan(meta, cnt, MOE_TILE, n_tiles)
        xs = _dispatch(dest, x1, gate, xs)
        ys = _experts(tile_lo, tile_hi, n_used, xs, exp_w_gate[i].astype(BF16), exp_w_up[i].astype(BF16),
                      exp_w_down[i].astype(BF16), MOE_TILE)
        xt = _combine_ln_ple(dest, x1, p[i].reshape(seq, PLE_DIM), ys, ln2_g[i], ln2_b[i],
                             ple_w_gate[i].astype(BF16), ple_w_proj[i].astype(BF16))
    return xt.reshape(bsz, seq, d)
```

```python
import functools

import jax
import jax.numpy as jnp
from jax import lax
from jax.experimental import pallas as pl
from jax.experimental.pallas import tpu as pltpu

F32 = jnp.float32
BF16 = jnp.bfloat16

D_MODEL = 1024
DEPTH = 4
CHUNK = 64
PLE_DIM = 256
N_MIXERS = 3
LN_EPS = 1e-5
DEEPNORM_ALPHA = (2 * DEPTH) ** 0.25
A_HEADS, A_DK, A_DV = 4, 128, 256
B_HEADS, B_DK, B_DV = 4, 128, 256
B_RANK = 16
B_TAU = 16.0
C_HEADS, C_DK, C_DV = 8, 128, 128
N_EXPERTS = 16
N_GROUPS = 4
EXPERTS_PER_GROUP = 4
D_EXPERT = 512
PAIRS_PER_GROUP = EXPERTS_PER_GROUP * (EXPERTS_PER_GROUP - 1) // 2
N_BUCKETS = N_GROUPS * PAIRS_PER_GROUP
BUCKET_ROWS = 32
MOE_TILE = 256

LANES = 128
SLOT = 8
N_SLOTS = CHUNK // SLOT
A_MAIN = 2 * A_HEADS * A_DK + 2 * A_HEADS * A_DV
A_PROJ_PAD = A_MAIN + LANES
B_MAIN = 2 * B_HEADS * B_DK + 2 * B_HEADS * B_DV
B_PROJ_PAD = B_MAIN + LANES
C_PROJ = 2 * C_HEADS * C_DK + 2 * C_HEADS * C_DV
VMEM_LIMIT = 56 * 1024 * 1024

NT_DIMS = (((1,), (1,)), ((), ()))
TN_DIMS = (((0,), (0,)), ((), ()))


def _cparams(*sem):
    return pltpu.CompilerParams(dimension_semantics=sem, vmem_limit_bytes=VMEM_LIMIT)


def _sigmoid(x):
    return 1.0 / (1.0 + jnp.exp(-x))


def _silu(x):
    return x * _sigmoid(x)


def _log_sigmoid(x):
    return jnp.minimum(x, 0.0) - jnp.log(1.0 + jnp.exp(-jnp.abs(x)))


def _layer_norm(u, g, b):
    mu = jnp.mean(u, axis=-1, keepdims=True)
    d = u - mu
    var = jnp.mean(d * d, axis=-1, keepdims=True)
    return d * lax.rsqrt(var + LN_EPS) * g + b


def _tri(n, lower):
    r = lax.broadcasted_iota(jnp.int32, (n, n), 0)
    c = lax.broadcasted_iota(jnp.int32, (n, n), 1)
    return (r >= c) if lower else (r <= c)


def _bdot(a, b):
    return jnp.dot(a.astype(BF16), b.astype(BF16), preferred_element_type=F32)


def _bdot_nt(a, b):
    return lax.dot_general(a.astype(BF16), b.astype(BF16), NT_DIMS, preferred_element_type=F32)


def _bdot_tn(a, b):
    return lax.dot_general(a.astype(BF16), b.astype(BF16), TN_DIMS, preferred_element_type=F32)


def _const_spec(shape):
    return pl.BlockSpec(shape, lambda i: (0,) * len(shape))


def _inproj_kernel(x_ref, w_ref, z_ref):
    z_ref[...] = jnp.dot(x_ref[...].astype(BF16), w_ref[...], preferred_element_type=F32)


def _inproj_gates_kernel(x_ref, w_ref, wgt_ref, z_ref, gt_ref):
    xb = x_ref[...].astype(BF16)
    z_ref[...] = jnp.dot(xb, w_ref[...], preferred_element_type=F32)
    for c in range(gt_ref.shape[0]):
        gt_ref[c] = lax.dot_general(wgt_ref[...], xb[c * CHUNK:(c + 1) * CHUNK], NT_DIMS,
                                    preferred_element_type=F32)


def _inproj(x, w, wgt=None, tm=512):
    t, d = x.shape
    n = w.shape[1]
    grid = (t // tm,)
    x_spec = pl.BlockSpec((tm, d), lambda i: (i, 0))
    z_spec = pl.BlockSpec((tm, n), lambda i: (i, 0))
    if wgt is None:
        return pl.pallas_call(
            _inproj_kernel, grid=grid, in_specs=[x_spec, _const_spec((d, n))], out_specs=z_spec,
            out_shape=jax.ShapeDtypeStruct((t, n), F32), compiler_params=_cparams("parallel"),
            name="inproj")(x, w)
    return pl.pallas_call(
        _inproj_gates_kernel, grid=grid,
        in_specs=[x_spec, _const_spec((d, n)), _const_spec(wgt.shape)],
        out_specs=[z_spec, pl.BlockSpec((tm // CHUNK, 16, CHUNK), lambda i: (i, 0, 0))],
        out_shape=[jax.ShapeDtypeStruct((t, n), F32),
                   jax.ShapeDtypeStruct((t // CHUNK, 16, CHUNK), F32)],
        compiler_params=_cparams("parallel"), name="inproj_gates")(x, w, wgt)


def _mlstm_kernel(z_ref, gt_ref, conv_ref, gbr_ref, gbc_ref, ng_ref, y_ref,
                  ext_ref, ct_ref, n_ref, m_ref):
    tt = z_ref.shape[0]
    nqk = A_HEADS * A_DK

    @pl.when(pl.program_id(0) == 0)
    def _():
        ext_ref[0:8, :] = jnp.zeros((8, 2 * nqk), F32)
        ct_ref[...] = jnp.zeros(ct_ref.shape, F32)
        n_ref[...] = jnp.zeros(n_ref.shape, F32)
        m_ref[...] = jnp.zeros(m_ref.shape, F32)

    qk_raw = z_ref[:, 0:2 * nqk]
    ext_ref[8:8 + tt, :] = qk_raw
    acc = conv_ref[3:4, :] * qk_raw
    for s in range(1, 4):
        acc = acc + conv_ref[3 - s:4 - s, :] * ext_ref[pl.ds(8 - s, tt), :]
    ext_ref[0:8, :] = qk_raw[tt - 8:tt, :]
    qkc = _silu(acc)

    tri_lo = _tri(CHUNK, True)
    tri_lo_f = tri_lo.astype(F32)
    tri_up_f = _tri(CHUNK, False).astype(F32)

    for c in range(tt // CHUNK):
        r0 = c * CHUNK
        gc = z_ref[r0:r0 + CHUNK, A_MAIN:A_MAIN + LANES] + gbr_ref[...]
        a_col = jnp.dot(tri_lo_f, _log_sigmoid(gc), preferred_element_type=F32,
                        precision=lax.Precision.HIGHEST)
        gr = gt_ref[c] + gbc_ref[:, 0:1]
        a_row = jnp.dot(_log_sigmoid(gr), tri_up_f, preferred_element_type=F32,
                        precision=lax.Precision.HIGHEST)
        for h in range(A_HEADS):
            ac = a_col[:, A_HEADS + h:A_HEADS + h + 1]
            lic = gc[:, h:h + 1]
            ar = a_row[A_HEADS + h:A_HEADS + h + 1, :]
            lir = gr[h:h + 1, :]
            m_st = m_ref[h:h + 1, 0:1]
            qh = qkc[r0:r0 + CHUNK, h * A_DK:(h + 1) * A_DK]
            kh = qkc[r0:r0 + CHUNK, nqk + h * A_DK:nqk + (h + 1) * A_DK] * (A_DK ** -0.5)
            vh = z_ref[r0:r0 + CHUNK, 2 * nqk + h * A_DV:2 * nqk + (h + 1) * A_DV]
            oh = z_ref[r0:r0 + CHUNK, 2 * nqk + A_HEADS * A_DV + h * A_DV:
                       2 * nqk + A_HEADS * A_DV + (h + 1) * A_DV]

            d_log = jnp.where(tri_lo, ac - ar + lir, -jnp.inf)
            d_max = jnp.max(d_log, axis=-1, keepdims=True)
            s = _bdot_nt(qh, kh) * jnp.exp(d_log - d_max)
            sv = _bdot(s, vh)
            s_sum = jnp.sum(s, axis=-1, keepdims=True)
            a_tot = ar[:, CHUNK - 1:CHUNK]
            g_max = jnp.max(a_tot - ar + lir, axis=-1, keepdims=True)
            wk = jnp.exp(a_tot - ac + lic - g_max) * kh
            c_inc = _bdot_tn(vh, wk)
            n_inc = jnp.sum(wk, axis=0, keepdims=True)

            e_log = ac + m_st
            m_row = jnp.maximum(e_log, d_max)
            w_inter = jnp.exp(e_log - m_row)
            w_intra = jnp.exp(d_max - m_row)
            num = w_inter * _bdot_nt(qh, ct_ref[h]) + w_intra * sv
            den = (w_inter * jnp.sum(qh * n_ref[h:h + 1, :], axis=-1, keepdims=True) + w_intra * s_sum)
            hout = num / jnp.maximum(jnp.abs(den), jnp.exp(-m_row))
            m_new = jnp.maximum(a_tot + m_st, g_max)
            decay = jnp.exp(a_tot + m_st - m_new)
            inc_scale = jnp.exp(g_max - m_new)
            ct_ref[h] = decay * ct_ref[h] + inc_scale * c_inc
            n_ref[h:h + 1, :] = decay * n_ref[h:h + 1, :] + inc_scale * n_inc
            m_ref[h:h + 1, :] = jnp.broadcast_to(m_new, (1, LANES))

            hn = hout * lax.rsqrt(jnp.mean(hout * hout, axis=-1, keepdims=True) + LN_EPS)
            y_ref[r0:r0 + CHUNK, h * A_DV:(h + 1) * A_DV] = (
                hn * ng_ref[:, h * A_DV:(h + 1) * A_DV] * _sigmoid(oh))


def _mlstm_mix(z, gt, conv_w, gate_b, norm_g, tt=256):
    t = z.shape[0]
    nqk2 = 2 * A_HEADS * A_DK
    conv_p = jnp.zeros((8, nqk2), F32).at[0:4].set(conv_w.astype(F32))
    gb = gate_b.astype(F32)
    gb_row = jnp.zeros((1, LANES), F32).at[0, 0:2 * A_HEADS].set(gb)
    gb_col = jnp.zeros((16, LANES), F32).at[0:2 * A_HEADS, :].set(jnp.broadcast_to(gb[:, None], (2 * A_HEADS, LANES)))
    ng = norm_g.astype(F32).reshape(1, A_HEADS * A_DV)
    return pl.pallas_call(
        _mlstm_kernel, grid=(t // tt,),
        in_specs=[pl.BlockSpec((tt, A_PROJ_PAD), lambda i: (i, 0)),
                  pl.BlockSpec((tt // CHUNK, 16, CHUNK), lambda i: (i, 0, 0)),
                  _const_spec((8, nqk2)), _const_spec((1, LANES)), _const_spec((16, LANES)),
                  _const_spec((1, A_HEADS * A_DV))],
        out_specs=pl.BlockSpec((tt, A_HEADS * A_DV), lambda i: (i, 0)),
        out_shape=jax.ShapeDtypeStruct((t, A_HEADS * A_DV), F32),
        scratch_shapes=[pltpu.VMEM((tt + 8, nqk2), F32),
                        pltpu.VMEM((A_HEADS, A_DV, A_DK), F32),
                        pltpu.VMEM((8, A_DK), F32),
                        pltpu.VMEM((8, LANES), F32)],
        compiler_params=_cparams("arbitrary"), name="mlstm_mix")(z, gt, conv_p, gb_row, gb_col, ng)


def _store_lane_tiles(ref, x):
    for t in range(ref.shape[0]):
        ref[t] = x[:, t * LANES:(t + 1) * LANES]


def _interleaved(ref):
    return jnp.concatenate(
        [jnp.concatenate([ref[t, pl.ds(c, N_SLOTS, stride=SLOT), :] for t in range(ref.shape[0])], axis=1)
         for c in range(SLOT)], axis=0)


def _pair_factors(q, k, b, ref_of_slot):
    zero = jnp.zeros((SLOT, q.shape[1]), F32)
    factors = []
    for half in (1, 2, 4):
        qs, ks = [], []
        for u in range(N_SLOTS):
            r = ref_of_slot((u // (2 * half)) * 2 * half + half - 1)
            rows = slice(u * SLOT, (u + 1) * SLOT)
            if u % (2 * half) >= half:
                qs.append(q[rows] * jnp.exp(b[rows] - r))
                ks.append(zero)
            else:
                qs.append(zero)
                ks.append(k[rows] * jnp.exp(r - b[rows]))
        factors.append((jnp.concatenate(qs, axis=0), jnp.concatenate(ks, axis=0)))
    return factors


def _pair_masks():
    r = lax.broadcasted_iota(jnp.int32, (CHUNK, CHUNK), 0)
    c = lax.broadcasted_iota(jnp.int32, (CHUNK, CHUNK), 1)
    sr, sc = r >> 3, c >> 3
    same_sublane = (r & (SLOT - 1)) == (c & (SLOT - 1))
    natural, interleaved = [], []
    for shift, half in ((1, 1), (2, 2), (3, 4)):
        m = jnp.logical_and((sr >> shift) == (sc >> shift),
                            jnp.logical_and((sr & half) != 0, (sc & half) == 0))
        natural.append(m)
        interleaved.append(jnp.logical_and(m, same_sublane))
    return r == c, natural, interleaved


def _decay_attention_chunk(q, k, v, la, st_ref, qkb_ref, vo_ref, heads, dv):
    dk = LANES
    b = jnp.dot(_tri(CHUNK, True).astype(F32), la, preferred_element_type=F32,
                precision=lax.Precision.HIGHEST)
    b_last = b[CHUNK - 1:CHUNK, :]
    qe = q * jnp.exp(b)
    kd = k * jnp.exp(b_last - b)

    _store_lane_tiles(qkb_ref.at[0], q)
    _store_lane_tiles(qkb_ref.at[1], k)
    _store_lane_tiles(qkb_ref.at[2], b)
    _store_lane_tiles(vo_ref.at[0], v)
    q_i, k_i, b_i = _interleaved(qkb_ref.at[0]), _interleaved(qkb_ref.at[1]), _interleaved(qkb_ref.at[2])
    v_i = _interleaved(vo_ref.at[0])

    eye, masks_n, masks_i = _pair_masks()
    fact_n = _pair_factors(q, k, b, lambda s: b[s * SLOT + SLOT - 1:(s + 1) * SLOT, :])
    fact_i = _pair_factors(q_i, k_i, b_i, lambda s: b_i[s * SLOT:(s + 1) * SLOT, :])

    outs = []
    for h in range(heads):
        kc = slice(h * dk, (h + 1) * dk)
        vc = slice(h * dv, (h + 1) * dv)
        att = jnp.where(eye, _bdot_nt(q[:, kc], k[:, kc]), 0.0)
        for (qf, kf), m in zip(fact_n, masks_n):
            att = att + jnp.where(m, _bdot_nt(qf[:, kc], kf[:, kc]), 0.0)
        att_i = jnp.zeros((CHUNK, CHUNK), F32)
        for (qf, kf), m in zip(fact_i, masks_i):
            att_i = att_i + jnp.where(m, _bdot_nt(qf[:, kc], kf[:, kc]), 0.0)
        outs.append(_bdot_nt(qe[:, kc], st_ref[h]) + _bdot(att, v[:, vc]))
        _store_lane_tiles(vo_ref.at[1, h * dv // LANES:(h + 1) * dv // LANES], _bdot(att_i, v_i[:, vc]))
        st_ref[h] = st_ref[h] * jnp.exp(b_last[:, kc]) + _bdot_tn(v[:, vc], kd[:, kc])
    o_i = _interleaved(vo_ref.at[1])
    return [o + o_i[:, h * dv:(h + 1) * dv] for h, o in enumerate(outs)]


def _head_norm_store(outs, y_ref, r0, ng_ref, gate_act, dv):
    for h, o in enumerate(outs):
        hn = o * lax.rsqrt(jnp.mean(o * o, axis=-1, keepdims=True) + LN_EPS)
        y_ref[r0:r0 + CHUNK, h * dv:(h + 1) * dv] = (
            hn * ng_ref[:, h * dv:(h + 1) * dv] * gate_act[:, h * dv:(h + 1) * dv])


def _gla_kernel(z_ref, wa2_ref, ba_ref, ng_ref, y_ref, st_ref, qkb_ref, vo_ref):
    @pl.when(pl.program_id(0) == 0)
    def _():
        st_ref[...] = jnp.zeros(st_ref.shape, F32)

    nk, nv = B_HEADS * B_DK, B_HEADS * B_DV
    for c in range(z_ref.shape[0] // CHUNK):
        r0 = c * CHUNK
        q = z_ref[r0:r0 + CHUNK, 0:nk] * (B_DK ** -0.5)
        k = z_ref[r0:r0 + CHUNK, nk:2 * nk]
        v = z_ref[r0:r0 + CHUNK, 2 * nk:2 * nk + nv]
        g = z_ref[r0:r0 + CHUNK, 2 * nk + nv:2 * nk + 2 * nv]
        a_lr = z_ref[r0:r0 + CHUNK, B_MAIN:B_MAIN + LANES]
        la = _log_sigmoid(jnp.dot(a_lr.astype(BF16), wa2_ref[...], preferred_element_type=F32)
                          + ba_ref[...]) / B_TAU
        outs = _decay_attention_chunk(q, k, v, la, st_ref, qkb_ref.at[c], vo_ref.at[c], B_HEADS, B_DV)
        _head_norm_store(outs, y_ref, r0, ng_ref, _silu(g), B_DV)


def _gla_mix(z, w_a2, b_a, norm_g, tt=128):
    t = z.shape[0]
    nk, nv = B_HEADS * B_DK, B_HEADS * B_DV
    wa2 = jnp.zeros((LANES, nk), BF16).at[0:B_RANK].set(w_a2.astype(BF16))
    return pl.pallas_call(
        _gla_kernel, grid=(t // tt,),
        in_specs=[pl.BlockSpec((tt, B_PROJ_PAD), lambda i: (i, 0)),
                  _const_spec((LANES, nk)), _const_spec((1, nk)), _const_spec((1, nv))],
        out_specs=pl.BlockSpec((tt, nv), lambda i: (i, 0)),
        out_shape=jax.ShapeDtypeStruct((t, nv), F32),
        scratch_shapes=[pltpu.VMEM((B_HEADS, B_DV, B_DK), F32),
                        pltpu.VMEM((tt // CHUNK, 3, nk // LANES, CHUNK, LANES), F32),
                        pltpu.VMEM((tt // CHUNK, 2, nv // LANES, CHUNK, LANES), F32)],
        compiler_params=_cparams("arbitrary"), name="gla_mix")(
            z, wa2, b_a.astype(F32).reshape(1, nk), norm_g.astype(F32).reshape(1, nv))


def _hgrn_kernel(layer, z_ref, lbp_ref, ng_ref, y_ref, st_ref, qkb_ref, vo_ref):
    @pl.when(pl.program_id(0) == 0)
    def _():
        st_ref[...] = jnp.zeros(st_ref.shape, F32)

    lbp = lbp_ref[0:DEPTH, :]
    e = jnp.exp(lbp - jnp.max(lbp, axis=0, keepdims=True))
    sm = e / jnp.sum(e, axis=0, keepdims=True)
    lb = jnp.zeros((1, sm.shape[1]), F32)
    for r in range(layer + 1):
        lb = lb + sm[r:r + 1, :]
    lb = lb - sm[0:1, :]

    nk, nv = C_HEADS * C_DK, C_HEADS * C_DV
    for c in range(z_ref.shape[0] // CHUNK):
        r0 = c * CHUNK
        q = _silu(z_ref[r0:r0 + CHUNK, 0:nk])
        fg = lb + (1.0 - lb) * _sigmoid(z_ref[r0:r0 + CHUNK, nk:2 * nk])
        v = z_ref[r0:r0 + CHUNK, 2 * nk:2 * nk + nv]
        g = z_ref[r0:r0 + CHUNK, 2 * nk + nv:2 * nk + 2 * nv]
        outs = _decay_attention_chunk(q, 1.0 - fg, v, jnp.log(fg), st_ref, qkb_ref.at[c], vo_ref.at[c],
                                      C_HEADS, C_DV)
        _head_norm_store(outs, y_ref, r0, ng_ref, _sigmoid(g), C_DV)


def _hgrn_mix(z, hgrn_lb, layer, norm_g, tt=128):
    t = z.shape[0]
    nk, nv = C_HEADS * C_DK, C_HEADS * C_DV
    lbp = jnp.zeros((8, nk), F32).at[0:DEPTH].set(hgrn_lb.astype(F32))
    return pl.pallas_call(
        functools.partial(_hgrn_kernel, layer), grid=(t // tt,),
        in_specs=[pl.BlockSpec((tt, C_PROJ), lambda i: (i, 0)),
                  _const_spec((8, nk)), _const_spec((1, nv))],
        out_specs=pl.BlockSpec((tt, nv), lambda i: (i, 0)),
        out_shape=jax.ShapeDtypeStruct((t, nv), F32),
        scratch_shapes=[pltpu.VMEM((C_HEADS, C_DV, C_DK), F32),
                        pltpu.VMEM((tt // CHUNK, 3, nk // LANES, CHUNK, LANES), F32),
                        pltpu.VMEM((tt // CHUNK, 2, nv // LANES, CHUNK, LANES), F32)],
        compiler_params=_cparams("arbitrary"), name="hgrn_mix")(
            z, lbp, norm_g.astype(F32).reshape(1, nv))


def _route(lt, rb_ref):
    rows = [lt[e:e + 1, :] for e in range(N_EXPERTS)]
    mx = rows[0]
    for r in rows[1:]:
        mx = jnp.maximum(mx, r)
    ex = [jnp.exp(r - mx) for r in rows]
    tot = ex[0]
    for r in ex[1:]:
        tot = tot + r
    scores = [r / tot for r in ex]
    sel = [scores[e] + rb_ref[e:e + 1, 0:1] for e in range(N_EXPERTS)]

    gscore = []
    for g in range(N_GROUPS):
        m = sel[g * EXPERTS_PER_GROUP:(g + 1) * EXPERTS_PER_GROUP]
        best = None
        for a in range(EXPERTS_PER_GROUP):
            for b2 in range(a + 1, EXPERTS_PER_GROUP):
                pair = m[a] + m[b2]
                best = pair if best is None else jnp.maximum(best, pair)
        gscore.append(best)
    g_best = gscore[0]
    g_idx = jnp.zeros_like(g_best, dtype=jnp.int32)
    for g in range(1, N_GROUPS):
        better = gscore[g] > g_best
        g_best = jnp.where(better, gscore[g], g_best)
        g_idx = jnp.where(better, g, g_idx)

    masked = [jnp.where(g_idx == (e // EXPERTS_PER_GROUP), sel[e], -jnp.inf) for e in range(N_EXPERTS)]

    def first_argmax(vals):
        best_v, best_i = vals[0], jnp.zeros_like(g_idx)
        for e in range(1, N_EXPERTS):
            better = vals[e] > best_v
            best_v = jnp.where(better, vals[e], best_v)
            best_i = jnp.where(better, e, best_i)
        return best_i

    i1 = first_argmax(masked)
    i2 = first_argmax([jnp.where(i1 == e, -jnp.inf, masked[e]) for e in range(N_EXPERTS)])
    w1 = jnp.zeros_like(mx)
    w2 = jnp.zeros_like(mx)
    for e in range(N_EXPERTS):
        w1 = jnp.where(i1 == e, scores[e], w1)
        w2 = jnp.where(i2 == e, scores[e], w2)
    wsum = w1 + w2
    return i1, i2, w1 / wsum, w2 / wsum


def _outproj_kernel(y_ref, x_ref, w_ref, g_ref, b_ref, rwh_ref, rwl_ref, rb_ref,
                    x1g_ref, meta_ref, cnt_ref, gt_scr, carry_ref):
    tm = x_ref.shape[0]

    @pl.when(pl.program_id(0) == 0)
    def _():
        carry_ref[...] = jnp.zeros(carry_ref.shape, F32)

    mix = jnp.dot(y_ref[...].astype(BF16), w_ref[...], preferred_element_type=F32)
    x1 = _layer_norm(DEEPNORM_ALPHA * x_ref[...] + mix, g_ref[...], b_ref[...])
    x1g_ref[:, 0:D_MODEL] = x1
    x_hi = x1.astype(BF16)
    x_lo = (x1 - x_hi.astype(F32)).astype(BF16)
    logits = (jnp.dot(x_hi, rwh_ref[...], preferred_element_type=F32)
              + (jnp.dot(x_lo, rwh_ref[...], preferred_element_type=F32)
                 + jnp.dot(x_hi, rwl_ref[...], preferred_element_type=F32)))
    i1, i2, w1, w2 = _route(logits.T, rb_ref)

    lo = jnp.minimum(i1, i2)
    hi = jnp.maximum(i1, i2)
    a = lo & (EXPERTS_PER_GROUP - 1)
    b2 = hi & (EXPERTS_PER_GROUP - 1)
    bucket = (lo >> 2) * PAIRS_PER_GROUP + ((a * (7 - a)) >> 1) + (b2 - a - 1)

    onehot = lax.broadcasted_iota(jnp.int32, (BUCKET_ROWS, tm), 0) == bucket
    r = lax.broadcasted_iota(jnp.int32, (tm, tm), 0)
    c = lax.broadcasted_iota(jnp.int32, (tm, tm), 1)
    before = jnp.dot(onehot.astype(BF16), (r < c).astype(BF16), preferred_element_type=F32)
    rank = jnp.sum(jnp.where(onehot, before + carry_ref[:, 0:1], 0.0), axis=0, keepdims=True)
    carry_ref[...] = carry_ref[...] + jnp.sum(onehot.astype(F32), axis=1, keepdims=True)
    cnt_ref[...] = carry_ref[...].astype(jnp.int32)

    meta_ref[...] = jnp.zeros(meta_ref.shape, jnp.int32)
    meta_ref[0:1, :] = bucket
    meta_ref[1:2, :] = rank.astype(jnp.int32)

    gt_scr[...] = jnp.zeros(gt_scr.shape, F32)
    gt_scr[0:1, :] = jnp.where(i1 == lo, w1, w2)
    gt_scr[1:2, :] = jnp.where(i1 == lo, w2, w1)
    x1g_ref[:, D_MODEL:] = gt_scr[...].T


def _outproj_ln_route(y, x, w_out, ln_g, ln_b, router_w, router_b, tm=512):
    t, d = x.shape
    rw = jnp.zeros((d, LANES), F32).at[:, 0:N_EXPERTS].set(router_w.astype(F32))
    rw_hi = rw.astype(BF16)
    rw_lo = (rw - rw_hi.astype(F32)).astype(BF16)
    rb = jnp.broadcast_to(router_b.astype(F32)[:, None], (N_EXPERTS, LANES))
    return pl.pallas_call(
        _outproj_kernel, grid=(t // tm,),
        in_specs=[pl.BlockSpec((tm, y.shape[1]), lambda i: (i, 0)), pl.BlockSpec((tm, d), lambda i: (i, 0)),
                  _const_spec(w_out.shape), _const_spec((1, d)), _const_spec((1, d)),
                  _const_spec((d, LANES)), _const_spec((d, LANES)), _const_spec((N_EXPERTS, LANES))],
        out_specs=[pl.BlockSpec((tm, d + LANES), lambda i: (i, 0)),
                   pl.BlockSpec((8, tm), lambda i: (0, i)), _const_spec((BUCKET_ROWS, LANES))],
        out_shape=[jax.ShapeDtypeStruct((t, d + LANES), F32),
                   jax.ShapeDtypeStruct((8, t), jnp.int32),
                   jax.ShapeDtypeStruct((BUCKET_ROWS, LANES), jnp.int32)],
        scratch_shapes=[pltpu.VMEM((LANES, tm), F32), pltpu.VMEM((BUCKET_ROWS, LANES), F32)],
        compiler_params=_cparams("arbitrary"), name="outproj_ln_route")(
            y, x, w_out, ln_g.astype(F32).reshape(1, d), ln_b.astype(F32).reshape(1, d), rw_hi, rw_lo, rb)


def _moe_plan(meta, cnt, tile_rows, n_tiles):
    counts = cnt[0:N_BUCKETS, 0]
    tiles = (counts + tile_rows - 1) // tile_rows
    tile_end = jnp.cumsum(tiles)
    offs = (tile_end - tiles) * tile_rows
    dest = offs[meta[0]] + meta[1]
    tok = jnp.full((n_tiles * tile_rows,), -1, jnp.int32).at[dest].set(
        jnp.arange(dest.shape[0], dtype=jnp.int32))
    n_used = tile_end[-1]
    j = jnp.minimum(jnp.arange(n_tiles, dtype=jnp.int32), n_used - 1)
    bucket_of_tile = jnp.minimum(jnp.sum(tile_end[None, :] <= j[:, None], axis=1), N_BUCKETS - 1)
    pairs = [(a, b) for a in range(EXPERTS_PER_GROUP) for b in range(a + 1, EXPERTS_PER_GROUP)]
    lo_tab = jnp.array([g * EXPERTS_PER_GROUP + a for g in range(N_GROUPS) for a, _ in pairs], jnp.int32)
    hi_tab = jnp.array([g * EXPERTS_PER_GROUP + b for g in range(N_GROUPS) for _, b in pairs], jnp.int32)
    return tok, lo_tab[bucket_of_tile], hi_tab[bucket_of_tile], n_used.astype(jnp.int32).reshape(1)


def _expert(x, wg_ref, wu_ref, wd_ref):
    hg = jnp.dot(x, wg_ref[0, 0], preferred_element_type=F32)
    hu = jnp.dot(x, wu_ref[0, 0], preferred_element_type=F32)
    return jnp.dot((_silu(hg) * hu).astype(BF16), wd_ref[0, 0], preferred_element_type=F32)


def _experts_kernel(lo_ref, hi_ref, nused_ref, tok_ref, x1g_ref, y_in_ref, wg0, wu0, wd0, wg1, wu1, wd1,
                    y_ref, xbuf0, xbuf1, ybuf0, ybuf1, gsem, ssem):
    del lo_ref, hi_ref, y_in_ref
    tm = xbuf0.shape[0]
    n_tok = x1g_ref.shape[0]
    j = pl.program_id(0)
    n_used = nused_ref[0]
    nxt = jnp.minimum(j + 1, n_used - 1)

    def gather(tile, xbuf, s, r):
        src = jnp.maximum(tok_ref[tile * tm + r], 0)
        return pltpu.make_async_copy(x1g_ref.at[pl.ds(src, 1)], xbuf.at[pl.ds(r, 1)], gsem.at[s])

    def scatter(tile, ybuf, s, r, live=True):
        tk = jnp.where(live, tok_ref[tile * tm + r], -1)
        dst = jnp.where(tk >= 0, tk, n_tok + s * tm + r)
        return pltpu.make_async_copy(ybuf.at[pl.ds(r, 1)], y_ref.at[pl.ds(dst, 1)], ssem.at[s])

    def step(s, x_cur, x_nxt, y_cur, y_prv):
        o = 1 - s
        if s == 0:
            @pl.when(j == 0)
            def _():
                y_prv[...] = jnp.zeros(y_prv.shape, F32)
                for r in range(tm):
                    gather(0, x_cur, s, r).start()

        @pl.when(j >= 1)
        def _():
            for r in range(tm):
                scatter(0, y_cur, s, r).wait()

        for r in range(tm):
            gather(j, x_cur, s, r).wait()
        for r in range(tm):
            gather(nxt, x_nxt, o, r).start()
        for r in range(tm):
            scatter(jnp.maximum(j - 1, 0), y_prv, o, r, live=j >= 1).start()
        x = x_cur[:, 0:D_MODEL].astype(BF16)
        g = x_cur[:, D_MODEL:]
        y_cur[...] = g[:, 0:1] * _expert(x, wg0, wu0, wd0) + g[:, 1:2] * _expert(x, wg1, wu1, wd1)

        @pl.when(j == n_used - 1)
        def _():
            for r in range(tm):
                scatter(j, y_cur, s, r).start()
            for r in range(tm):
                gather(nxt, x_nxt, o, r).wait()
                scatter(0, y_prv, o, r).wait()
                scatter(j, y_cur, s, r).wait()

    @pl.when(jnp.logical_and(j < n_used, j % 2 == 0))
    def _():
        step(0, xbuf0, xbuf1, ybuf0, ybuf1)

    @pl.when(jnp.logical_and(j < n_used, j % 2 == 1))
    def _():
        step(1, xbuf1, xbuf0, ybuf1, ybuf0)


def _experts(layer, tok, tile_lo, tile_hi, n_used, x1g, y_buf, wg, wu, wd, tm):
    d = D_MODEL
    n_tiles = tok.shape[0] // tm
    used = lambda j, n: jnp.maximum(jnp.minimum(j, n[0] - 1), 0)
    w_in = lambda which: pl.BlockSpec(
        (1, 1, d, D_EXPERT), lambda j, lo, hi, n, tok: (layer, (lo, hi)[which][used(j, n)], 0, 0))
    w_out = lambda which: pl.BlockSpec(
        (1, 1, D_EXPERT, d), lambda j, lo, hi, n, tok: (layer, (lo, hi)[which][used(j, n)], 0, 0))
    hbm = pl.BlockSpec(memory_space=pl.ANY)
    return pl.pallas_call(
        _experts_kernel,
        grid_spec=pltpu.PrefetchScalarGridSpec(
            num_scalar_prefetch=4, grid=(n_tiles,),
            in_specs=[hbm, hbm, w_in(0), w_in(0), w_out(0), w_in(1), w_in(1), w_out(1)],
            out_specs=hbm,
            scratch_shapes=[pltpu.VMEM((tm, x1g.shape[1]), F32), pltpu.VMEM((tm, x1g.shape[1]), F32),
                            pltpu.VMEM((tm, d), F32), pltpu.VMEM((tm, d), F32),
                            pltpu.SemaphoreType.DMA((2,)), pltpu.SemaphoreType.DMA((2,))]),
        out_shape=jax.ShapeDtypeStruct(y_buf.shape, F32), input_output_aliases={5: 0},
        compiler_params=_cparams("arbitrary"), name="moe_experts")(
            tile_lo, tile_hi, n_used, tok, x1g, y_buf, wg, wu, wd, wg, wu, wd)


def _ln_ple_kernel(x1g_ref, y_ref, p_ref, g_ref, b_ref, wpg_ref, wpp_ref, o_ref):
    x2 = _layer_norm(DEEPNORM_ALPHA * x1g_ref[:, 0:D_MODEL] + y_ref[...], g_ref[...], b_ref[...])
    gate = _sigmoid(jnp.dot(x2.astype(BF16), wpg_ref[...], preferred_element_type=F32))
    o_ref[...] = x2 + gate * jnp.dot(p_ref[...].astype(BF16), wpp_ref[...], preferred_element_type=F32)


def _ln_ple(x1g, y_buf, p, ln_g, ln_b, w_gate, w_proj, tm=512):
    t = x1g.shape[0]
    d = D_MODEL
    tile = pl.BlockSpec((tm, d), lambda i: (i, 0))
    return pl.pallas_call(
        _ln_ple_kernel, grid=(t // tm,),
        in_specs=[pl.BlockSpec((tm, x1g.shape[1]), lambda i: (i, 0)), tile,
                  pl.BlockSpec((tm, PLE_DIM), lambda i: (i, 0)),
                  _const_spec((1, d)), _const_spec((1, d)), _const_spec((d, d)), _const_spec((PLE_DIM, d))],
        out_specs=tile, out_shape=jax.ShapeDtypeStruct((t, d), F32),
        compiler_params=_cparams("parallel"), name="ln_ple")(
            x1g, y_buf, p, ln_g.astype(F32).reshape(1, d), ln_b.astype(F32).reshape(1, d), w_gate, w_proj)


def _pad_cols(w, n):
    return jnp.zeros((w.shape[0], n), w.dtype).at[:, 0:w.shape[1]].set(w)


def kernel(x, p, ln1_g, ln1_b, ln2_g, ln2_b, mlstm_w_in, mlstm_conv, mlstm_gate_b, mlstm_norm_g, mlstm_w_out, gla_w_in, gla_w_a2, gla_b_a, gla_norm_g, gla_w_out, hgrn_w_in, hgrn_lb, hgrn_norm_g, hgrn_w_out, router_w, router_b, exp_w_gate, exp_w_up, exp_w_down, ple_w_proj, ple_w_gate):
    bsz, seq, d = x.shape
    assert bsz == 1 and d == D_MODEL and seq % 1024 == 0
    xt = x.reshape(seq, d).astype(F32)
    n_tiles = seq // MOE_TILE + N_BUCKETS
    y_buf = jnp.zeros((seq + 2 * MOE_TILE, d), F32)
    wg_all, wu_all, wd_all = exp_w_gate.astype(BF16), exp_w_up.astype(BF16), exp_w_down.astype(BF16)
    for i in range(DEPTH):
        kind, slot = i % N_MIXERS, i // N_MIXERS
        if kind == 0:
            w_in = mlstm_w_in[slot]
            w = _pad_cols(w_in.astype(BF16), A_PROJ_PAD)
            wgt = jnp.zeros((16, d), BF16).at[0:2 * A_HEADS].set(w_in[:, A_MAIN:].T.astype(BF16))
            z, gt = _inproj(xt, w, wgt)
            y = _mlstm_mix(z, gt, mlstm_conv[slot], mlstm_gate_b[slot], mlstm_norm_g[slot])
            w_out = mlstm_w_out[slot]
        elif kind == 1:
            z = _inproj(xt, _pad_cols(gla_w_in[slot].astype(BF16), B_PROJ_PAD))
            y = _gla_mix(z, gla_w_a2[slot], gla_b_a[slot], gla_norm_g[slot])
            w_out = gla_w_out[slot]
        else:
            z = _inproj(xt, hgrn_w_in[slot].astype(BF16))
            y = _hgrn_mix(z, hgrn_lb, i, hgrn_norm_g[slot])
            w_out = hgrn_w_out[slot]
        x1g, meta, cnt = _outproj_ln_route(y, xt, w_out.astype(BF16), ln1_g[i], ln1_b[i], router_w, router_b)
        tok, tile_lo, tile_hi, n_used = _moe_plan(meta, cnt, MOE_TILE, n_tiles)
        y_buf = _experts(i, tok, tile_lo, tile_hi, n_used, x1g, y_buf, wg_all, wu_all, wd_all, MOE_TILE)
        xt = _ln_ple(x1g, y_buf, p[i].reshape(seq, PLE_DIM), ln2_g[i], ln2_b[i],
                     ple_w_gate[i].astype(BF16), ple_w_proj[i].astype(BF16))
    return xt.reshape(bsz, seq, d)
```

```python
import functools

import jax
import jax.numpy as jnp
from jax import lax
from jax.experimental import pallas as pl
from jax.experimental.pallas import tpu as pltpu

F32 = jnp.float32
BF16 = jnp.bfloat16

D_MODEL = 1024
DEPTH = 4
CHUNK = 64
PLE_DIM = 256
N_MIXERS = 3
LN_EPS = 1e-5
DEEPNORM_ALPHA = (2 * DEPTH) ** 0.25
A_HEADS, A_DK, A_DV = 4, 128, 256
B_HEADS, B_DK, B_DV = 4, 128, 256
B_RANK = 16
B_TAU = 16.0
C_HEADS, C_DK, C_DV = 8, 128, 128
N_EXPERTS = 16
N_GROUPS = 4
EXPERTS_PER_GROUP = 4
D_EXPERT = 512
PAIRS_PER_GROUP = EXPERTS_PER_GROUP * (EXPERTS_PER_GROUP - 1) // 2
N_BUCKETS = N_GROUPS * PAIRS_PER_GROUP
BUCKET_ROWS = 32
MOE_TILE = 256

LANES = 128
SLOT = 8
X_ROW_TILES = D_MODEL // LANES
Y_ROW_TILES = 2 * X_ROW_TILES
N_SLOTS = CHUNK // SLOT
A_MAIN = 2 * A_HEADS * A_DK + 2 * A_HEADS * A_DV
A_PROJ_PAD = A_MAIN + LANES
B_MAIN = 2 * B_HEADS * B_DK + 2 * B_HEADS * B_DV
B_PROJ_PAD = B_MAIN + LANES
C_PROJ = 2 * C_HEADS * C_DK + 2 * C_HEADS * C_DV
VMEM_LIMIT = 56 * 1024 * 1024

NT_DIMS = (((1,), (1,)), ((), ()))
TN_DIMS = (((0,), (0,)), ((), ()))


def _cparams(*sem):
    return pltpu.CompilerParams(dimension_semantics=sem, vmem_limit_bytes=VMEM_LIMIT)


def _sigmoid(x):
    return 1.0 / (1.0 + jnp.exp(-x))


def _silu(x):
    return x * _sigmoid(x)


def _log_sigmoid(x):
    return jnp.minimum(x, 0.0) - jnp.log(1.0 + jnp.exp(-jnp.abs(x)))


def _layer_norm(u, g, b):
    mu = jnp.mean(u, axis=-1, keepdims=True)
    d = u - mu
    var = jnp.mean(d * d, axis=-1, keepdims=True)
    return d * lax.rsqrt(var + LN_EPS) * g + b


def _tri(n, lower):
    r = lax.broadcasted_iota(jnp.int32, (n, n), 0)
    c = lax.broadcasted_iota(jnp.int32, (n, n), 1)
    return (r >= c) if lower else (r <= c)


def _bdot(a, b):
    return jnp.dot(a.astype(BF16), b.astype(BF16), preferred_element_type=F32)


def _bdot_nt(a, b):
    return lax.dot_general(a.astype(BF16), b.astype(BF16), NT_DIMS, preferred_element_type=F32)


def _bdot_tn(a, b):
    return lax.dot_general(a.astype(BF16), b.astype(BF16), TN_DIMS, preferred_element_type=F32)


def _const_spec(shape):
    return pl.BlockSpec(shape, lambda i: (0,) * len(shape))


def _inproj_kernel(x_ref, w_ref, z_ref):
    z_ref[...] = jnp.dot(x_ref[...].astype(BF16), w_ref[...], preferred_element_type=F32)


def _inproj_gates_kernel(x_ref, w_ref, wgt_ref, z_ref, gt_ref):
    xb = x_ref[...].astype(BF16)
    z_ref[...] = jnp.dot(xb, w_ref[...], preferred_element_type=F32)
    for c in range(gt_ref.shape[0]):
        gt_ref[c] = lax.dot_general(wgt_ref[...], xb[c * CHUNK:(c + 1) * CHUNK], NT_DIMS,
                                    preferred_element_type=F32)


def _inproj(x, w, wgt=None, tm=512):
    t, d = x.shape
    n = w.shape[1]
    grid = (t // tm,)
    x_spec = pl.BlockSpec((tm, d), lambda i: (i, 0))
    z_spec = pl.BlockSpec((tm, n), lambda i: (i, 0))
    if wgt is None:
        return pl.pallas_call(
            _inproj_kernel, grid=grid, in_specs=[x_spec, _const_spec((d, n))], out_specs=z_spec,
            out_shape=jax.ShapeDtypeStruct((t, n), F32), compiler_params=_cparams("parallel"),
            name="inproj")(x, w)
    return pl.pallas_call(
        _inproj_gates_kernel, grid=grid,
        in_specs=[x_spec, _const_spec((d, n)), _const_spec(wgt.shape)],
        out_specs=[z_spec, pl.BlockSpec((tm // CHUNK, 16, CHUNK), lambda i: (i, 0, 0))],
        out_shape=[jax.ShapeDtypeStruct((t, n), F32),
                   jax.ShapeDtypeStruct((t // CHUNK, 16, CHUNK), F32)],
        compiler_params=_cparams("parallel"), name="inproj_gates")(x, w, wgt)


def _mlstm_kernel(z_ref, gt_ref, conv_ref, gbr_ref, gbc_ref, ng_ref, y_ref,
                  ext_ref, ct_ref, n_ref, m_ref):
    tt = z_ref.shape[0]
    nqk = A_HEADS * A_DK

    @pl.when(pl.program_id(0) == 0)
    def _():
        ext_ref[0:8, :] = jnp.zeros((8, 2 * nqk), F32)
        ct_ref[...] = jnp.zeros(ct_ref.shape, F32)
        n_ref[...] = jnp.zeros(n_ref.shape, F32)
        m_ref[...] = jnp.zeros(m_ref.shape, F32)

    qk_raw = z_ref[:, 0:2 * nqk]
    ext_ref[8:8 + tt, :] = qk_raw
    acc = conv_ref[3:4, :] * qk_raw
    for s in range(1, 4):
        acc = acc + conv_ref[3 - s:4 - s, :] * ext_ref[pl.ds(8 - s, tt), :]
    ext_ref[0:8, :] = qk_raw[tt - 8:tt, :]
    qkc = _silu(acc)

    tri_lo = _tri(CHUNK, True)
    tri_lo_f = tri_lo.astype(F32)
    tri_up_f = _tri(CHUNK, False).astype(F32)

    for c in range(tt // CHUNK):
        r0 = c * CHUNK
        gc = z_ref[r0:r0 + CHUNK, A_MAIN:A_MAIN + LANES] + gbr_ref[...]
        a_col = jnp.dot(tri_lo_f, _log_sigmoid(gc), preferred_element_type=F32,
                        precision=lax.Precision.HIGHEST)
        gr = gt_ref[c] + gbc_ref[:, 0:1]
        a_row = jnp.dot(_log_sigmoid(gr), tri_up_f, preferred_element_type=F32,
                        precision=lax.Precision.HIGHEST)
        for h in range(A_HEADS):
            ac = a_col[:, A_HEADS + h:A_HEADS + h + 1]
            lic = gc[:, h:h + 1]
            ar = a_row[A_HEADS + h:A_HEADS + h + 1, :]
            lir = gr[h:h + 1, :]
            m_st = m_ref[h:h + 1, 0:1]
            qh = qkc[r0:r0 + CHUNK, h * A_DK:(h + 1) * A_DK]
            kh = qkc[r0:r0 + CHUNK, nqk + h * A_DK:nqk + (h + 1) * A_DK] * (A_DK ** -0.5)
            vh = z_ref[r0:r0 + CHUNK, 2 * nqk + h * A_DV:2 * nqk + (h + 1) * A_DV]
            oh = z_ref[r0:r0 + CHUNK, 2 * nqk + A_HEADS * A_DV + h * A_DV:
                       2 * nqk + A_HEADS * A_DV + (h + 1) * A_DV]

            d_log = jnp.where(tri_lo, ac - ar + lir, -jnp.inf)
            d_max = jnp.max(d_log, axis=-1, keepdims=True)
            s = _bdot_nt(qh, kh) * jnp.exp(d_log - d_max)
            sv = _bdot(s, vh)
            s_sum = jnp.sum(s, axis=-1, keepdims=True)
            a_tot = ar[:, CHUNK - 1:CHUNK]
            g_max = jnp.max(a_tot - ar + lir, axis=-1, keepdims=True)
            wk = jnp.exp(a_tot - ac + lic - g_max) * kh
            c_inc = _bdot_tn(vh, wk)
            n_inc = jnp.sum(wk, axis=0, keepdims=True)

            e_log = ac + m_st
            m_row = jnp.maximum(e_log, d_max)
            w_inter = jnp.exp(e_log - m_row)
            w_intra = jnp.exp(d_max - m_row)
            num = w_inter * _bdot_nt(qh, ct_ref[h]) + w_intra * sv
            den = (w_inter * jnp.sum(qh * n_ref[h:h + 1, :], axis=-1, keepdims=True) + w_intra * s_sum)
            hout = num / jnp.maximum(jnp.abs(den), jnp.exp(-m_row))
            m_new = jnp.maximum(a_tot + m_st, g_max)
            decay = jnp.exp(a_tot + m_st - m_new)
            inc_scale = jnp.exp(g_max - m_new)
            ct_ref[h] = decay * ct_ref[h] + inc_scale * c_inc
            n_ref[h:h + 1, :] = decay * n_ref[h:h + 1, :] + inc_scale * n_inc
            m_ref[h:h + 1, :] = jnp.broadcast_to(m_new, (1, LANES))

            hn = hout * lax.rsqrt(jnp.mean(hout * hout, axis=-1, keepdims=True) + LN_EPS)
            y_ref[r0:r0 + CHUNK, h * A_DV:(h + 1) * A_DV] = (
                hn * ng_ref[:, h * A_DV:(h + 1) * A_DV] * _sigmoid(oh))


def _mlstm_mix(z, gt, conv_w, gate_b, norm_g, tt=256):
    t = z.shape[0]
    nqk2 = 2 * A_HEADS * A_DK
    conv_p = jnp.zeros((8, nqk2), F32).at[0:4].set(conv_w.astype(F32))
    gb = gate_b.astype(F32)
    gb_row = jnp.zeros((1, LANES), F32).at[0, 0:2 * A_HEADS].set(gb)
    gb_col = jnp.zeros((16, LANES), F32).at[0:2 * A_HEADS, :].set(jnp.broadcast_to(gb[:, None], (2 * A_HEADS, LANES)))
    ng = norm_g.astype(F32).reshape(1, A_HEADS * A_DV)
    return pl.pallas_call(
        _mlstm_kernel, grid=(t // tt,),
        in_specs=[pl.BlockSpec((tt, A_PROJ_PAD), lambda i: (i, 0)),
                  pl.BlockSpec((tt // CHUNK, 16, CHUNK), lambda i: (i, 0, 0)),
                  _const_spec((8, nqk2)), _const_spec((1, LANES)), _const_spec((16, LANES)),
                  _const_spec((1, A_HEADS * A_DV))],
        out_specs=pl.BlockSpec((tt, A_HEADS * A_DV), lambda i: (i, 0)),
        out_shape=jax.ShapeDtypeStruct((t, A_HEADS * A_DV), F32),
        scratch_shapes=[pltpu.VMEM((tt + 8, nqk2), F32),
                        pltpu.VMEM((A_HEADS, A_DV, A_DK), F32),
                        pltpu.VMEM((8, A_DK), F32),
                        pltpu.VMEM((8, LANES), F32)],
        compiler_params=_cparams("arbitrary"), name="mlstm_mix")(z, gt, conv_p, gb_row, gb_col, ng)


def _store_lane_tiles(ref, x):
    for t in range(ref.shape[0]):
        ref[t] = x[:, t * LANES:(t + 1) * LANES]


def _interleaved(ref):
    return jnp.concatenate(
        [jnp.concatenate([ref[t, pl.ds(c, N_SLOTS, stride=SLOT), :] for t in range(ref.shape[0])], axis=1)
         for c in range(SLOT)], axis=0)


def _pair_factors(q, k, b, ref_of_slot):
    zero = jnp.zeros((SLOT, q.shape[1]), F32)
    factors = []
    for half in (1, 2, 4):
        qs, ks = [], []
        for u in range(N_SLOTS):
            r = ref_of_slot((u // (2 * half)) * 2 * half + half - 1)
            rows = slice(u * SLOT, (u + 1) * SLOT)
            if u % (2 * half) >= half:
                qs.append(q[rows] * jnp.exp(b[rows] - r))
                ks.append(zero)
            else:
                qs.append(zero)
                ks.append(k[rows] * jnp.exp(r - b[rows]))
        factors.append((jnp.concatenate(qs, axis=0), jnp.concatenate(ks, axis=0)))
    return factors


def _pair_masks():
    r = lax.broadcasted_iota(jnp.int32, (CHUNK, CHUNK), 0)
    c = lax.broadcasted_iota(jnp.int32, (CHUNK, CHUNK), 1)
    sr, sc = r >> 3, c >> 3
    same_sublane = (r & (SLOT - 1)) == (c & (SLOT - 1))
    natural, interleaved = [], []
    for shift, half in ((1, 1), (2, 2), (3, 4)):
        m = jnp.logical_and((sr >> shift) == (sc >> shift),
                            jnp.logical_and((sr & half) != 0, (sc & half) == 0))
        natural.append(m)
        interleaved.append(jnp.logical_and(m, same_sublane))
    return r == c, natural, interleaved


def _decay_attention_chunk(q, k, v, la, st_ref, qkb_ref, vo_ref, heads, dv):
    dk = LANES
    b = jnp.dot(_tri(CHUNK, True).astype(F32), la, preferred_element_type=F32,
                precision=lax.Precision.HIGHEST)
    b_last = b[CHUNK - 1:CHUNK, :]
    qe = q * jnp.exp(b)
    kd = k * jnp.exp(b_last - b)

    _store_lane_tiles(qkb_ref.at[0], q)
    _store_lane_tiles(qkb_ref.at[1], k)
    _store_lane_tiles(qkb_ref.at[2], b)
    _store_lane_tiles(vo_ref.at[0], v)
    q_i, k_i, b_i = _interleaved(qkb_ref.at[0]), _interleaved(qkb_ref.at[1]), _interleaved(qkb_ref.at[2])
    v_i = _interleaved(vo_ref.at[0])

    eye, masks_n, masks_i = _pair_masks()
    fact_n = _pair_factors(q, k, b, lambda s: b[s * SLOT + SLOT - 1:(s + 1) * SLOT, :])
    fact_i = _pair_factors(q_i, k_i, b_i, lambda s: b_i[s * SLOT:(s + 1) * SLOT, :])

    outs = []
    for h in range(heads):
        kc = slice(h * dk, (h + 1) * dk)
        vc = slice(h * dv, (h + 1) * dv)
        att = jnp.where(eye, _bdot_nt(q[:, kc], k[:, kc]), 0.0)
        for (qf, kf), m in zip(fact_n, masks_n):
            att = att + jnp.where(m, _bdot_nt(qf[:, kc], kf[:, kc]), 0.0)
        att_i = jnp.zeros((CHUNK, CHUNK), F32)
        for (qf, kf), m in zip(fact_i, masks_i):
            att_i = att_i + jnp.where(m, _bdot_nt(qf[:, kc], kf[:, kc]), 0.0)
        outs.append(_bdot_nt(qe[:, kc], st_ref[h]) + _bdot(att, v[:, vc]))
        _store_lane_tiles(vo_ref.at[1, h * dv // LANES:(h + 1) * dv // LANES], _bdot(att_i, v_i[:, vc]))
        st_ref[h] = st_ref[h] * jnp.exp(b_last[:, kc]) + _bdot_tn(v[:, vc], kd[:, kc])
    o_i = _interleaved(vo_ref.at[1])
    return [o + o_i[:, h * dv:(h + 1) * dv] for h, o in enumerate(outs)]


def _head_norm_store(outs, y_ref, r0, ng_ref, gate_act, dv):
    for h, o in enumerate(outs):
        hn = o * lax.rsqrt(jnp.mean(o * o, axis=-1, keepdims=True) + LN_EPS)
        y_ref[r0:r0 + CHUNK, h * dv:(h + 1) * dv] = (
            hn * ng_ref[:, h * dv:(h + 1) * dv] * gate_act[:, h * dv:(h + 1) * dv])


def _gla_kernel(z_ref, wa2_ref, ba_ref, ng_ref, y_ref, st_ref, qkb_ref, vo_ref):
    @pl.when(pl.program_id(0) == 0)
    def _():
        st_ref[...] = jnp.zeros(st_ref.shape, F32)

    nk, nv = B_HEADS * B_DK, B_HEADS * B_DV
    for c in range(z_ref.shape[0] // CHUNK):
        r0 = c * CHUNK
        q = z_ref[r0:r0 + CHUNK, 0:nk] * (B_DK ** -0.5)
        k = z_ref[r0:r0 + CHUNK, nk:2 * nk]
        v = z_ref[r0:r0 + CHUNK, 2 * nk:2 * nk + nv]
        g = z_ref[r0:r0 + CHUNK, 2 * nk + nv:2 * nk + 2 * nv]
        a_lr = z_ref[r0:r0 + CHUNK, B_MAIN:B_MAIN + LANES]
        la = _log_sigmoid(jnp.dot(a_lr.astype(BF16), wa2_ref[...], preferred_element_type=F32)
                          + ba_ref[...]) / B_TAU
        outs = _decay_attention_chunk(q, k, v, la, st_ref, qkb_ref.at[c], vo_ref.at[c], B_HEADS, B_DV)
        _head_norm_store(outs, y_ref, r0, ng_ref, _silu(g), B_DV)


def _gla_mix(z, w_a2, b_a, norm_g, tt=128):
    t = z.shape[0]
    nk, nv = B_HEADS * B_DK, B_HEADS * B_DV
    wa2 = jnp.zeros((LANES, nk), BF16).at[0:B_RANK].set(w_a2.astype(BF16))
    return pl.pallas_call(
        _gla_kernel, grid=(t // tt,),
        in_specs=[pl.BlockSpec((tt, B_PROJ_PAD), lambda i: (i, 0)),
                  _const_spec((LANES, nk)), _const_spec((1, nk)), _const_spec((1, nv))],
        out_specs=pl.BlockSpec((tt, nv), lambda i: (i, 0)),
        out_shape=jax.ShapeDtypeStruct((t, nv), F32),
        scratch_shapes=[pltpu.VMEM((B_HEADS, B_DV, B_DK), F32),
                        pltpu.VMEM((tt // CHUNK, 3, nk // LANES, CHUNK, LANES), F32),
                        pltpu.VMEM((tt // CHUNK, 2, nv // LANES, CHUNK, LANES), F32)],
        compiler_params=_cparams("arbitrary"), name="gla_mix")(
            z, wa2, b_a.astype(F32).reshape(1, nk), norm_g.astype(F32).reshape(1, nv))


def _hgrn_kernel(layer, z_ref, lbp_ref, ng_ref, y_ref, st_ref, qkb_ref, vo_ref):
    @pl.when(pl.program_id(0) == 0)
    def _():
        st_ref[...] = jnp.zeros(st_ref.shape, F32)

    lbp = lbp_ref[0:DEPTH, :]
    e = jnp.exp(lbp - jnp.max(lbp, axis=0, keepdims=True))
    sm = e / jnp.sum(e, axis=0, keepdims=True)
    lb = jnp.zeros((1, sm.shape[1]), F32)
    for r in range(layer + 1):
        lb = lb + sm[r:r + 1, :]
    lb = lb - sm[0:1, :]

    nk, nv = C_HEADS * C_DK, C_HEADS * C_DV
    for c in range(z_ref.shape[0] // CHUNK):
        r0 = c * CHUNK
        q = _silu(z_ref[r0:r0 + CHUNK, 0:nk])
        fg = lb + (1.0 - lb) * _sigmoid(z_ref[r0:r0 + CHUNK, nk:2 * nk])
        v = z_ref[r0:r0 + CHUNK, 2 * nk:2 * nk + nv]
        g = z_ref[r0:r0 + CHUNK, 2 * nk + nv:2 * nk + 2 * nv]
        outs = _decay_attention_chunk(q, 1.0 - fg, v, jnp.log(fg), st_ref, qkb_ref.at[c], vo_ref.at[c],
                                      C_HEADS, C_DV)
        _head_norm_store(outs, y_ref, r0, ng_ref, _sigmoid(g), C_DV)


def _hgrn_mix(z, hgrn_lb, layer, norm_g, tt=128):
    t = z.shape[0]
    nk, nv = C_HEADS * C_DK, C_HEADS * C_DV
    lbp = jnp.zeros((8, nk), F32).at[0:DEPTH].set(hgrn_lb.astype(F32))
    return pl.pallas_call(
        functools.partial(_hgrn_kernel, layer), grid=(t // tt,),
        in_specs=[pl.BlockSpec((tt, C_PROJ), lambda i: (i, 0)),
                  _const_spec((8, nk)), _const_spec((1, nv))],
        out_specs=pl.BlockSpec((tt, nv), lambda i: (i, 0)),
        out_shape=jax.ShapeDtypeStruct((t, nv), F32),
        scratch_shapes=[pltpu.VMEM((C_HEADS, C_DV, C_DK), F32),
                        pltpu.VMEM((tt // CHUNK, 3, nk // LANES, CHUNK, LANES), F32),
                        pltpu.VMEM((tt // CHUNK, 2, nv // LANES, CHUNK, LANES), F32)],
        compiler_params=_cparams("arbitrary"), name="hgrn_mix")(
            z, lbp, norm_g.astype(F32).reshape(1, nv))


def _route(lt, rb_ref):
    rows = [lt[e:e + 1, :] for e in range(N_EXPERTS)]
    mx = rows[0]
    for r in rows[1:]:
        mx = jnp.maximum(mx, r)
    ex = [jnp.exp(r - mx) for r in rows]
    tot = ex[0]
    for r in ex[1:]:
        tot = tot + r
    scores = [r / tot for r in ex]
    sel = [scores[e] + rb_ref[e:e + 1, 0:1] for e in range(N_EXPERTS)]

    gscore = []
    for g in range(N_GROUPS):
        m = sel[g * EXPERTS_PER_GROUP:(g + 1) * EXPERTS_PER_GROUP]
        best = None
        for a in range(EXPERTS_PER_GROUP):
            for b2 in range(a + 1, EXPERTS_PER_GROUP):
                pair = m[a] + m[b2]
                best = pair if best is None else jnp.maximum(best, pair)
        gscore.append(best)
    g_best = gscore[0]
    g_idx = jnp.zeros_like(g_best, dtype=jnp.int32)
    for g in range(1, N_GROUPS):
        better = gscore[g] > g_best
        g_best = jnp.where(better, gscore[g], g_best)
        g_idx = jnp.where(better, g, g_idx)

    masked = [jnp.where(g_idx == (e // EXPERTS_PER_GROUP), sel[e], -jnp.inf) for e in range(N_EXPERTS)]

    def first_argmax(vals):
        best_v, best_i = vals[0], jnp.zeros_like(g_idx)
        for e in range(1, N_EXPERTS):
            better = vals[e] > best_v
            best_v = jnp.where(better, vals[e], best_v)
            best_i = jnp.where(better, e, best_i)
        return best_i

    i1 = first_argmax(masked)
    i2 = first_argmax([jnp.where(i1 == e, -jnp.inf, masked[e]) for e in range(N_EXPERTS)])
    w1 = jnp.zeros_like(mx)
    w2 = jnp.zeros_like(mx)
    for e in range(N_EXPERTS):
        w1 = jnp.where(i1 == e, scores[e], w1)
        w2 = jnp.where(i2 == e, scores[e], w2)
    wsum = w1 + w2
    return i1, i2, w1 / wsum, w2 / wsum


def _outproj_kernel(y_ref, x_ref, w_ref, g_ref, b_ref, rwh_ref, rwl_ref, rb_ref,
                    x1g_ref, meta_ref, cnt_ref, gt_scr, carry_ref):
    tm = x_ref.shape[0]

    @pl.when(pl.program_id(0) == 0)
    def _():
        carry_ref[...] = jnp.zeros(carry_ref.shape, F32)

    mix = jnp.dot(y_ref[...].astype(BF16), w_ref[...], preferred_element_type=F32)
    x1 = _layer_norm(DEEPNORM_ALPHA * x_ref[...] + mix, g_ref[...], b_ref[...])
    x1g_ref[:, 0:D_MODEL] = x1
    x_hi = x1.astype(BF16)
    x_lo = (x1 - x_hi.astype(F32)).astype(BF16)
    logits = (jnp.dot(x_hi, rwh_ref[...], preferred_element_type=F32)
              + (jnp.dot(x_lo, rwh_ref[...], preferred_element_type=F32)
                 + jnp.dot(x_hi, rwl_ref[...], preferred_element_type=F32)))
    i1, i2, w1, w2 = _route(logits.T, rb_ref)

    lo = jnp.minimum(i1, i2)
    hi = jnp.maximum(i1, i2)
    a = lo & (EXPERTS_PER_GROUP - 1)
    b2 = hi & (EXPERTS_PER_GROUP - 1)
    bucket = (lo >> 2) * PAIRS_PER_GROUP + ((a * (7 - a)) >> 1) + (b2 - a - 1)

    onehot = lax.broadcasted_iota(jnp.int32, (BUCKET_ROWS, tm), 0) == bucket
    r = lax.broadcasted_iota(jnp.int32, (tm, tm), 0)
    c = lax.broadcasted_iota(jnp.int32, (tm, tm), 1)
    before = jnp.dot(onehot.astype(BF16), (r < c).astype(BF16), preferred_element_type=F32)
    rank = jnp.sum(jnp.where(onehot, before + carry_ref[:, 0:1], 0.0), axis=0, keepdims=True)
    carry_ref[...] = carry_ref[...] + jnp.sum(onehot.astype(F32), axis=1, keepdims=True)
    cnt_ref[...] = carry_ref[...].astype(jnp.int32)

    meta_ref[...] = jnp.zeros(meta_ref.shape, jnp.int32)
    meta_ref[0:1, :] = bucket
    meta_ref[1:2, :] = rank.astype(jnp.int32)

    gt_scr[...] = jnp.zeros(gt_scr.shape, F32)
    gt_scr[0:1, :] = jnp.where(i1 == lo, w1, w2)
    gt_scr[1:2, :] = jnp.where(i1 == lo, w2, w1)
    x1g_ref[:, D_MODEL:] = gt_scr[...].T


def _outproj_ln_route(y, x, w_out, ln_g, ln_b, router_w, router_b, tm=512):
    t, d = x.shape
    rw = jnp.zeros((d, LANES), F32).at[:, 0:N_EXPERTS].set(router_w.astype(F32))
    rw_hi = rw.astype(BF16)
    rw_lo = (rw - rw_hi.astype(F32)).astype(BF16)
    rb = jnp.broadcast_to(router_b.astype(F32)[:, None], (N_EXPERTS, LANES))
    return pl.pallas_call(
        _outproj_kernel, grid=(t // tm,),
        in_specs=[pl.BlockSpec((tm, y.shape[1]), lambda i: (i, 0)), pl.BlockSpec((tm, d), lambda i: (i, 0)),
                  _const_spec(w_out.shape), _const_spec((1, d)), _const_spec((1, d)),
                  _const_spec((d, LANES)), _const_spec((d, LANES)), _const_spec((N_EXPERTS, LANES))],
        out_specs=[pl.BlockSpec((tm, d + LANES), lambda i: (i, 0)),
                   pl.BlockSpec((8, tm), lambda i: (0, i)), _const_spec((BUCKET_ROWS, LANES))],
        out_shape=[jax.ShapeDtypeStruct((t, d + LANES), F32),
                   jax.ShapeDtypeStruct((8, t), jnp.int32),
                   jax.ShapeDtypeStruct((BUCKET_ROWS, LANES), jnp.int32)],
        scratch_shapes=[pltpu.VMEM((LANES, tm), F32), pltpu.VMEM((BUCKET_ROWS, LANES), F32)],
        compiler_params=_cparams("arbitrary"), name="outproj_ln_route")(
            y, x, w_out, ln_g.astype(F32).reshape(1, d), ln_b.astype(F32).reshape(1, d), rw_hi, rw_lo, rb)


def _moe_plan(meta, cnt, tile_rows, n_tiles):
    counts = cnt[0:N_BUCKETS, 0]
    tiles = (counts + tile_rows - 1) // tile_rows
    tile_end = jnp.cumsum(tiles)
    offs = (tile_end - tiles) * tile_rows
    dest = (offs[meta[0]] + meta[1]).astype(jnp.int32)
    n_used = tile_end[-1]
    j = jnp.minimum(jnp.arange(n_tiles, dtype=jnp.int32), n_used - 1)
    bucket_of_tile = jnp.minimum(jnp.sum(tile_end[None, :] <= j[:, None], axis=1), N_BUCKETS - 1)
    pairs = [(a, b) for a in range(EXPERTS_PER_GROUP) for b in range(a + 1, EXPERTS_PER_GROUP)]
    lo_tab = jnp.array([g * EXPERTS_PER_GROUP + a for g in range(N_GROUPS) for a, _ in pairs], jnp.int32)
    hi_tab = jnp.array([g * EXPERTS_PER_GROUP + b for g in range(N_GROUPS) for _, b in pairs], jnp.int32)
    return dest, lo_tab[bucket_of_tile], hi_tab[bucket_of_tile], n_used.astype(jnp.int32).reshape(1)


def _rows_to_tiles(ref, x, row_tiles, first=0):
    n = x.shape[0]
    for c in range(x.shape[1] // LANES):
        ref[pl.ds(first + c, n, stride=row_tiles), :] = x[:, c * LANES:(c + 1) * LANES]


def _tiles_to_rows(ref, n, row_tiles, first=0, count=D_MODEL // LANES):
    return jnp.concatenate([ref[pl.ds(first + c, n, stride=row_tiles), :] for c in range(count)], axis=1)


def _dispatch_kernel(dest_ref, x1g_ref, xs_in_ref, xs_ref, comb_ref, sem):
    del xs_in_ref
    tm = x1g_ref.shape[0]
    base = pl.program_id(0) * tm
    _rows_to_tiles(comb_ref, x1g_ref[:, 0:D_MODEL], X_ROW_TILES)

    def row_copy(r):
        src = pl.multiple_of(r * X_ROW_TILES, X_ROW_TILES)
        dst = pl.multiple_of(dest_ref[base + r] * X_ROW_TILES, X_ROW_TILES)
        return pltpu.make_async_copy(comb_ref.at[pl.ds(src, X_ROW_TILES)], xs_ref.at[pl.ds(dst, X_ROW_TILES)], sem)

    def issue(r, carry):
        row_copy(r).start()
        return carry

    def drain(r, carry):
        row_copy(r).wait()
        return carry

    lax.fori_loop(0, tm, issue, 0, unroll=8)
    lax.fori_loop(0, tm, drain, 0, unroll=8)


def _dispatch(dest, x1g, xs, tm=256):
    t = x1g.shape[0]
    return pl.pallas_call(
        _dispatch_kernel,
        grid_spec=pltpu.PrefetchScalarGridSpec(
            num_scalar_prefetch=1, grid=(t // tm,),
            in_specs=[pl.BlockSpec((tm, x1g.shape[1]), lambda i, dest: (i, 0)),
                      pl.BlockSpec(memory_space=pl.ANY)],
            out_specs=pl.BlockSpec(memory_space=pl.ANY),
            scratch_shapes=[pltpu.VMEM((tm * X_ROW_TILES, LANES), F32), pltpu.SemaphoreType.DMA(())]),
        out_shape=jax.ShapeDtypeStruct(xs.shape, F32), input_output_aliases={2: 0},
        compiler_params=_cparams("arbitrary"), name="moe_dispatch")(dest, x1g, xs)


def _expert(x, wg_ref, wu_ref, wd_ref):
    hg = jnp.dot(x, wg_ref[0, 0], preferred_element_type=F32)
    hu = jnp.dot(x, wu_ref[0, 0], preferred_element_type=F32)
    return jnp.dot((_silu(hg) * hu).astype(BF16), wd_ref[0, 0], preferred_element_type=F32)


def _experts_kernel(lo_ref, hi_ref, nused_ref, xs_ref, wg0, wu0, wd0, wg1, wu1, wd1, ys_ref):
    del lo_ref, hi_ref
    tm = xs_ref.shape[0] // X_ROW_TILES

    @pl.when(pl.program_id(0) < nused_ref[0])
    def _():
        x = _tiles_to_rows(xs_ref, tm, X_ROW_TILES).astype(BF16)
        _rows_to_tiles(ys_ref, _expert(x, wg0, wu0, wd0), Y_ROW_TILES)
        _rows_to_tiles(ys_ref, _expert(x, wg1, wu1, wd1), Y_ROW_TILES, first=X_ROW_TILES)

    @pl.when(pl.program_id(0) >= nused_ref[0])
    def _():
        ys_ref[...] = jnp.zeros(ys_ref.shape, F32)


def _experts(layer, tile_lo, tile_hi, n_used, xs, wg, wu, wd, tm):
    d = D_MODEL
    n_tiles = xs.shape[0] // (tm * X_ROW_TILES)
    used = lambda j, n: jnp.maximum(jnp.minimum(j, n[0] - 1), 0)
    w_in = lambda which: pl.BlockSpec(
        (1, 1, d, D_EXPERT), lambda j, lo, hi, n: (layer, (lo, hi)[which][used(j, n)], 0, 0))
    w_out = lambda which: pl.BlockSpec(
        (1, 1, D_EXPERT, d), lambda j, lo, hi, n: (layer, (lo, hi)[which][used(j, n)], 0, 0))
    return pl.pallas_call(
        _experts_kernel,
        grid_spec=pltpu.PrefetchScalarGridSpec(
            num_scalar_prefetch=3, grid=(n_tiles,),
            in_specs=[pl.BlockSpec((tm * X_ROW_TILES, LANES), lambda j, lo, hi, n: (used(j, n), 0)),
                      w_in(0), w_in(0), w_out(0), w_in(1), w_in(1), w_out(1)],
            out_specs=pl.BlockSpec((tm * Y_ROW_TILES, LANES), lambda j, lo, hi, n: (j, 0))),
        out_shape=jax.ShapeDtypeStruct((n_tiles * tm * Y_ROW_TILES, LANES), F32),
        compiler_params=_cparams("arbitrary"), name="moe_experts")(
            tile_lo, tile_hi, n_used, xs, wg, wu, wd, wg, wu, wd)


def _combine_kernel(dest_ref, x1g_ref, p_ref, ys_ref, g_ref, b_ref, wpg_ref, wpp_ref, o_ref, ybuf, sems):
    i = pl.program_id(0)
    tm = x1g_ref.shape[0]
    slot = i % 2

    def row_copy(tile, s, r):
        src = pl.multiple_of(dest_ref[tile * tm + r] * Y_ROW_TILES, Y_ROW_TILES)
        dst = pl.multiple_of(r * Y_ROW_TILES, Y_ROW_TILES)
        return pltpu.make_async_copy(ys_ref.at[pl.ds(src, Y_ROW_TILES)],
                                     ybuf.at[s].at[pl.ds(dst, Y_ROW_TILES)], sems.at[s])

    def start_tile(tile, s):
        def issue(r, carry):
            row_copy(tile, s, r).start()
            return carry
        lax.fori_loop(0, tm, issue, 0, unroll=8)

    @pl.when(i == 0)
    def _():
        start_tile(0, 0)

    @pl.when(i + 1 < pl.num_programs(0))
    def _():
        start_tile(i + 1, 1 - slot)

    def drain(r, carry):
        row_copy(i, slot, r).wait()
        return carry

    lax.fori_loop(0, tm, drain, 0, unroll=8)
    g = x1g_ref[:, D_MODEL:]
    y = (g[:, 0:1] * _tiles_to_rows(ybuf.at[slot], tm, Y_ROW_TILES)
         + g[:, 1:2] * _tiles_to_rows(ybuf.at[slot], tm, Y_ROW_TILES, first=X_ROW_TILES))
    x2 = _layer_norm(DEEPNORM_ALPHA * x1g_ref[:, 0:D_MODEL] + y, g_ref[...], b_ref[...])
    gate = _sigmoid(jnp.dot(x2.astype(BF16), wpg_ref[...], preferred_element_type=F32))
    o_ref[...] = x2 + gate * jnp.dot(p_ref[...].astype(BF16), wpp_ref[...], preferred_element_type=F32)


def _combine_ln_ple(dest, x1g, p, ys, ln_g, ln_b, w_gate, w_proj, tm=256):
    t = x1g.shape[0]
    d = D_MODEL
    const = lambda shape: pl.BlockSpec(shape, lambda i, dest: (0,) * len(shape))
    return pl.pallas_call(
        _combine_kernel,
        grid_spec=pltpu.PrefetchScalarGridSpec(
            num_scalar_prefetch=1, grid=(t // tm,),
            in_specs=[pl.BlockSpec((tm, x1g.shape[1]), lambda i, dest: (i, 0)),
                      pl.BlockSpec((tm, PLE_DIM), lambda i, dest: (i, 0)),
                      pl.BlockSpec(memory_space=pl.ANY), const((1, d)), const((1, d)),
                      const((d, d)), const((PLE_DIM, d))],
            out_specs=pl.BlockSpec((tm, d), lambda i, dest: (i, 0)),
            scratch_shapes=[pltpu.VMEM((2, tm * Y_ROW_TILES, LANES), F32), pltpu.SemaphoreType.DMA((2,))]),
        out_shape=jax.ShapeDtypeStruct((t, d), F32),
        compiler_params=_cparams("arbitrary"), name="moe_combine_ln_ple")(
            dest, x1g, p, ys, ln_g.astype(F32).reshape(1, d), ln_b.astype(F32).reshape(1, d), w_gate, w_proj)


def _pad_cols(w, n):
    return jnp.zeros((w.shape[0], n), w.dtype).at[:, 0:w.shape[1]].set(w)


def kernel(x, p, ln1_g, ln1_b, ln2_g, ln2_b, mlstm_w_in, mlstm_conv, mlstm_gate_b, mlstm_norm_g, mlstm_w_out, gla_w_in, gla_w_a2, gla_b_a, gla_norm_g, gla_w_out, hgrn_w_in, hgrn_lb, hgrn_norm_g, hgrn_w_out, router_w, router_b, exp_w_gate, exp_w_up, exp_w_down, ple_w_proj, ple_w_gate):
    bsz, seq, d = x.shape
    assert bsz == 1 and d == D_MODEL and seq % 1024 == 0
    xt = x.reshape(seq, d).astype(F32)
    n_tiles = seq // MOE_TILE + N_BUCKETS
    xs = jnp.zeros((n_tiles * MOE_TILE * X_ROW_TILES, LANES), F32)
    wg_all, wu_all, wd_all = exp_w_gate.astype(BF16), exp_w_up.astype(BF16), exp_w_down.astype(BF16)
    for i in range(DEPTH):
        kind, slot = i % N_MIXERS, i // N_MIXERS
        if kind == 0:
            w_in = mlstm_w_in[slot]
            w = _pad_cols(w_in.astype(BF16), A_PROJ_PAD)
            wgt = jnp.zeros((16, d), BF16).at[0:2 * A_HEADS].set(w_in[:, A_MAIN:].T.astype(BF16))
            z, gt = _inproj(xt, w, wgt)
            y = _mlstm_mix(z, gt, mlstm_conv[slot], mlstm_gate_b[slot], mlstm_norm_g[slot])
            w_out = mlstm_w_out[slot]
        elif kind == 1:
            z = _inproj(xt, _pad_cols(gla_w_in[slot].astype(BF16), B_PROJ_PAD))
            y = _gla_mix(z, gla_w_a2[slot], gla_b_a[slot], gla_norm_g[slot])
            w_out = gla_w_out[slot]
        else:
            z = _inproj(xt, hgrn_w_in[slot].astype(BF16))
            y = _hgrn_mix(z, hgrn_lb, i, hgrn_norm_g[slot])
            w_out = hgrn_w_out[slot]
        x1g, meta, cnt = _outproj_ln_route(y, xt, w_out.astype(BF16), ln1_g[i], ln1_b[i], router_w, router_b)
        dest, tile_lo, tile_hi, n_used = _moe_plan(meta, cnt, MOE_TILE, n_tiles)
        xs = _dispatch(dest, x1g, xs)
        ys = _experts(i, tile_lo, tile_hi, n_used, xs, wg_all, wu_all, wd_all, MOE_TILE)
        xt = _combine_ln_ple(dest, x1g, p[i].reshape(seq, PLE_DIM), ys, ln2_g[i], ln2_b[i],
                             ple_w_gate[i].astype(BF16), ple_w_proj[i].astype(BF16))
    return xt.reshape(bsz, seq, d)
```

```python
import functools

import jax
import jax.numpy as jnp
from jax import lax
from jax.experimental import pallas as pl
from jax.experimental.pallas import tpu as pltpu

F32 = jnp.float32
BF16 = jnp.bfloat16

D_MODEL = 1024
DEPTH = 4
CHUNK = 64
PLE_DIM = 256
N_MIXERS = 3
LN_EPS = 1e-5
DEEPNORM_ALPHA = (2 * DEPTH) ** 0.25
A_HEADS, A_DK, A_DV = 4, 128, 256
B_HEADS, B_DK, B_DV = 4, 128, 256
B_RANK = 16
B_TAU = 16.0
C_HEADS, C_DK, C_DV = 8, 128, 128
N_EXPERTS = 16
N_GROUPS = 4
EXPERTS_PER_GROUP = 4
D_EXPERT = 512
PAIRS_PER_GROUP = EXPERTS_PER_GROUP * (EXPERTS_PER_GROUP - 1) // 2
N_BUCKETS = N_GROUPS * PAIRS_PER_GROUP
BUCKET_ROWS = 32
MOE_TILE = 256
DMA_ISSUE_UNROLL = 64

LANES = 128
SLOT = 8
X_ROW_TILES = D_MODEL // LANES
Y_ROW_TILES = 2 * X_ROW_TILES
N_SLOTS = CHUNK // SLOT
A_MAIN = 2 * A_HEADS * A_DK + 2 * A_HEADS * A_DV
A_PROJ_PAD = A_MAIN + LANES
B_MAIN = 2 * B_HEADS * B_DK + 2 * B_HEADS * B_DV
B_PROJ_PAD = B_MAIN + LANES
C_PROJ = 2 * C_HEADS * C_DK + 2 * C_HEADS * C_DV
VMEM_LIMIT = 56 * 1024 * 1024

NT_DIMS = (((1,), (1,)), ((), ()))
TN_DIMS = (((0,), (0,)), ((), ()))


def _cparams(*sem):
    return pltpu.CompilerParams(dimension_semantics=sem, vmem_limit_bytes=VMEM_LIMIT)


def _sigmoid(x):
    return 0.5 * jnp.tanh(0.5 * x) + 0.5


def _silu(x):
    return x * _sigmoid(x)


def _log_sigmoid(x):
    return jnp.minimum(x, 0.0) - jnp.log(1.0 + jnp.exp(-jnp.abs(x)))


def _layer_norm(u, g, b):
    mu = jnp.mean(u, axis=-1, keepdims=True)
    d = u - mu
    var = jnp.mean(d * d, axis=-1, keepdims=True)
    return d * lax.rsqrt(var + LN_EPS) * g + b


def _tri(n, lower):
    r = lax.broadcasted_iota(jnp.int32, (n, n), 0)
    c = lax.broadcasted_iota(jnp.int32, (n, n), 1)
    return (r >= c) if lower else (r <= c)


def _bdot(a, b):
    return jnp.dot(a.astype(BF16), b.astype(BF16), preferred_element_type=F32)


def _bdot_nt(a, b):
    return lax.dot_general(a.astype(BF16), b.astype(BF16), NT_DIMS, preferred_element_type=F32)


def _bdot_tn(a, b):
    return lax.dot_general(a.astype(BF16), b.astype(BF16), TN_DIMS, preferred_element_type=F32)


def _const_spec(shape):
    return pl.BlockSpec(shape, lambda i: (0,) * len(shape))


def _inproj_kernel(x_ref, w_ref, z_ref):
    z_ref[...] = jnp.dot(x_ref[...].astype(BF16), w_ref[...], preferred_element_type=F32)


def _inproj_gates_kernel(x_ref, w_ref, wgt_ref, z_ref, gt_ref):
    xb = x_ref[...].astype(BF16)
    z_ref[...] = jnp.dot(xb, w_ref[...], preferred_element_type=F32)
    for c in range(gt_ref.shape[0]):
        gt_ref[c] = lax.dot_general(wgt_ref[...], xb[c * CHUNK:(c + 1) * CHUNK], NT_DIMS,
                                    preferred_element_type=F32)


def _inproj(x, w, wgt=None, tm=512):
    t, d = x.shape
    n = w.shape[1]
    grid = (t // tm,)
    x_spec = pl.BlockSpec((tm, d), lambda i: (i, 0))
    z_spec = pl.BlockSpec((tm, n), lambda i: (i, 0))
    if wgt is None:
        return pl.pallas_call(
            _inproj_kernel, grid=grid, in_specs=[x_spec, _const_spec((d, n))], out_specs=z_spec,
            out_shape=jax.ShapeDtypeStruct((t, n), F32), compiler_params=_cparams("parallel"),
            name="inproj")(x, w)
    return pl.pallas_call(
        _inproj_gates_kernel, grid=grid,
        in_specs=[x_spec, _const_spec((d, n)), _const_spec(wgt.shape)],
        out_specs=[z_spec, pl.BlockSpec((tm // CHUNK, 16, CHUNK), lambda i: (i, 0, 0))],
        out_shape=[jax.ShapeDtypeStruct((t, n), F32),
                   jax.ShapeDtypeStruct((t // CHUNK, 16, CHUNK), F32)],
        compiler_params=_cparams("parallel"), name="inproj_gates")(x, w, wgt)


def _mlstm_kernel(z_ref, gt_ref, conv_ref, gbr_ref, gbc_ref, ng_ref, y_ref,
                  ext_ref, ct_ref, n_ref, m_ref):
    tt = z_ref.shape[0]
    nqk = A_HEADS * A_DK

    @pl.when(pl.program_id(0) == 0)
    def _():
        ext_ref[0:8, :] = jnp.zeros((8, 2 * nqk), F32)
        ct_ref[...] = jnp.zeros(ct_ref.shape, F32)
        n_ref[...] = jnp.zeros(n_ref.shape, F32)
        m_ref[...] = jnp.zeros(m_ref.shape, F32)

    qk_raw = z_ref[:, 0:2 * nqk]
    ext_ref[8:8 + tt, :] = qk_raw
    acc = conv_ref[3:4, :] * qk_raw
    for s in range(1, 4):
        acc = acc + conv_ref[3 - s:4 - s, :] * ext_ref[pl.ds(8 - s, tt), :]
    ext_ref[0:8, :] = qk_raw[tt - 8:tt, :]
    qkc = _silu(acc)

    tri_lo = _tri(CHUNK, True)
    tri_lo_f = tri_lo.astype(F32)
    tri_up_f = _tri(CHUNK, False).astype(F32)

    for c in range(tt // CHUNK):
        r0 = c * CHUNK
        gc = z_ref[r0:r0 + CHUNK, A_MAIN:A_MAIN + LANES] + gbr_ref[...]
        a_col = jnp.dot(tri_lo_f, _log_sigmoid(gc), preferred_element_type=F32,
                        precision=lax.Precision.HIGHEST)
        gr = gt_ref[c] + gbc_ref[:, 0:1]
        a_row = jnp.dot(_log_sigmoid(gr), tri_up_f, preferred_element_type=F32,
                        precision=lax.Precision.HIGHEST)
        for h in range(A_HEADS):
            ac = a_col[:, A_HEADS + h:A_HEADS + h + 1]
            lic = gc[:, h:h + 1]
            ar = a_row[A_HEADS + h:A_HEADS + h + 1, :]
            lir = gr[h:h + 1, :]
            m_st = m_ref[h:h + 1, 0:1]
            qh = qkc[r0:r0 + CHUNK, h * A_DK:(h + 1) * A_DK]
            kh = qkc[r0:r0 + CHUNK, nqk + h * A_DK:nqk + (h + 1) * A_DK] * (A_DK ** -0.5)
            vh = z_ref[r0:r0 + CHUNK, 2 * nqk + h * A_DV:2 * nqk + (h + 1) * A_DV]
            oh = z_ref[r0:r0 + CHUNK, 2 * nqk + A_HEADS * A_DV + h * A_DV:
                       2 * nqk + A_HEADS * A_DV + (h + 1) * A_DV]

            d_log = jnp.where(tri_lo, ac - ar + lir, -jnp.inf)
            d_max = jnp.max(d_log, axis=-1, keepdims=True)
            s = _bdot_nt(qh, kh) * jnp.exp(d_log - d_max)
            sv = _bdot(s, vh)
            s_sum = jnp.sum(s, axis=-1, keepdims=True)
            a_tot = ar[:, CHUNK - 1:CHUNK]
            g_max = jnp.max(a_tot - ar + lir, axis=-1, keepdims=True)
            wk = jnp.exp(a_tot - ac + lic - g_max) * kh
            c_inc = _bdot_tn(vh, wk)
            n_inc = jnp.sum(wk, axis=0, keepdims=True)

            e_log = ac + m_st
            m_row = jnp.maximum(e_log, d_max)
            w_inter = jnp.exp(e_log - m_row)
            w_intra = jnp.exp(d_max - m_row)
            num = w_inter * _bdot_nt(qh, ct_ref[h]) + w_intra * sv
            den = (w_inter * jnp.sum(qh * n_ref[h:h + 1, :], axis=-1, keepdims=True) + w_intra * s_sum)
            hout = num / jnp.maximum(jnp.abs(den), jnp.exp(-m_row))
            m_new = jnp.maximum(a_tot + m_st, g_max)
            decay = jnp.exp(a_tot + m_st - m_new)
            inc_scale = jnp.exp(g_max - m_new)
            ct_ref[h] = decay * ct_ref[h] + inc_scale * c_inc
            n_ref[h:h + 1, :] = decay * n_ref[h:h + 1, :] + inc_scale * n_inc
            m_ref[h:h + 1, :] = jnp.broadcast_to(m_new, (1, LANES))

            hn = hout * lax.rsqrt(jnp.mean(hout * hout, axis=-1, keepdims=True) + LN_EPS)
            y_ref[r0:r0 + CHUNK, h * A_DV:(h + 1) * A_DV] = (
                hn * ng_ref[:, h * A_DV:(h + 1) * A_DV] * _sigmoid(oh))


def _mlstm_mix(z, gt, conv_w, gate_b, norm_g, tt=256):
    t = z.shape[0]
    nqk2 = 2 * A_HEADS * A_DK
    conv_p = jnp.zeros((8, nqk2), F32).at[0:4].set(conv_w.astype(F32))
    gb = gate_b.astype(F32)
    gb_row = jnp.zeros((1, LANES), F32).at[0, 0:2 * A_HEADS].set(gb)
    gb_col = jnp.zeros((16, LANES), F32).at[0:2 * A_HEADS, :].set(jnp.broadcast_to(gb[:, None], (2 * A_HEADS, LANES)))
    ng = norm_g.astype(F32).reshape(1, A_HEADS * A_DV)
    return pl.pallas_call(
        _mlstm_kernel, grid=(t // tt,),
        in_specs=[pl.BlockSpec((tt, A_PROJ_PAD), lambda i: (i, 0)),
                  pl.BlockSpec((tt // CHUNK, 16, CHUNK), lambda i: (i, 0, 0)),
                  _const_spec((8, nqk2)), _const_spec((1, LANES)), _const_spec((16, LANES)),
                  _const_spec((1, A_HEADS * A_DV))],
        out_specs=pl.BlockSpec((tt, A_HEADS * A_DV), lambda i: (i, 0)),
        out_shape=jax.ShapeDtypeStruct((t, A_HEADS * A_DV), F32),
        scratch_shapes=[pltpu.VMEM((tt + 8, nqk2), F32),
                        pltpu.VMEM((A_HEADS, A_DV, A_DK), F32),
                        pltpu.VMEM((8, A_DK), F32),
                        pltpu.VMEM((8, LANES), F32)],
        compiler_params=_cparams("arbitrary"), name="mlstm_mix")(z, gt, conv_p, gb_row, gb_col, ng)


def _store_lane_tiles(ref, x):
    for t in range(ref.shape[0]):
        ref[t] = x[:, t * LANES:(t + 1) * LANES]


def _interleaved(ref):
    return jnp.concatenate(
        [jnp.concatenate([ref[t, pl.ds(c, N_SLOTS, stride=SLOT), :] for t in range(ref.shape[0])], axis=1)
         for c in range(SLOT)], axis=0)


def _pair_factors(q, k, b, ref_of_slot):
    zero = jnp.zeros((SLOT, q.shape[1]), F32)
    factors = []
    for half in (1, 2, 4):
        qs, ks = [], []
        for u in range(N_SLOTS):
            r = ref_of_slot((u // (2 * half)) * 2 * half + half - 1)
            rows = slice(u * SLOT, (u + 1) * SLOT)
            if u % (2 * half) >= half:
                qs.append(q[rows] * jnp.exp(b[rows] - r))
                ks.append(zero)
            else:
                qs.append(zero)
                ks.append(k[rows] * jnp.exp(r - b[rows]))
        factors.append((jnp.concatenate(qs, axis=0), jnp.concatenate(ks, axis=0)))
    return factors


def _pair_masks():
    r = lax.broadcasted_iota(jnp.int32, (CHUNK, CHUNK), 0)
    c = lax.broadcasted_iota(jnp.int32, (CHUNK, CHUNK), 1)
    sr, sc = r >> 3, c >> 3
    same_sublane = (r & (SLOT - 1)) == (c & (SLOT - 1))
    natural, interleaved = [], []
    for shift, half in ((1, 1), (2, 2), (3, 4)):
        m = jnp.logical_and((sr >> shift) == (sc >> shift),
                            jnp.logical_and((sr & half) != 0, (sc & half) == 0))
        natural.append(m)
        interleaved.append(jnp.logical_and(m, same_sublane))
    return r == c, natural, interleaved


def _decay_attention_chunk(q, k, v, la, st_ref, qkb_ref, vo_ref, heads, dv):
    dk = LANES
    tri = _tri(CHUNK, True).astype(BF16)
    la_1 = la.astype(BF16)
    rest = la - la_1.astype(F32)
    la_2 = rest.astype(BF16)
    la_3 = (rest - la_2.astype(F32)).astype(BF16)
    b = (jnp.dot(tri, la_1, preferred_element_type=F32)
         + (jnp.dot(tri, la_2, preferred_element_type=F32) + jnp.dot(tri, la_3, preferred_element_type=F32)))
    b_last = b[CHUNK - 1:CHUNK, :]
    qe = q * jnp.exp(b)
    kd = k * jnp.exp(b_last - b)

    _store_lane_tiles(qkb_ref.at[0], q)
    _store_lane_tiles(qkb_ref.at[1], k)
    _store_lane_tiles(qkb_ref.at[2], b)
    _store_lane_tiles(vo_ref.at[0], v)
    q_i, k_i, b_i = _interleaved(qkb_ref.at[0]), _interleaved(qkb_ref.at[1]), _interleaved(qkb_ref.at[2])
    v_i = _interleaved(vo_ref.at[0])

    eye, masks_n, masks_i = _pair_masks()
    fact_n = _pair_factors(q, k, b, lambda s: b[s * SLOT + SLOT - 1:(s + 1) * SLOT, :])
    fact_i = _pair_factors(q_i, k_i, b_i, lambda s: b_i[s * SLOT:(s + 1) * SLOT, :])

    outs = []
    for h in range(heads):
        kc = slice(h * dk, (h + 1) * dk)
        vc = slice(h * dv, (h + 1) * dv)
        att = jnp.where(eye, _bdot_nt(q[:, kc], k[:, kc]), 0.0)
        for (qf, kf), m in zip(fact_n, masks_n):
            att = att + jnp.where(m, _bdot_nt(qf[:, kc], kf[:, kc]), 0.0)
        att_i = jnp.zeros((CHUNK, CHUNK), F32)
        for (qf, kf), m in zip(fact_i, masks_i):
            att_i = att_i + jnp.where(m, _bdot_nt(qf[:, kc], kf[:, kc]), 0.0)
        outs.append(_bdot_nt(qe[:, kc], st_ref[h]) + _bdot(att, v[:, vc]))
        _store_lane_tiles(vo_ref.at[1, h * dv // LANES:(h + 1) * dv // LANES], _bdot(att_i, v_i[:, vc]))
        st_ref[h] = st_ref[h] * jnp.exp(b_last[:, kc]) + _bdot_tn(v[:, vc], kd[:, kc])
    o_i = _interleaved(vo_ref.at[1])
    return [o + o_i[:, h * dv:(h + 1) * dv] for h, o in enumerate(outs)]


def _head_norm_store(outs, y_ref, r0, ng_ref, gate_act, dv):
    for h, o in enumerate(outs):
        hn = o * lax.rsqrt(jnp.mean(o * o, axis=-1, keepdims=True) + LN_EPS)
        y_ref[r0:r0 + CHUNK, h * dv:(h + 1) * dv] = (
            hn * ng_ref[:, h * dv:(h + 1) * dv] * gate_act[:, h * dv:(h + 1) * dv])


def _gla_kernel(z_ref, wa2_ref, ba_ref, ng_ref, y_ref, st_ref, qkb_ref, vo_ref):
    @pl.when(pl.program_id(0) == 0)
    def _():
        st_ref[...] = jnp.zeros(st_ref.shape, F32)

    nk, nv = B_HEADS * B_DK, B_HEADS * B_DV
    for c in range(z_ref.shape[0] // CHUNK):
        r0 = c * CHUNK
        q = z_ref[r0:r0 + CHUNK, 0:nk] * (B_DK ** -0.5)
        k = z_ref[r0:r0 + CHUNK, nk:2 * nk]
        v = z_ref[r0:r0 + CHUNK, 2 * nk:2 * nk + nv]
        g = z_ref[r0:r0 + CHUNK, 2 * nk + nv:2 * nk + 2 * nv]
        a_lr = z_ref[r0:r0 + CHUNK, B_MAIN:B_MAIN + LANES]
        la = _log_sigmoid(jnp.dot(a_lr.astype(BF16), wa2_ref[...], preferred_element_type=F32)
                          + ba_ref[...]) / B_TAU
        outs = _decay_attention_chunk(q, k, v, la, st_ref, qkb_ref.at[c], vo_ref.at[c], B_HEADS, B_DV)
        _head_norm_store(outs, y_ref, r0, ng_ref, _silu(g), B_DV)


def _gla_mix(z, w_a2, b_a, norm_g, tt=128):
    t = z.shape[0]
    nk, nv = B_HEADS * B_DK, B_HEADS * B_DV
    wa2 = jnp.zeros((LANES, nk), BF16).at[0:B_RANK].set(w_a2.astype(BF16))
    return pl.pallas_call(
        _gla_kernel, grid=(t // tt,),
        in_specs=[pl.BlockSpec((tt, B_PROJ_PAD), lambda i: (i, 0)),
                  _const_spec((LANES, nk)), _const_spec((1, nk)), _const_spec((1, nv))],
        out_specs=pl.BlockSpec((tt, nv), lambda i: (i, 0)),
        out_shape=jax.ShapeDtypeStruct((t, nv), F32),
        scratch_shapes=[pltpu.VMEM((B_HEADS, B_DV, B_DK), F32),
                        pltpu.VMEM((tt // CHUNK, 3, nk // LANES, CHUNK, LANES), F32),
                        pltpu.VMEM((tt // CHUNK, 2, nv // LANES, CHUNK, LANES), F32)],
        compiler_params=_cparams("arbitrary"), name="gla_mix")(
            z, wa2, b_a.astype(F32).reshape(1, nk), norm_g.astype(F32).reshape(1, nv))


def _hgrn_kernel(layer, z_ref, lbp_ref, ng_ref, y_ref, st_ref, qkb_ref, vo_ref):
    @pl.when(pl.program_id(0) == 0)
    def _():
        st_ref[...] = jnp.zeros(st_ref.shape, F32)

    lbp = lbp_ref[0:DEPTH, :]
    e = jnp.exp(lbp - jnp.max(lbp, axis=0, keepdims=True))
    sm = e / jnp.sum(e, axis=0, keepdims=True)
    lb = jnp.zeros((1, sm.shape[1]), F32)
    for r in range(layer + 1):
        lb = lb + sm[r:r + 1, :]
    lb = lb - sm[0:1, :]

    nk, nv = C_HEADS * C_DK, C_HEADS * C_DV
    for c in range(z_ref.shape[0] // CHUNK):
        r0 = c * CHUNK
        q = _silu(z_ref[r0:r0 + CHUNK, 0:nk])
        fg = lb + (1.0 - lb) * _sigmoid(z_ref[r0:r0 + CHUNK, nk:2 * nk])
        v = z_ref[r0:r0 + CHUNK, 2 * nk:2 * nk + nv]
        g = z_ref[r0:r0 + CHUNK, 2 * nk + nv:2 * nk + 2 * nv]
        outs = _decay_attention_chunk(q, 1.0 - fg, v, jnp.log(fg), st_ref, qkb_ref.at[c], vo_ref.at[c],
                                      C_HEADS, C_DV)
        _head_norm_store(outs, y_ref, r0, ng_ref, _sigmoid(g), C_DV)


def _hgrn_mix(z, hgrn_lb, layer, norm_g, tt=128):
    t = z.shape[0]
    nk, nv = C_HEADS * C_DK, C_HEADS * C_DV
    lbp = jnp.zeros((8, nk), F32).at[0:DEPTH].set(hgrn_lb.astype(F32))
    return pl.pallas_call(
        functools.partial(_hgrn_kernel, layer), grid=(t // tt,),
        in_specs=[pl.BlockSpec((tt, C_PROJ), lambda i: (i, 0)),
                  _const_spec((8, nk)), _const_spec((1, nv))],
        out_specs=pl.BlockSpec((tt, nv), lambda i: (i, 0)),
        out_shape=jax.ShapeDtypeStruct((t, nv), F32),
        scratch_shapes=[pltpu.VMEM((C_HEADS, C_DV, C_DK), F32),
                        pltpu.VMEM((tt // CHUNK, 3, nk // LANES, CHUNK, LANES), F32),
                        pltpu.VMEM((tt // CHUNK, 2, nv // LANES, CHUNK, LANES), F32)],
        compiler_params=_cparams("arbitrary"), name="hgrn_mix")(
            z, lbp, norm_g.astype(F32).reshape(1, nv))


def _route(lt, rb_ref):
    rows = [lt[e:e + 1, :] for e in range(N_EXPERTS)]
    mx = rows[0]
    for r in rows[1:]:
        mx = jnp.maximum(mx, r)
    ex = [jnp.exp(r - mx) for r in rows]
    tot = ex[0]
    for r in ex[1:]:
        tot = tot + r
    scores = [r / tot for r in ex]
    sel = [scores[e] + rb_ref[e:e + 1, 0:1] for e in range(N_EXPERTS)]

    gscore = []
    for g in range(N_GROUPS):
        m = sel[g * EXPERTS_PER_GROUP:(g + 1) * EXPERTS_PER_GROUP]
        best = None
        for a in range(EXPERTS_PER_GROUP):
            for b2 in range(a + 1, EXPERTS_PER_GROUP):
                pair = m[a] + m[b2]
                best = pair if best is None else jnp.maximum(best, pair)
        gscore.append(best)
    g_best = gscore[0]
    g_idx = jnp.zeros_like(g_best, dtype=jnp.int32)
    for g in range(1, N_GROUPS):
        better = gscore[g] > g_best
        g_best = jnp.where(better, gscore[g], g_best)
        g_idx = jnp.where(better, g, g_idx)

    masked = [jnp.where(g_idx == (e // EXPERTS_PER_GROUP), sel[e], -jnp.inf) for e in range(N_EXPERTS)]

    def first_argmax(vals):
        best_v, best_i = vals[0], jnp.zeros_like(g_idx)
        for e in range(1, N_EXPERTS):
            better = vals[e] > best_v
            best_v = jnp.where(better, vals[e], best_v)
            best_i = jnp.where(better, e, best_i)
        return best_i

    i1 = first_argmax(masked)
    i2 = first_argmax([jnp.where(i1 == e, -jnp.inf, masked[e]) for e in range(N_EXPERTS)])
    w1 = jnp.zeros_like(mx)
    w2 = jnp.zeros_like(mx)
    for e in range(N_EXPERTS):
        w1 = jnp.where(i1 == e, scores[e], w1)
        w2 = jnp.where(i2 == e, scores[e], w2)
    wsum = w1 + w2
    return i1, i2, w1 / wsum, w2 / wsum


def _outproj_kernel(y_ref, x_ref, w_ref, g_ref, b_ref, rwh_ref, rwl_ref, rb_ref,
                    x1g_ref, meta_ref, cnt_ref, gt_scr, carry_ref):
    tm = x_ref.shape[0]

    @pl.when(pl.program_id(0) == 0)
    def _():
        carry_ref[...] = jnp.zeros(carry_ref.shape, F32)

    mix = jnp.dot(y_ref[...].astype(BF16), w_ref[...], preferred_element_type=F32)
    x1 = _layer_norm(DEEPNORM_ALPHA * x_ref[...] + mix, g_ref[...], b_ref[...])
    x1g_ref[:, 0:D_MODEL] = x1
    x_hi = x1.astype(BF16)
    x_lo = (x1 - x_hi.astype(F32)).astype(BF16)
    logits = (jnp.dot(x_hi, rwh_ref[...], preferred_element_type=F32)
              + (jnp.dot(x_lo, rwh_ref[...], preferred_element_type=F32)
                 + jnp.dot(x_hi, rwl_ref[...], preferred_element_type=F32)))
    i1, i2, w1, w2 = _route(logits.T, rb_ref)

    lo = jnp.minimum(i1, i2)
    hi = jnp.maximum(i1, i2)
    a = lo & (EXPERTS_PER_GROUP - 1)
    b2 = hi & (EXPERTS_PER_GROUP - 1)
    bucket = (lo >> 2) * PAIRS_PER_GROUP + ((a * (7 - a)) >> 1) + (b2 - a - 1)

    onehot = lax.broadcasted_iota(jnp.int32, (BUCKET_ROWS, tm), 0) == bucket
    r = lax.broadcasted_iota(jnp.int32, (tm, tm), 0)
    c = lax.broadcasted_iota(jnp.int32, (tm, tm), 1)
    before = jnp.dot(onehot.astype(BF16), (r < c).astype(BF16), preferred_element_type=F32)
    rank = jnp.sum(jnp.where(onehot, before + carry_ref[:, 0:1], 0.0), axis=0, keepdims=True)
    carry_ref[...] = carry_ref[...] + jnp.sum(onehot.astype(F32), axis=1, keepdims=True)
    cnt_ref[...] = carry_ref[...].astype(jnp.int32)

    meta_ref[...] = jnp.zeros(meta_ref.shape, jnp.int32)
    meta_ref[0:1, :] = bucket
    meta_ref[1:2, :] = rank.astype(jnp.int32)

    gt_scr[...] = jnp.zeros(gt_scr.shape, F32)
    gt_scr[0:1, :] = jnp.where(i1 == lo, w1, w2)
    gt_scr[1:2, :] = jnp.where(i1 == lo, w2, w1)
    x1g_ref[:, D_MODEL:] = gt_scr[...].T


def _outproj_ln_route(y, x, w_out, ln_g, ln_b, router_w, router_b, tm=512):
    t, d = x.shape
    rw = jnp.zeros((d, LANES), F32).at[:, 0:N_EXPERTS].set(router_w.astype(F32))
    rw_hi = rw.astype(BF16)
    rw_lo = (rw - rw_hi.astype(F32)).astype(BF16)
    rb = jnp.broadcast_to(router_b.astype(F32)[:, None], (N_EXPERTS, LANES))
    return pl.pallas_call(
        _outproj_kernel, grid=(t // tm,),
        in_specs=[pl.BlockSpec((tm, y.shape[1]), lambda i: (i, 0)), pl.BlockSpec((tm, d), lambda i: (i, 0)),
                  _const_spec(w_out.shape), _const_spec((1, d)), _const_spec((1, d)),
                  _const_spec((d, LANES)), _const_spec((d, LANES)), _const_spec((N_EXPERTS, LANES))],
        out_specs=[pl.BlockSpec((tm, d + LANES), lambda i: (i, 0)),
                   pl.BlockSpec((8, tm), lambda i: (0, i)), _const_spec((BUCKET_ROWS, LANES))],
        out_shape=[jax.ShapeDtypeStruct((t, d + LANES), F32),
                   jax.ShapeDtypeStruct((8, t), jnp.int32),
                   jax.ShapeDtypeStruct((BUCKET_ROWS, LANES), jnp.int32)],
        scratch_shapes=[pltpu.VMEM((LANES, tm), F32), pltpu.VMEM((BUCKET_ROWS, LANES), F32)],
        compiler_params=_cparams("arbitrary"), name="outproj_ln_route")(
            y, x, w_out, ln_g.astype(F32).reshape(1, d), ln_b.astype(F32).reshape(1, d), rw_hi, rw_lo, rb)


def _moe_plan(meta, cnt, tile_rows, n_tiles):
    counts = cnt[0:N_BUCKETS, 0]
    tiles = (counts + tile_rows - 1) // tile_rows
    tile_end = jnp.cumsum(tiles)
    offs = (tile_end - tiles) * tile_rows
    dest = (offs[meta[0]] + meta[1]).astype(jnp.int32)
    n_used = tile_end[-1]
    j = jnp.minimum(jnp.arange(n_tiles, dtype=jnp.int32), n_used - 1)
    bucket_of_tile = jnp.minimum(jnp.sum(tile_end[None, :] <= j[:, None], axis=1), N_BUCKETS - 1)
    pairs = [(a, b) for a in range(EXPERTS_PER_GROUP) for b in range(a + 1, EXPERTS_PER_GROUP)]
    lo_tab = jnp.array([g * EXPERTS_PER_GROUP + a for g in range(N_GROUPS) for a, _ in pairs], jnp.int32)
    hi_tab = jnp.array([g * EXPERTS_PER_GROUP + b for g in range(N_GROUPS) for _, b in pairs], jnp.int32)
    return dest, lo_tab[bucket_of_tile], hi_tab[bucket_of_tile], n_used.astype(jnp.int32).reshape(1)


def _rows_to_tiles(ref, x, row_tiles, first=0):
    n = x.shape[0]
    for c in range(x.shape[1] // LANES):
        ref[pl.ds(first + c, n, stride=row_tiles), :] = x[:, c * LANES:(c + 1) * LANES]


def _tiles_to_rows(ref, n, row_tiles, first=0, count=D_MODEL // LANES):
    return jnp.concatenate([ref[pl.ds(first + c, n, stride=row_tiles), :] for c in range(count)], axis=1)


def _dispatch_kernel(dest_ref, x1g_ref, xs_in_ref, xs_ref, comb_ref, sem):
    del xs_in_ref
    tm = x1g_ref.shape[0]
    base = pl.program_id(0) * tm
    _rows_to_tiles(comb_ref, x1g_ref[:, 0:D_MODEL], X_ROW_TILES)

    def row_copy(r):
        src = pl.multiple_of(r * X_ROW_TILES, X_ROW_TILES)
        dst = pl.multiple_of(dest_ref[base + r] * X_ROW_TILES, X_ROW_TILES)
        return pltpu.make_async_copy(comb_ref.at[pl.ds(src, X_ROW_TILES)], xs_ref.at[pl.ds(dst, X_ROW_TILES)], sem)

    def issue(r, carry):
        row_copy(r).start()
        return carry

    lax.fori_loop(0, tm, issue, 0, unroll=DMA_ISSUE_UNROLL)
    pltpu.make_async_copy(comb_ref, xs_ref.at[pl.ds(0, tm * X_ROW_TILES)], sem).wait()


def _dispatch(dest, x1g, xs, tm=256):
    t = x1g.shape[0]
    return pl.pallas_call(
        _dispatch_kernel,
        grid_spec=pltpu.PrefetchScalarGridSpec(
            num_scalar_prefetch=1, grid=(t // tm,),
            in_specs=[pl.BlockSpec((tm, x1g.shape[1]), lambda i, dest: (i, 0)),
                      pl.BlockSpec(memory_space=pl.ANY)],
            out_specs=pl.BlockSpec(memory_space=pl.ANY),
            scratch_shapes=[pltpu.VMEM((tm * X_ROW_TILES, LANES), F32), pltpu.SemaphoreType.DMA(())]),
        out_shape=jax.ShapeDtypeStruct(xs.shape, F32), input_output_aliases={2: 0},
        compiler_params=_cparams("arbitrary"), name="moe_dispatch")(dest, x1g, xs)


def _expert(x, wg_ref, wu_ref, wd_ref):
    hg = jnp.dot(x, wg_ref[0, 0], preferred_element_type=F32)
    hu = jnp.dot(x, wu_ref[0, 0], preferred_element_type=F32)
    return jnp.dot((_silu(hg) * hu).astype(BF16), wd_ref[0, 0], preferred_element_type=F32)


def _experts_kernel(lo_ref, hi_ref, nused_ref, xs_ref, wg0, wu0, wd0, wg1, wu1, wd1, ys_ref):
    del lo_ref, hi_ref
    tm = xs_ref.shape[0] // X_ROW_TILES

    @pl.when(pl.program_id(0) < nused_ref[0])
    def _():
        x = _tiles_to_rows(xs_ref, tm, X_ROW_TILES).astype(BF16)
        _rows_to_tiles(ys_ref, _expert(x, wg0, wu0, wd0), Y_ROW_TILES)
        _rows_to_tiles(ys_ref, _expert(x, wg1, wu1, wd1), Y_ROW_TILES, first=X_ROW_TILES)

    @pl.when(pl.program_id(0) >= nused_ref[0])
    def _():
        ys_ref[...] = jnp.zeros(ys_ref.shape, F32)


def _experts(layer, tile_lo, tile_hi, n_used, xs, wg, wu, wd, tm):
    d = D_MODEL
    n_tiles = xs.shape[0] // (tm * X_ROW_TILES)
    used = lambda j, n: jnp.maximum(jnp.minimum(j, n[0] - 1), 0)
    w_in = lambda which: pl.BlockSpec(
        (1, 1, d, D_EXPERT), lambda j, lo, hi, n: (layer, (lo, hi)[which][used(j, n)], 0, 0))
    w_out = lambda which: pl.BlockSpec(
        (1, 1, D_EXPERT, d), lambda j, lo, hi, n: (layer, (lo, hi)[which][used(j, n)], 0, 0))
    return pl.pallas_call(
        _experts_kernel,
        grid_spec=pltpu.PrefetchScalarGridSpec(
            num_scalar_prefetch=3, grid=(n_tiles,),
            in_specs=[pl.BlockSpec((tm * X_ROW_TILES, LANES), lambda j, lo, hi, n: (used(j, n), 0)),
                      w_in(0), w_in(0), w_out(0), w_in(1), w_in(1), w_out(1)],
            out_specs=pl.BlockSpec((tm * Y_ROW_TILES, LANES), lambda j, lo, hi, n: (j, 0))),
        out_shape=jax.ShapeDtypeStruct((n_tiles * tm * Y_ROW_TILES, LANES), F32),
        compiler_params=_cparams("arbitrary"), name="moe_experts")(
            tile_lo, tile_hi, n_used, xs, wg, wu, wd, wg, wu, wd)


def _combine_kernel(dest_ref, x1g_ref, p_ref, ys_ref, g_ref, b_ref, wpg_ref, wpp_ref, o_ref, ybuf, sems):
    i = pl.program_id(0)
    tm = x1g_ref.shape[0]
    slot = i % 2

    def row_copy(tile, s, r):
        src = pl.multiple_of(dest_ref[tile * tm + r] * Y_ROW_TILES, Y_ROW_TILES)
        dst = pl.multiple_of(r * Y_ROW_TILES, Y_ROW_TILES)
        return pltpu.make_async_copy(ys_ref.at[pl.ds(src, Y_ROW_TILES)],
                                     ybuf.at[s].at[pl.ds(dst, Y_ROW_TILES)], sems.at[s])

    def start_tile(tile, s):
        def issue(r, carry):
            row_copy(tile, s, r).start()
            return carry
        lax.fori_loop(0, tm, issue, 0, unroll=DMA_ISSUE_UNROLL)

    @pl.when(i == 0)
    def _():
        start_tile(0, 0)

    @pl.when(i + 1 < pl.num_programs(0))
    def _():
        start_tile(i + 1, 1 - slot)

    pltpu.make_async_copy(ys_ref.at[pl.ds(0, tm * Y_ROW_TILES)], ybuf.at[slot], sems.at[slot]).wait()
    g = x1g_ref[:, D_MODEL:]
    y = (g[:, 0:1] * _tiles_to_rows(ybuf.at[slot], tm, Y_ROW_TILES)
         + g[:, 1:2] * _tiles_to_rows(ybuf.at[slot], tm, Y_ROW_TILES, first=X_ROW_TILES))
    x2 = _layer_norm(DEEPNORM_ALPHA * x1g_ref[:, 0:D_MODEL] + y, g_ref[...], b_ref[...])
    gate = _sigmoid(jnp.dot(x2.astype(BF16), wpg_ref[...], preferred_element_type=F32))
    o_ref[...] = x2 + gate * jnp.dot(p_ref[...].astype(BF16), wpp_ref[...], preferred_element_type=F32)


def _combine_ln_ple(dest, x1g, p, ys, ln_g, ln_b, w_gate, w_proj, tm=256):
    t = x1g.shape[0]
    d = D_MODEL
    const = lambda shape: pl.BlockSpec(shape, lambda i, dest: (0,) * len(shape))
    return pl.pallas_call(
        _combine_kernel,
        grid_spec=pltpu.PrefetchScalarGridSpec(
            num_scalar_prefetch=1, grid=(t // tm,),
            in_specs=[pl.BlockSpec((tm, x1g.shape[1]), lambda i, dest: (i, 0)),
                      pl.BlockSpec((tm, PLE_DIM), lambda i, dest: (i, 0)),
                      pl.BlockSpec(memory_space=pl.ANY), const((1, d)), const((1, d)),
                      const((d, d)), const((PLE_DIM, d))],
            out_specs=pl.BlockSpec((tm, d), lambda i, dest: (i, 0)),
            scratch_shapes=[pltpu.VMEM((2, tm * Y_ROW_TILES, LANES), F32), pltpu.SemaphoreType.DMA((2,))]),
        out_shape=jax.ShapeDtypeStruct((t, d), F32),
        compiler_params=_cparams("arbitrary"), name="moe_combine_ln_ple")(
            dest, x1g, p, ys, ln_g.astype(F32).reshape(1, d), ln_b.astype(F32).reshape(1, d), w_gate, w_proj)


def _pad_cols(w, n):
    return jnp.zeros((w.shape[0], n), w.dtype).at[:, 0:w.shape[1]].set(w)


def kernel(x, p, ln1_g, ln1_b, ln2_g, ln2_b, mlstm_w_in, mlstm_conv, mlstm_gate_b, mlstm_norm_g, mlstm_w_out, gla_w_in, gla_w_a2, gla_b_a, gla_norm_g, gla_w_out, hgrn_w_in, hgrn_lb, hgrn_norm_g, hgrn_w_out, router_w, router_b, exp_w_gate, exp_w_up, exp_w_down, ple_w_proj, ple_w_gate):
    bsz, seq, d = x.shape
    assert bsz == 1 and d == D_MODEL and seq % 1024 == 0
    xt = x.reshape(seq, d).astype(F32)
    n_tiles = seq // MOE_TILE + N_BUCKETS
    xs = jnp.zeros((n_tiles * MOE_TILE * X_ROW_TILES, LANES), F32)
    wg_all, wu_all, wd_all = exp_w_gate.astype(BF16), exp_w_up.astype(BF16), exp_w_down.astype(BF16)
    for i in range(DEPTH):
        kind, slot = i % N_MIXERS, i // N_MIXERS
        if kind == 0:
            w_in = mlstm_w_in[slot]
            w = _pad_cols(w_in.astype(BF16), A_PROJ_PAD)
            wgt = jnp.zeros((16, d), BF16).at[0:2 * A_HEADS].set(w_in[:, A_MAIN:].T.astype(BF16))
            z, gt = _inproj(xt, w, wgt)
            y = _mlstm_mix(z, gt, mlstm_conv[slot], mlstm_gate_b[slot], mlstm_norm_g[slot])
            w_out = mlstm_w_out[slot]
        elif kind == 1:
            z = _inproj(xt, _pad_cols(gla_w_in[slot].astype(BF16), B_PROJ_PAD))
            y = _gla_mix(z, gla_w_a2[slot], gla_b_a[slot], gla_norm_g[slot])
            w_out = gla_w_out[slot]
        else:
            z = _inproj(xt, hgrn_w_in[slot].astype(BF16))
            y = _hgrn_mix(z, hgrn_lb, i, hgrn_norm_g[slot])
            w_out = hgrn_w_out[slot]
        x1g, meta, cnt = _outproj_ln_route(y, xt, w_out.astype(BF16), ln1_g[i], ln1_b[i], router_w, router_b)
        dest, tile_lo, tile_hi, n_used = _moe_plan(meta, cnt, MOE_TILE, n_tiles)
        xs = _dispatch(dest, x1g, xs)
        ys = _experts(i, tile_lo, tile_hi, n_used, xs, wg_all, wu_all, wd_all, MOE_TILE)
        xt = _combine_ln_ple(dest, x1g, p[i].reshape(seq, PLE_DIM), ys, ln2_g[i], ln2_b[i],
                             ple_w_gate[i].astype(BF16), ple_w_proj[i].astype(BF16))
    return xt.reshape(bsz, seq, d)
```

```python
import functools

import jax
import jax.numpy as jnp
from jax import lax
from jax.experimental import pallas as pl
from jax.experimental.pallas import tpu as pltpu

F32 = jnp.float32
BF16 = jnp.bfloat16

D_MODEL = 1024
DEPTH = 4
CHUNK = 64
PLE_DIM = 256
N_MIXERS = 3
LN_EPS = 1e-5
DEEPNORM_ALPHA = (2 * DEPTH) ** 0.25
A_HEADS, A_DK, A_DV = 4, 128, 256
B_HEADS, B_DK, B_DV = 4, 128, 256
B_RANK = 16
B_TAU = 16.0
C_HEADS, C_DK, C_DV = 8, 128, 128
N_EXPERTS = 16
N_GROUPS = 4
EXPERTS_PER_GROUP = 4
D_EXPERT = 512
PAIRS_PER_GROUP = EXPERTS_PER_GROUP * (EXPERTS_PER_GROUP - 1) // 2
N_BUCKETS = N_GROUPS * PAIRS_PER_GROUP
BUCKET_ROWS = 32
MOE_TILE = 256
DMA_ISSUE_UNROLL = 64

LANES = 128
SLOT = 8
X_ROW_TILES = D_MODEL // LANES
Y_ROW_TILES = 2 * X_ROW_TILES
N_SLOTS = CHUNK // SLOT
A_MAIN = 2 * A_HEADS * A_DK + 2 * A_HEADS * A_DV
A_PROJ_PAD = A_MAIN + LANES
B_MAIN = 2 * B_HEADS * B_DK + 2 * B_HEADS * B_DV
B_PROJ_PAD = B_MAIN + LANES
C_PROJ = 2 * C_HEADS * C_DK + 2 * C_HEADS * C_DV
VMEM_LIMIT = 56 * 1024 * 1024

NT_DIMS = (((1,), (1,)), ((), ()))
TN_DIMS = (((0,), (0,)), ((), ()))


def _cparams(*sem):
    return pltpu.CompilerParams(dimension_semantics=sem, vmem_limit_bytes=VMEM_LIMIT)


def _sigmoid(x):
    return 0.5 * jnp.tanh(0.5 * x) + 0.5


def _silu(x):
    return x * _sigmoid(x)


def _log_sigmoid(x):
    return jnp.minimum(x, 0.0) - jnp.log(1.0 + jnp.exp(-jnp.abs(x)))


def _layer_norm(u, g, b):
    mu = jnp.mean(u, axis=-1, keepdims=True)
    d = u - mu
    var = jnp.mean(d * d, axis=-1, keepdims=True)
    return d * lax.rsqrt(var + LN_EPS) * g + b


def _tri(n, lower):
    r = lax.broadcasted_iota(jnp.int32, (n, n), 0)
    c = lax.broadcasted_iota(jnp.int32, (n, n), 1)
    return (r >= c) if lower else (r <= c)


def _bdot(a, b):
    return jnp.dot(a.astype(BF16), b.astype(BF16), preferred_element_type=F32)


def _bdot_nt(a, b):
    return lax.dot_general(a.astype(BF16), b.astype(BF16), NT_DIMS, preferred_element_type=F32)


def _bdot_tn(a, b):
    return lax.dot_general(a.astype(BF16), b.astype(BF16), TN_DIMS, preferred_element_type=F32)


def _const_spec(shape):
    return pl.BlockSpec(shape, lambda i: (0,) * len(shape))


def _inproj_kernel(x_ref, w_ref, z_ref):
    z_ref[...] = jnp.dot(x_ref[...].astype(BF16), w_ref[...], preferred_element_type=F32)


def _inproj_gates_kernel(x_ref, w_ref, wgt_ref, z_ref, gt_ref):
    xb = x_ref[...].astype(BF16)
    z_ref[...] = jnp.dot(xb, w_ref[...], preferred_element_type=F32)
    for c in range(gt_ref.shape[0]):
        gt_ref[c] = lax.dot_general(wgt_ref[...], xb[c * CHUNK:(c + 1) * CHUNK], NT_DIMS,
                                    preferred_element_type=F32)


def _inproj(x, w, wgt=None, tm=512):
    t, d = x.shape
    n = w.shape[1]
    grid = (t // tm,)
    x_spec = pl.BlockSpec((tm, d), lambda i: (i, 0))
    z_spec = pl.BlockSpec((tm, n), lambda i: (i, 0))
    if wgt is None:
        return pl.pallas_call(
            _inproj_kernel, grid=grid, in_specs=[x_spec, _const_spec((d, n))], out_specs=z_spec,
            out_shape=jax.ShapeDtypeStruct((t, n), F32), compiler_params=_cparams("parallel"),
            name="inproj")(x, w)
    return pl.pallas_call(
        _inproj_gates_kernel, grid=grid,
        in_specs=[x_spec, _const_spec((d, n)), _const_spec(wgt.shape)],
        out_specs=[z_spec, pl.BlockSpec((tm // CHUNK, 16, CHUNK), lambda i: (i, 0, 0))],
        out_shape=[jax.ShapeDtypeStruct((t, n), F32),
                   jax.ShapeDtypeStruct((t // CHUNK, 16, CHUNK), F32)],
        compiler_params=_cparams("parallel"), name="inproj_gates")(x, w, wgt)


def _mlstm_kernel(z_ref, gt_ref, conv_ref, gbr_ref, gbc_ref, ng_ref, y_ref,
                  ext_ref, ct_ref, n_ref, m_ref):
    tt = z_ref.shape[0]
    nqk = A_HEADS * A_DK

    @pl.when(pl.program_id(0) == 0)
    def _():
        ext_ref[0:8, :] = jnp.zeros((8, 2 * nqk), F32)
        ct_ref[...] = jnp.zeros(ct_ref.shape, F32)
        n_ref[...] = jnp.zeros(n_ref.shape, F32)
        m_ref[...] = jnp.zeros(m_ref.shape, F32)

    qk_raw = z_ref[:, 0:2 * nqk]
    ext_ref[8:8 + tt, :] = qk_raw
    acc = conv_ref[3:4, :] * qk_raw
    for s in range(1, 4):
        acc = acc + conv_ref[3 - s:4 - s, :] * ext_ref[pl.ds(8 - s, tt), :]
    ext_ref[0:8, :] = qk_raw[tt - 8:tt, :]
    qkc = _silu(acc)

    tri_lo = _tri(CHUNK, True)
    tri_lo_f = tri_lo.astype(F32)
    tri_up_f = _tri(CHUNK, False).astype(F32)

    for c in range(tt // CHUNK):
        r0 = c * CHUNK
        gc = z_ref[r0:r0 + CHUNK, A_MAIN:A_MAIN + LANES] + gbr_ref[...]
        a_col = jnp.dot(tri_lo_f, _log_sigmoid(gc), preferred_element_type=F32,
                        precision=lax.Precision.HIGHEST)
        gr = gt_ref[c] + gbc_ref[:, 0:1]
        a_row = jnp.dot(_log_sigmoid(gr), tri_up_f, preferred_element_type=F32,
                        precision=lax.Precision.HIGHEST)
        for h in range(A_HEADS):
            ac = a_col[:, A_HEADS + h:A_HEADS + h + 1]
            lic = gc[:, h:h + 1]
            ar = a_row[A_HEADS + h:A_HEADS + h + 1, :]
            lir = gr[h:h + 1, :]
            m_st = m_ref[h:h + 1, 0:1]
            qh = qkc[r0:r0 + CHUNK, h * A_DK:(h + 1) * A_DK]
            kh = qkc[r0:r0 + CHUNK, nqk + h * A_DK:nqk + (h + 1) * A_DK] * (A_DK ** -0.5)
            vh = z_ref[r0:r0 + CHUNK, 2 * nqk + h * A_DV:2 * nqk + (h + 1) * A_DV]
            oh = z_ref[r0:r0 + CHUNK, 2 * nqk + A_HEADS * A_DV + h * A_DV:
                       2 * nqk + A_HEADS * A_DV + (h + 1) * A_DV]

            d_log = jnp.where(tri_lo, ac - ar + lir, -jnp.inf)
            d_max = jnp.max(d_log, axis=-1, keepdims=True)
            s = _bdot_nt(qh, kh) * jnp.exp(d_log - d_max)
            sv = _bdot(s, vh)
            s_sum = jnp.sum(s, axis=-1, keepdims=True)
            a_tot = ar[:, CHUNK - 1:CHUNK]
            g_max = jnp.max(a_tot - ar + lir, axis=-1, keepdims=True)
            wk = jnp.exp(a_tot - ac + lic - g_max) * kh
            c_inc = _bdot_tn(vh, wk)
            n_inc = jnp.sum(wk, axis=0, keepdims=True)

            e_log = ac + m_st
            m_row = jnp.maximum(e_log, d_max)
            w_inter = jnp.exp(e_log - m_row)
            w_intra = jnp.exp(d_max - m_row)
            num = w_inter * _bdot_nt(qh, ct_ref[h]) + w_intra * sv
            den = (w_inter * jnp.sum(qh * n_ref[h:h + 1, :], axis=-1, keepdims=True) + w_intra * s_sum)
            hout = num / jnp.maximum(jnp.abs(den), jnp.exp(-m_row))
            m_new = jnp.maximum(a_tot + m_st, g_max)
            decay = jnp.exp(a_tot + m_st - m_new)
            inc_scale = jnp.exp(g_max - m_new)
            ct_ref[h] = decay * ct_ref[h] + inc_scale * c_inc
            n_ref[h:h + 1, :] = decay * n_ref[h:h + 1, :] + inc_scale * n_inc
            m_ref[h:h + 1, :] = jnp.broadcast_to(m_new, (1, LANES))

            hn = hout * lax.rsqrt(jnp.mean(hout * hout, axis=-1, keepdims=True) + LN_EPS)
            y_ref[r0:r0 + CHUNK, h * A_DV:(h + 1) * A_DV] = (
                hn * ng_ref[:, h * A_DV:(h + 1) * A_DV] * _sigmoid(oh))


def _mlstm_mix(z, gt, conv_w, gate_b, norm_g, tt=256):
    t = z.shape[0]
    nqk2 = 2 * A_HEADS * A_DK
    conv_p = jnp.zeros((8, nqk2), F32).at[0:4].set(conv_w.astype(F32))
    gb = gate_b.astype(F32)
    gb_row = jnp.zeros((1, LANES), F32).at[0, 0:2 * A_HEADS].set(gb)
    gb_col = jnp.zeros((16, LANES), F32).at[0:2 * A_HEADS, :].set(jnp.broadcast_to(gb[:, None], (2 * A_HEADS, LANES)))
    ng = norm_g.astype(F32).reshape(1, A_HEADS * A_DV)
    return pl.pallas_call(
        _mlstm_kernel, grid=(t // tt,),
        in_specs=[pl.BlockSpec((tt, A_PROJ_PAD), lambda i: (i, 0)),
                  pl.BlockSpec((tt // CHUNK, 16, CHUNK), lambda i: (i, 0, 0)),
                  _const_spec((8, nqk2)), _const_spec((1, LANES)), _const_spec((16, LANES)),
                  _const_spec((1, A_HEADS * A_DV))],
        out_specs=pl.BlockSpec((tt, A_HEADS * A_DV), lambda i: (i, 0)),
        out_shape=jax.ShapeDtypeStruct((t, A_HEADS * A_DV), F32),
        scratch_shapes=[pltpu.VMEM((tt + 8, nqk2), F32),
                        pltpu.VMEM((A_HEADS, A_DV, A_DK), F32),
                        pltpu.VMEM((8, A_DK), F32),
                        pltpu.VMEM((8, LANES), F32)],
        compiler_params=_cparams("arbitrary"), name="mlstm_mix")(z, gt, conv_p, gb_row, gb_col, ng)


def _store_lane_tiles(ref, x):
    for t in range(ref.shape[0]):
        ref[t] = x[:, t * LANES:(t + 1) * LANES]


def _interleaved(ref):
    return jnp.concatenate(
        [jnp.concatenate([ref[t, pl.ds(c, N_SLOTS, stride=SLOT), :] for t in range(ref.shape[0])], axis=1)
         for c in range(SLOT)], axis=0)


def _pair_factors(q, k, b, ref_of_slot):
    zero = jnp.zeros((SLOT, q.shape[1]), F32)
    factors = []
    for half in (1, 2, 4):
        qs, ks = [], []
        for u in range(N_SLOTS):
            r = ref_of_slot((u // (2 * half)) * 2 * half + half - 1)
            rows = slice(u * SLOT, (u + 1) * SLOT)
            if u % (2 * half) >= half:
                qs.append(q[rows] * jnp.exp(b[rows] - r))
                ks.append(zero)
            else:
                qs.append(zero)
                ks.append(k[rows] * jnp.exp(r - b[rows]))
        factors.append((jnp.concatenate(qs, axis=0), jnp.concatenate(ks, axis=0)))
    return factors


def _pair_masks():
    r = lax.broadcasted_iota(jnp.int32, (CHUNK, CHUNK), 0)
    c = lax.broadcasted_iota(jnp.int32, (CHUNK, CHUNK), 1)
    sr, sc = r >> 3, c >> 3
    same_sublane = (r & (SLOT - 1)) == (c & (SLOT - 1))
    natural, interleaved = [], []
    for shift, half in ((1, 1), (2, 2), (3, 4)):
        m = jnp.logical_and((sr >> shift) == (sc >> shift),
                            jnp.logical_and((sr & half) != 0, (sc & half) == 0))
        natural.append(m)
        interleaved.append(jnp.logical_and(m, same_sublane))
    return r == c, natural, interleaved


def _decay_attention_chunk(q, k, v, la, st_ref, qkb_ref, vo_ref, heads, dv):
    dk = LANES
    tri = _tri(CHUNK, True).astype(BF16)
    la_1 = la.astype(BF16)
    rest = la - la_1.astype(F32)
    la_2 = rest.astype(BF16)
    la_3 = (rest - la_2.astype(F32)).astype(BF16)
    b = (jnp.dot(tri, la_1, preferred_element_type=F32)
         + (jnp.dot(tri, la_2, preferred_element_type=F32) + jnp.dot(tri, la_3, preferred_element_type=F32)))
    b_last = b[CHUNK - 1:CHUNK, :]
    qe = q * jnp.exp(b)
    kd = k * jnp.exp(b_last - b)

    _store_lane_tiles(qkb_ref.at[0], q)
    _store_lane_tiles(qkb_ref.at[1], k)
    _store_lane_tiles(qkb_ref.at[2], b)
    _store_lane_tiles(vo_ref.at[0], v)
    q_i, k_i, b_i = _interleaved(qkb_ref.at[0]), _interleaved(qkb_ref.at[1]), _interleaved(qkb_ref.at[2])
    v_i = _interleaved(vo_ref.at[0])

    eye, masks_n, masks_i = _pair_masks()
    fact_n = _pair_factors(q, k, b, lambda s: b[s * SLOT + SLOT - 1:(s + 1) * SLOT, :])
    fact_i = _pair_factors(q_i, k_i, b_i, lambda s: b_i[s * SLOT:(s + 1) * SLOT, :])

    outs = []
    for h in range(heads):
        kc = slice(h * dk, (h + 1) * dk)
        vc = slice(h * dv, (h + 1) * dv)
        att = jnp.where(eye, _bdot_nt(q[:, kc], k[:, kc]), 0.0)
        for (qf, kf), m in zip(fact_n, masks_n):
            att = att + jnp.where(m, _bdot_nt(qf[:, kc], kf[:, kc]), 0.0)
        att_i = jnp.zeros((CHUNK, CHUNK), F32)
        for (qf, kf), m in zip(fact_i, masks_i):
            att_i = att_i + jnp.where(m, _bdot_nt(qf[:, kc], kf[:, kc]), 0.0)
        outs.append(_bdot_nt(qe[:, kc], st_ref[h]) + _bdot(att, v[:, vc]))
        _store_lane_tiles(vo_ref.at[1, h * dv // LANES:(h + 1) * dv // LANES], _bdot(att_i, v_i[:, vc]))
        st_ref[h] = st_ref[h] * jnp.exp(b_last[:, kc]) + _bdot_tn(v[:, vc], kd[:, kc])
    o_i = _interleaved(vo_ref.at[1])
    return [o + o_i[:, h * dv:(h + 1) * dv] for h, o in enumerate(outs)]


def _head_norm_store(outs, y_ref, r0, ng_ref, gate_act, dv):
    for h, o in enumerate(outs):
        hn = o * lax.rsqrt(jnp.mean(o * o, axis=-1, keepdims=True) + LN_EPS)
        y_ref[r0:r0 + CHUNK, h * dv:(h + 1) * dv] = (
            hn * ng_ref[:, h * dv:(h + 1) * dv] * gate_act[:, h * dv:(h + 1) * dv])


def _gla_kernel(z_ref, wa2_ref, ba_ref, ng_ref, y_ref, st_ref, qkb_ref, vo_ref):
    @pl.when(pl.program_id(0) == 0)
    def _():
        st_ref[...] = jnp.zeros(st_ref.shape, F32)

    nk, nv = B_HEADS * B_DK, B_HEADS * B_DV
    for c in range(z_ref.shape[0] // CHUNK):
        r0 = c * CHUNK
        q = z_ref[r0:r0 + CHUNK, 0:nk] * (B_DK ** -0.5)
        k = z_ref[r0:r0 + CHUNK, nk:2 * nk]
        v = z_ref[r0:r0 + CHUNK, 2 * nk:2 * nk + nv]
        g = z_ref[r0:r0 + CHUNK, 2 * nk + nv:2 * nk + 2 * nv]
        a_lr = z_ref[r0:r0 + CHUNK, B_MAIN:B_MAIN + LANES]
        la = _log_sigmoid(jnp.dot(a_lr.astype(BF16), wa2_ref[...], preferred_element_type=F32)
                          + ba_ref[...]) / B_TAU
        outs = _decay_attention_chunk(q, k, v, la, st_ref, qkb_ref.at[c], vo_ref.at[c], B_HEADS, B_DV)
        _head_norm_store(outs, y_ref, r0, ng_ref, _silu(g), B_DV)


def _gla_mix(z, w_a2, b_a, norm_g, tt=128):
    t = z.shape[0]
    nk, nv = B_HEADS * B_DK, B_HEADS * B_DV
    wa2 = jnp.zeros((LANES, nk), BF16).at[0:B_RANK].set(w_a2.astype(BF16))
    return pl.pallas_call(
        _gla_kernel, grid=(t // tt,),
        in_specs=[pl.BlockSpec((tt, B_PROJ_PAD), lambda i: (i, 0)),
                  _const_spec((LANES, nk)), _const_spec((1, nk)), _const_spec((1, nv))],
        out_specs=pl.BlockSpec((tt, nv), lambda i: (i, 0)),
        out_shape=jax.ShapeDtypeStruct((t, nv), F32),
        scratch_shapes=[pltpu.VMEM((B_HEADS, B_DV, B_DK), F32),
                        pltpu.VMEM((tt // CHUNK, 3, nk // LANES, CHUNK, LANES), F32),
                        pltpu.VMEM((tt // CHUNK, 2, nv // LANES, CHUNK, LANES), F32)],
        compiler_params=_cparams("arbitrary"), name="gla_mix")(
            z, wa2, b_a.astype(F32).reshape(1, nk), norm_g.astype(F32).reshape(1, nv))


def _hgrn_kernel(layer, z_ref, lbp_ref, ng_ref, y_ref, st_ref, qkb_ref, vo_ref):
    @pl.when(pl.program_id(0) == 0)
    def _():
        st_ref[...] = jnp.zeros(st_ref.shape, F32)

    lbp = lbp_ref[0:DEPTH, :]
    e = jnp.exp(lbp - jnp.max(lbp, axis=0, keepdims=True))
    sm = e / jnp.sum(e, axis=0, keepdims=True)
    lb = jnp.zeros((1, sm.shape[1]), F32)
    for r in range(layer + 1):
        lb = lb + sm[r:r + 1, :]
    lb = lb - sm[0:1, :]

    nk, nv = C_HEADS * C_DK, C_HEADS * C_DV
    for c in range(z_ref.shape[0] // CHUNK):
        r0 = c * CHUNK
        q = _silu(z_ref[r0:r0 + CHUNK, 0:nk])
        fg = lb + (1.0 - lb) * _sigmoid(z_ref[r0:r0 + CHUNK, nk:2 * nk])
        v = z_ref[r0:r0 + CHUNK, 2 * nk:2 * nk + nv]
        g = z_ref[r0:r0 + CHUNK, 2 * nk + nv:2 * nk + 2 * nv]
        outs = _decay_attention_chunk(q, 1.0 - fg, v, jnp.log(fg), st_ref, qkb_ref.at[c], vo_ref.at[c],
                                      C_HEADS, C_DV)
        _head_norm_store(outs, y_ref, r0, ng_ref, _sigmoid(g), C_DV)


def _hgrn_mix(z, hgrn_lb, layer, norm_g, tt=128):
    t = z.shape[0]
    nk, nv = C_HEADS * C_DK, C_HEADS * C_DV
    lbp = jnp.zeros((8, nk), F32).at[0:DEPTH].set(hgrn_lb.astype(F32))
    return pl.pallas_call(
        functools.partial(_hgrn_kernel, layer), grid=(t // tt,),
        in_specs=[pl.BlockSpec((tt, C_PROJ), lambda i: (i, 0)),
                  _const_spec((8, nk)), _const_spec((1, nv))],
        out_specs=pl.BlockSpec((tt, nv), lambda i: (i, 0)),
        out_shape=jax.ShapeDtypeStruct((t, nv), F32),
        scratch_shapes=[pltpu.VMEM((C_HEADS, C_DV, C_DK), F32),
                        pltpu.VMEM((tt // CHUNK, 3, nk // LANES, CHUNK, LANES), F32),
                        pltpu.VMEM((tt // CHUNK, 2, nv // LANES, CHUNK, LANES), F32)],
        compiler_params=_cparams("arbitrary"), name="hgrn_mix")(
            z, lbp, norm_g.astype(F32).reshape(1, nv))


def _route(lt, rb_ref):
    rows = [lt[e:e + 1, :] for e in range(N_EXPERTS)]
    mx = rows[0]
    for r in rows[1:]:
        mx = jnp.maximum(mx, r)
    ex = [jnp.exp(r - mx) for r in rows]
    tot = ex[0]
    for r in ex[1:]:
        tot = tot + r
    scores = [r / tot for r in ex]
    sel = [scores[e] + rb_ref[e:e + 1, 0:1] for e in range(N_EXPERTS)]

    gscore = []
    for g in range(N_GROUPS):
        m = sel[g * EXPERTS_PER_GROUP:(g + 1) * EXPERTS_PER_GROUP]
        best = None
        for a in range(EXPERTS_PER_GROUP):
            for b2 in range(a + 1, EXPERTS_PER_GROUP):
                pair = m[a] + m[b2]
                best = pair if best is None else jnp.maximum(best, pair)
        gscore.append(best)
    g_best = gscore[0]
    g_idx = jnp.zeros_like(g_best, dtype=jnp.int32)
    for g in range(1, N_GROUPS):
        better = gscore[g] > g_best
        g_best = jnp.where(better, gscore[g], g_best)
        g_idx = jnp.where(better, g, g_idx)

    masked = [jnp.where(g_idx == (e // EXPERTS_PER_GROUP), sel[e], -jnp.inf) for e in range(N_EXPERTS)]

    def first_argmax(vals):
        best_v, best_i = vals[0], jnp.zeros_like(g_idx)
        for e in range(1, N_EXPERTS):
            better = vals[e] > best_v
            best_v = jnp.where(better, vals[e], best_v)
            best_i = jnp.where(better, e, best_i)
        return best_i

    i1 = first_argmax(masked)
    i2 = first_argmax([jnp.where(i1 == e, -jnp.inf, masked[e]) for e in range(N_EXPERTS)])
    w1 = jnp.zeros_like(mx)
    w2 = jnp.zeros_like(mx)
    for e in range(N_EXPERTS):
        w1 = jnp.where(i1 == e, scores[e], w1)
        w2 = jnp.where(i2 == e, scores[e], w2)
    wsum = w1 + w2
    return i1, i2, w1 / wsum, w2 / wsum


def _outproj_kernel(y_ref, x_ref, w_ref, g_ref, b_ref, rwh_ref, rwl_ref, rb_ref,
                    x1g_ref, meta_ref, cnt_ref, gt_scr, carry_ref):
    tm = x_ref.shape[0]

    @pl.when(pl.program_id(0) == 0)
    def _():
        carry_ref[...] = jnp.zeros(carry_ref.shape, F32)

    mix = jnp.dot(y_ref[...].astype(BF16), w_ref[...], preferred_element_type=F32)
    x1 = _layer_norm(DEEPNORM_ALPHA * x_ref[...] + mix, g_ref[...], b_ref[...])
    x1g_ref[:, 0:D_MODEL] = x1
    x_hi = x1.astype(BF16)
    x_lo = (x1 - x_hi.astype(F32)).astype(BF16)
    logits = (jnp.dot(x_hi, rwh_ref[...], preferred_element_type=F32)
              + (jnp.dot(x_lo, rwh_ref[...], preferred_element_type=F32)
                 + jnp.dot(x_hi, rwl_ref[...], preferred_element_type=F32)))
    i1, i2, w1, w2 = _route(logits.T, rb_ref)

    lo = jnp.minimum(i1, i2)
    hi = jnp.maximum(i1, i2)
    a = lo & (EXPERTS_PER_GROUP - 1)
    b2 = hi & (EXPERTS_PER_GROUP - 1)
    bucket = (lo >> 2) * PAIRS_PER_GROUP + ((a * (7 - a)) >> 1) + (b2 - a - 1)

    onehot = lax.broadcasted_iota(jnp.int32, (BUCKET_ROWS, tm), 0) == bucket
    r = lax.broadcasted_iota(jnp.int32, (tm, tm), 0)
    c = lax.broadcasted_iota(jnp.int32, (tm, tm), 1)
    before = jnp.dot(onehot.astype(BF16), (r < c).astype(BF16), preferred_element_type=F32)
    rank = jnp.sum(jnp.where(onehot, before + carry_ref[:, 0:1], 0.0), axis=0, keepdims=True)
    carry_ref[...] = carry_ref[...] + jnp.sum(onehot.astype(F32), axis=1, keepdims=True)
    cnt_ref[...] = carry_ref[...].astype(jnp.int32)

    meta_ref[...] = jnp.zeros(meta_ref.shape, jnp.int32)
    meta_ref[0:1, :] = bucket
    meta_ref[1:2, :] = rank.astype(jnp.int32)

    gt_scr[...] = jnp.zeros(gt_scr.shape, F32)
    gt_scr[0:1, :] = jnp.where(i1 == lo, w1, w2)
    gt_scr[1:2, :] = jnp.where(i1 == lo, w2, w1)
    x1g_ref[:, D_MODEL:] = gt_scr[...].T


def _outproj_ln_route(y, x, w_out, ln_g, ln_b, router_w, router_b, tm=512):
    t, d = x.shape
    rw = jnp.zeros((d, LANES), F32).at[:, 0:N_EXPERTS].set(router_w.astype(F32))
    rw_hi = rw.astype(BF16)
    rw_lo = (rw - rw_hi.astype(F32)).astype(BF16)
    rb = jnp.broadcast_to(router_b.astype(F32)[:, None], (N_EXPERTS, LANES))
    return pl.pallas_call(
        _outproj_kernel, grid=(t // tm,),
        in_specs=[pl.BlockSpec((tm, y.shape[1]), lambda i: (i, 0)), pl.BlockSpec((tm, d), lambda i: (i, 0)),
                  _const_spec(w_out.shape), _const_spec((1, d)), _const_spec((1, d)),
                  _const_spec((d, LANES)), _const_spec((d, LANES)), _const_spec((N_EXPERTS, LANES))],
        out_specs=[pl.BlockSpec((tm, d + LANES), lambda i: (i, 0)),
                   pl.BlockSpec((8, tm), lambda i: (0, i)), _const_spec((BUCKET_ROWS, LANES))],
        out_shape=[jax.ShapeDtypeStruct((t, d + LANES), F32),
                   jax.ShapeDtypeStruct((8, t), jnp.int32),
                   jax.ShapeDtypeStruct((BUCKET_ROWS, LANES), jnp.int32)],
        scratch_shapes=[pltpu.VMEM((LANES, tm), F32), pltpu.VMEM((BUCKET_ROWS, LANES), F32)],
        compiler_params=_cparams("arbitrary"), name="outproj_ln_route")(
            y, x, w_out, ln_g.astype(F32).reshape(1, d), ln_b.astype(F32).reshape(1, d), rw_hi, rw_lo, rb)


def _moe_plan(meta, cnt, tile_rows, n_tiles):
    counts = cnt[0:N_BUCKETS, 0]
    tiles = (counts + tile_rows - 1) // tile_rows
    tile_end = jnp.cumsum(tiles)
    offs = (tile_end - tiles) * tile_rows
    dest = (offs[meta[0]] + meta[1]).astype(jnp.int32)
    n_used = tile_end[-1]
    j = jnp.minimum(jnp.arange(n_tiles, dtype=jnp.int32), n_used - 1)
    bucket_of_tile = jnp.minimum(jnp.sum(tile_end[None, :] <= j[:, None], axis=1), N_BUCKETS - 1)
    pairs = [(a, b) for a in range(EXPERTS_PER_GROUP) for b in range(a + 1, EXPERTS_PER_GROUP)]
    lo_tab = jnp.array([g * EXPERTS_PER_GROUP + a for g in range(N_GROUPS) for a, _ in pairs], jnp.int32)
    hi_tab = jnp.array([g * EXPERTS_PER_GROUP + b for g in range(N_GROUPS) for _, b in pairs], jnp.int32)
    return dest, lo_tab[bucket_of_tile], hi_tab[bucket_of_tile], n_used.astype(jnp.int32).reshape(1)


def _rows_to_tiles(ref, x, row_tiles, first=0):
    n = x.shape[0]
    for c in range(x.shape[1] // LANES):
        ref[pl.ds(first + c, n, stride=row_tiles), :] = x[:, c * LANES:(c + 1) * LANES]


def _tiles_to_rows(ref, n, row_tiles, first=0, count=D_MODEL // LANES):
    return jnp.concatenate([ref[pl.ds(first + c, n, stride=row_tiles), :] for c in range(count)], axis=1)


def _dispatch_kernel(dest_ref, x1g_ref, xs_in_ref, xs_ref, comb_ref, sem):
    del xs_in_ref
    tm = x1g_ref.shape[0]
    base = pl.program_id(0) * tm
    _rows_to_tiles(comb_ref, x1g_ref[:, 0:D_MODEL], X_ROW_TILES)

    def row_copy(r):
        src = pl.multiple_of(r * X_ROW_TILES, X_ROW_TILES)
        dst = pl.multiple_of(dest_ref[base + r] * X_ROW_TILES, X_ROW_TILES)
        return pltpu.make_async_copy(comb_ref.at[pl.ds(src, X_ROW_TILES)], xs_ref.at[pl.ds(dst, X_ROW_TILES)], sem)

    def issue(grp, carry):
        for u in range(DMA_ISSUE_UNROLL):
            row_copy(grp * DMA_ISSUE_UNROLL + u).start(priority=u % 2)
        return carry

    lax.fori_loop(0, tm // DMA_ISSUE_UNROLL, issue, 0)
    pltpu.make_async_copy(comb_ref, xs_ref.at[pl.ds(0, tm * X_ROW_TILES)], sem).wait()


def _dispatch(dest, x1g, xs, tm=256):
    t = x1g.shape[0]
    return pl.pallas_call(
        _dispatch_kernel,
        grid_spec=pltpu.PrefetchScalarGridSpec(
            num_scalar_prefetch=1, grid=(t // tm,),
            in_specs=[pl.BlockSpec((tm, x1g.shape[1]), lambda i, dest: (i, 0)),
                      pl.BlockSpec(memory_space=pl.ANY)],
            out_specs=pl.BlockSpec(memory_space=pl.ANY),
            scratch_shapes=[pltpu.VMEM((tm * X_ROW_TILES, LANES), F32), pltpu.SemaphoreType.DMA(())]),
        out_shape=jax.ShapeDtypeStruct(xs.shape, F32), input_output_aliases={2: 0},
        compiler_params=_cparams("arbitrary"), name="moe_dispatch")(dest, x1g, xs)


def _expert(x, wg_ref, wu_ref, wd_ref):
    hg = jnp.dot(x, wg_ref[0, 0], preferred_element_type=F32)
    hu = jnp.dot(x, wu_ref[0, 0], preferred_element_type=F32)
    return jnp.dot((_silu(hg) * hu).astype(BF16), wd_ref[0, 0], preferred_element_type=F32)


def _experts_kernel(lo_ref, hi_ref, nused_ref, xs_ref, wg0, wu0, wd0, wg1, wu1, wd1, ys_ref):
    del lo_ref, hi_ref
    tm = xs_ref.shape[0] // X_ROW_TILES

    @pl.when(pl.program_id(0) < nused_ref[0])
    def _():
        x = _tiles_to_rows(xs_ref, tm, X_ROW_TILES).astype(BF16)
        _rows_to_tiles(ys_ref, _expert(x, wg0, wu0, wd0), Y_ROW_TILES)
        _rows_to_tiles(ys_ref, _expert(x, wg1, wu1, wd1), Y_ROW_TILES, first=X_ROW_TILES)

    @pl.when(pl.program_id(0) >= nused_ref[0])
    def _():
        ys_ref[...] = jnp.zeros(ys_ref.shape, F32)


def _experts(layer, tile_lo, tile_hi, n_used, xs, wg, wu, wd, tm):
    d = D_MODEL
    n_tiles = xs.shape[0] // (tm * X_ROW_TILES)
    used = lambda j, n: jnp.maximum(jnp.minimum(j, n[0] - 1), 0)
    w_in = lambda which: pl.BlockSpec(
        (1, 1, d, D_EXPERT), lambda j, lo, hi, n: (layer, (lo, hi)[which][used(j, n)], 0, 0))
    w_out = lambda which: pl.BlockSpec(
        (1, 1, D_EXPERT, d), lambda j, lo, hi, n: (layer, (lo, hi)[which][used(j, n)], 0, 0))
    return pl.pallas_call(
        _experts_kernel,
        grid_spec=pltpu.PrefetchScalarGridSpec(
            num_scalar_prefetch=3, grid=(n_tiles,),
            in_specs=[pl.BlockSpec((tm * X_ROW_TILES, LANES), lambda j, lo, hi, n: (used(j, n), 0)),
                      w_in(0), w_in(0), w_out(0), w_in(1), w_in(1), w_out(1)],
            out_specs=pl.BlockSpec((tm * Y_ROW_TILES, LANES), lambda j, lo, hi, n: (j, 0))),
        out_shape=jax.ShapeDtypeStruct((n_tiles * tm * Y_ROW_TILES, LANES), F32),
        compiler_params=_cparams("arbitrary"), name="moe_experts")(
            tile_lo, tile_hi, n_used, xs, wg, wu, wd, wg, wu, wd)


def _combine_kernel(dest_ref, x1g_ref, p_ref, ys_ref, g_ref, b_ref, wpg_ref, wpp_ref, o_ref, ybuf, sems):
    i = pl.program_id(0)
    tm = x1g_ref.shape[0]
    slot = i % 2

    def row_copy(tile, s, r):
        src = pl.multiple_of(dest_ref[tile * tm + r] * Y_ROW_TILES, Y_ROW_TILES)
        dst = pl.multiple_of(r * Y_ROW_TILES, Y_ROW_TILES)
        return pltpu.make_async_copy(ys_ref.at[pl.ds(src, Y_ROW_TILES)],
                                     ybuf.at[s].at[pl.ds(dst, Y_ROW_TILES)], sems.at[s])

    def start_tile(tile, s):
        def issue(grp, carry):
            for u in range(DMA_ISSUE_UNROLL):
                row_copy(tile, s, grp * DMA_ISSUE_UNROLL + u).start(priority=u % 2)
            return carry
        lax.fori_loop(0, tm // DMA_ISSUE_UNROLL, issue, 0)

    @pl.when(i == 0)
    def _():
        start_tile(0, 0)

    @pl.when(i + 1 < pl.num_programs(0))
    def _():
        start_tile(i + 1, 1 - slot)

    pltpu.make_async_copy(ys_ref.at[pl.ds(0, tm * Y_ROW_TILES)], ybuf.at[slot], sems.at[slot]).wait()
    g = x1g_ref[:, D_MODEL:]
    y = (g[:, 0:1] * _tiles_to_rows(ybuf.at[slot], tm, Y_ROW_TILES)
         + g[:, 1:2] * _tiles_to_rows(ybuf.at[slot], tm, Y_ROW_TILES, first=X_ROW_TILES))
    x2 = _layer_norm(DEEPNORM_ALPHA * x1g_ref[:, 0:D_MODEL] + y, g_ref[...], b_ref[...])
    gate = _sigmoid(jnp.dot(x2.astype(BF16), wpg_ref[...], preferred_element_type=F32))
    o_ref[...] = x2 + gate * jnp.dot(p_ref[...].astype(BF16), wpp_ref[...], preferred_element_type=F32)


def _combine_ln_ple(dest, x1g, p, ys, ln_g, ln_b, w_gate, w_proj, tm=256):
    t = x1g.shape[0]
    d = D_MODEL
    const = lambda shape: pl.BlockSpec(shape, lambda i, dest: (0,) * len(shape))
    return pl.pallas_call(
        _combine_kernel,
        grid_spec=pltpu.PrefetchScalarGridSpec(
            num_scalar_prefetch=1, grid=(t // tm,),
            in_specs=[pl.BlockSpec((tm, x1g.shape[1]), lambda i, dest: (i, 0)),
                      pl.BlockSpec((tm, PLE_DIM), lambda i, dest: (i, 0)),
                      pl.BlockSpec(memory_space=pl.ANY), const((1, d)), const((1, d)),
                      const((d, d)), const((PLE_DIM, d))],
            out_specs=pl.BlockSpec((tm, d), lambda i, dest: (i, 0)),
            scratch_shapes=[pltpu.VMEM((2, tm * Y_ROW_TILES, LANES), F32), pltpu.SemaphoreType.DMA((2,))]),
        out_shape=jax.ShapeDtypeStruct((t, d), F32),
        compiler_params=_cparams("arbitrary"), name="moe_combine_ln_ple")(
            dest, x1g, p, ys, ln_g.astype(F32).reshape(1, d), ln_b.astype(F32).reshape(1, d), w_gate, w_proj)


def _pad_cols(w, n):
    return jnp.zeros((w.shape[0], n), w.dtype).at[:, 0:w.shape[1]].set(w)


def kernel(x, p, ln1_g, ln1_b, ln2_g, ln2_b, mlstm_w_in, mlstm_conv, mlstm_gate_b, mlstm_norm_g, mlstm_w_out, gla_w_in, gla_w_a2, gla_b_a, gla_norm_g, gla_w_out, hgrn_w_in, hgrn_lb, hgrn_norm_g, hgrn_w_out, router_w, router_b, exp_w_gate, exp_w_up, exp_w_down, ple_w_proj, ple_w_gate):
    bsz, seq, d = x.shape
    assert bsz == 1 and d == D_MODEL and seq % 1024 == 0
    xt = x.reshape(seq, d).astype(F32)
    n_tiles = seq // MOE_TILE + N_BUCKETS
    xs = jnp.zeros((n_tiles * MOE_TILE * X_ROW_TILES, LANES), F32)
    wg_all, wu_all, wd_all = exp_w_gate.astype(BF16), exp_w_up.astype(BF16), exp_w_down.astype(BF16)
    for i in range(DEPTH):
        kind, slot = i % N_MIXERS, i // N_MIXERS
        if kind == 0:
            w_in = mlstm_w_in[slot]
            w = _pad_cols(w_in.astype(BF16), A_PROJ_PAD)
            wgt = jnp.zeros((16, d), BF16).at[0:2 * A_HEADS].set(w_in[:, A_MAIN:].T.astype(BF16))
            z, gt = _inproj(xt, w, wgt)
            y = _mlstm_mix(z, gt, mlstm_conv[slot], mlstm_gate_b[slot], mlstm_norm_g[slot])
            w_out = mlstm_w_out[slot]
        elif kind == 1:
            z = _inproj(xt, _pad_cols(gla_w_in[slot].astype(BF16), B_PROJ_PAD))
            y = _gla_mix(z, gla_w_a2[slot], gla_b_a[slot], gla_norm_g[slot])
            w_out = gla_w_out[slot]
        else:
            z = _inproj(xt, hgrn_w_in[slot].astype(BF16))
            y = _hgrn_mix(z, hgrn_lb, i, hgrn_norm_g[slot])
            w_out = hgrn_w_out[slot]
        x1g, meta, cnt = _outproj_ln_route(y, xt, w_out.astype(BF16), ln1_g[i], ln1_b[i], router_w, router_b)
        dest, tile_lo, tile_hi, n_used = _moe_plan(meta, cnt, MOE_TILE, n_tiles)
        xs = _dispatch(dest, x1g, xs)
        ys = _experts(i, tile_lo, tile_hi, n_used, xs, wg_all, wu_all, wd_all, MOE_TILE)
        xt = _combine_ln_ple(dest, x1g, p[i].reshape(seq, PLE_DIM), ys, ln2_g[i], ln2_b[i],
                             ple_w_gate[i].astype(BF16), ple_w_proj[i].astype(BF16))
    return xt.reshape(bsz, seq, d)
```

```python
import functools

import jax
import jax.numpy as jnp
from jax import lax
from jax.experimental import pallas as pl
from jax.experimental.pallas import tpu as pltpu

F32 = jnp.float32
BF16 = jnp.bfloat16

D_MODEL = 1024
DEPTH = 4
CHUNK = 64
PLE_DIM = 256
N_MIXERS = 3
LN_EPS = 1e-5
DEEPNORM_ALPHA = (2 * DEPTH) ** 0.25
A_HEADS, A_DK, A_DV = 4, 128, 256
B_HEADS, B_DK, B_DV = 4, 128, 256
B_RANK = 16
B_TAU = 16.0
C_HEADS, C_DK, C_DV = 8, 128, 128
N_EXPERTS = 16
N_GROUPS = 4
EXPERTS_PER_GROUP = 4
D_EXPERT = 512
PAIRS_PER_GROUP = EXPERTS_PER_GROUP * (EXPERTS_PER_GROUP - 1) // 2
N_BUCKETS = N_GROUPS * PAIRS_PER_GROUP
BUCKET_ROWS = 32
MOE_TILE = 256
DMA_ISSUE_UNROLL = 64

LANES = 128
SLOT = 8
X_ROW_TILES = D_MODEL // LANES
N_SLOTS = CHUNK // SLOT
A_MAIN = 2 * A_HEADS * A_DK + 2 * A_HEADS * A_DV
A_PROJ_PAD = A_MAIN + LANES
B_MAIN = 2 * B_HEADS * B_DK + 2 * B_HEADS * B_DV
B_PROJ_PAD = B_MAIN + LANES
C_PROJ = 2 * C_HEADS * C_DK + 2 * C_HEADS * C_DV
VMEM_LIMIT = 56 * 1024 * 1024

NT_DIMS = (((1,), (1,)), ((), ()))
TN_DIMS = (((0,), (0,)), ((), ()))


def _cparams(*sem):
    return pltpu.CompilerParams(dimension_semantics=sem, vmem_limit_bytes=VMEM_LIMIT)


def _sigmoid(x):
    return 0.5 * jnp.tanh(0.5 * x) + 0.5


def _silu(x):
    return x * _sigmoid(x)


def _log_sigmoid(x):
    return jnp.minimum(x, 0.0) - jnp.log(1.0 + jnp.exp(-jnp.abs(x)))


def _layer_norm(u, g, b):
    mu = jnp.mean(u, axis=-1, keepdims=True)
    d = u - mu
    var = jnp.mean(d * d, axis=-1, keepdims=True)
    return d * lax.rsqrt(var + LN_EPS) * g + b


def _tri(n, lower):
    r = lax.broadcasted_iota(jnp.int32, (n, n), 0)
    c = lax.broadcasted_iota(jnp.int32, (n, n), 1)
    return (r >= c) if lower else (r <= c)


def _bdot(a, b):
    return jnp.dot(a.astype(BF16), b.astype(BF16), preferred_element_type=F32)


def _bdot_nt(a, b):
    return lax.dot_general(a.astype(BF16), b.astype(BF16), NT_DIMS, preferred_element_type=F32)


def _bdot_tn(a, b):
    return lax.dot_general(a.astype(BF16), b.astype(BF16), TN_DIMS, preferred_element_type=F32)


def _const_spec(shape):
    return pl.BlockSpec(shape, lambda i: (0,) * len(shape))


def _inproj_kernel(x_ref, w_ref, z_ref):
    z_ref[...] = jnp.dot(x_ref[...].astype(BF16), w_ref[...], preferred_element_type=F32)


def _inproj_gates_kernel(x_ref, w_ref, wgt_ref, z_ref, gt_ref):
    xb = x_ref[...].astype(BF16)
    z_ref[...] = jnp.dot(xb, w_ref[...], preferred_element_type=F32)
    for c in range(gt_ref.shape[0]):
        gt_ref[c] = lax.dot_general(wgt_ref[...], xb[c * CHUNK:(c + 1) * CHUNK], NT_DIMS,
                                    preferred_element_type=F32)


def _inproj(x, w, wgt=None, tm=512):
    t, d = x.shape
    n = w.shape[1]
    grid = (t // tm,)
    x_spec = pl.BlockSpec((tm, d), lambda i: (i, 0))
    z_spec = pl.BlockSpec((tm, n), lambda i: (i, 0))
    if wgt is None:
        return pl.pallas_call(
            _inproj_kernel, grid=grid, in_specs=[x_spec, _const_spec((d, n))], out_specs=z_spec,
            out_shape=jax.ShapeDtypeStruct((t, n), F32), compiler_params=_cparams("parallel"),
            name="inproj")(x, w)
    return pl.pallas_call(
        _inproj_gates_kernel, grid=grid,
        in_specs=[x_spec, _const_spec((d, n)), _const_spec(wgt.shape)],
        out_specs=[z_spec, pl.BlockSpec((tm // CHUNK, 16, CHUNK), lambda i: (i, 0, 0))],
        out_shape=[jax.ShapeDtypeStruct((t, n), F32),
                   jax.ShapeDtypeStruct((t // CHUNK, 16, CHUNK), F32)],
        compiler_params=_cparams("parallel"), name="inproj_gates")(x, w, wgt)


def _mlstm_kernel(z_ref, gt_ref, conv_ref, gbr_ref, gbc_ref, ng_ref, y_ref,
                  ext_ref, ct_ref, n_ref, m_ref):
    tt = z_ref.shape[0]
    nqk = A_HEADS * A_DK

    @pl.when(pl.program_id(0) == 0)
    def _():
        ext_ref[0:8, :] = jnp.zeros((8, 2 * nqk), F32)
        ct_ref[...] = jnp.zeros(ct_ref.shape, F32)
        n_ref[...] = jnp.zeros(n_ref.shape, F32)
        m_ref[...] = jnp.zeros(m_ref.shape, F32)

    qk_raw = z_ref[:, 0:2 * nqk]
    ext_ref[8:8 + tt, :] = qk_raw
    acc = conv_ref[3:4, :] * qk_raw
    for s in range(1, 4):
        acc = acc + conv_ref[3 - s:4 - s, :] * ext_ref[pl.ds(8 - s, tt), :]
    ext_ref[0:8, :] = qk_raw[tt - 8:tt, :]
    qkc = _silu(acc)

    tri_lo = _tri(CHUNK, True)
    tri_lo_f = tri_lo.astype(F32)
    tri_up_f = _tri(CHUNK, False).astype(F32)

    for c in range(tt // CHUNK):
        r0 = c * CHUNK
        gc = z_ref[r0:r0 + CHUNK, A_MAIN:A_MAIN + LANES] + gbr_ref[...]
        a_col = jnp.dot(tri_lo_f, _log_sigmoid(gc), preferred_element_type=F32,
                        precision=lax.Precision.HIGHEST)
        gr = gt_ref[c] + gbc_ref[:, 0:1]
        a_row = jnp.dot(_log_sigmoid(gr), tri_up_f, preferred_element_type=F32,
                        precision=lax.Precision.HIGHEST)
        for h in range(A_HEADS):
            ac = a_col[:, A_HEADS + h:A_HEADS + h + 1]
            lic = gc[:, h:h + 1]
            ar = a_row[A_HEADS + h:A_HEADS + h + 1, :]
            lir = gr[h:h + 1, :]
            m_st = m_ref[h:h + 1, 0:1]
            qh = qkc[r0:r0 + CHUNK, h * A_DK:(h + 1) * A_DK]
            kh = qkc[r0:r0 + CHUNK, nqk + h * A_DK:nqk + (h + 1) * A_DK] * (A_DK ** -0.5)
            vh = z_ref[r0:r0 + CHUNK, 2 * nqk + h * A_DV:2 * nqk + (h + 1) * A_DV]
            oh = z_ref[r0:r0 + CHUNK, 2 * nqk + A_HEADS * A_DV + h * A_DV:
                       2 * nqk + A_HEADS * A_DV + (h + 1) * A_DV]

            d_log = jnp.where(tri_lo, ac - ar + lir, -jnp.inf)
            d_max = jnp.max(d_log, axis=-1, keepdims=True)
            s = _bdot_nt(qh, kh) * jnp.exp(d_log - d_max)
            sv = _bdot(s, vh)
            s_sum = jnp.sum(s, axis=-1, keepdims=True)
            a_tot = ar[:, CHUNK - 1:CHUNK]
            g_max = jnp.max(a_tot - ar + lir, axis=-1, keepdims=True)
            wk = jnp.exp(a_tot - ac + lic - g_max) * kh
            c_inc = _bdot_tn(vh, wk)
            n_inc = jnp.sum(wk, axis=0, keepdims=True)

            e_log = ac + m_st
            m_row = jnp.maximum(e_log, d_max)
            w_inter = jnp.exp(e_log - m_row)
            w_intra = jnp.exp(d_max - m_row)
            num = w_inter * _bdot_nt(qh, ct_ref[h]) + w_intra * sv
            den = (w_inter * jnp.sum(qh * n_ref[h:h + 1, :], axis=-1, keepdims=True) + w_intra * s_sum)
            hout = num / jnp.maximum(jnp.abs(den), jnp.exp(-m_row))
            m_new = jnp.maximum(a_tot + m_st, g_max)
            decay = jnp.exp(a_tot + m_st - m_new)
            inc_scale = jnp.exp(g_max - m_new)
            ct_ref[h] = decay * ct_ref[h] + inc_scale * c_inc
            n_ref[h:h + 1, :] = decay * n_ref[h:h + 1, :] + inc_scale * n_inc
            m_ref[h:h + 1, :] = jnp.broadcast_to(m_new, (1, LANES))

            hn = hout * lax.rsqrt(jnp.mean(hout * hout, axis=-1, keepdims=True) + LN_EPS)
            y_ref[r0:r0 + CHUNK, h * A_DV:(h + 1) * A_DV] = (
                hn * ng_ref[:, h * A_DV:(h + 1) * A_DV] * _sigmoid(oh))


def _mlstm_mix(z, gt, conv_w, gate_b, norm_g, tt=256):
    t = z.shape[0]
    nqk2 = 2 * A_HEADS * A_DK
    conv_p = jnp.zeros((8, nqk2), F32).at[0:4].set(conv_w.astype(F32))
    gb = gate_b.astype(F32)
    gb_row = jnp.zeros((1, LANES), F32).at[0, 0:2 * A_HEADS].set(gb)
    gb_col = jnp.zeros((16, LANES), F32).at[0:2 * A_HEADS, :].set(jnp.broadcast_to(gb[:, None], (2 * A_HEADS, LANES)))
    ng = norm_g.astype(F32).reshape(1, A_HEADS * A_DV)
    return pl.pallas_call(
        _mlstm_kernel, grid=(t // tt,),
        in_specs=[pl.BlockSpec((tt, A_PROJ_PAD), lambda i: (i, 0)),
                  pl.BlockSpec((tt // CHUNK, 16, CHUNK), lambda i: (i, 0, 0)),
                  _const_spec((8, nqk2)), _const_spec((1, LANES)), _const_spec((16, LANES)),
                  _const_spec((1, A_HEADS * A_DV))],
        out_specs=pl.BlockSpec((tt, A_HEADS * A_DV), lambda i: (i, 0)),
        out_shape=jax.ShapeDtypeStruct((t, A_HEADS * A_DV), F32),
        scratch_shapes=[pltpu.VMEM((tt + 8, nqk2), F32),
                        pltpu.VMEM((A_HEADS, A_DV, A_DK), F32),
                        pltpu.VMEM((8, A_DK), F32),
                        pltpu.VMEM((8, LANES), F32)],
        compiler_params=_cparams("arbitrary"), name="mlstm_mix")(z, gt, conv_p, gb_row, gb_col, ng)


def _store_lane_tiles(ref, x):
    for t in range(ref.shape[0]):
        ref[t] = x[:, t * LANES:(t + 1) * LANES]


def _interleaved(ref):
    return jnp.concatenate(
        [jnp.concatenate([ref[t, pl.ds(c, N_SLOTS, stride=SLOT), :] for t in range(ref.shape[0])], axis=1)
         for c in range(SLOT)], axis=0)


def _pair_factors(q, k, b, ref_of_slot):
    zero = jnp.zeros((SLOT, q.shape[1]), F32)
    factors = []
    for half in (1, 2, 4):
        qs, ks = [], []
        for u in range(N_SLOTS):
            r = ref_of_slot((u // (2 * half)) * 2 * half + half - 1)
            rows = slice(u * SLOT, (u + 1) * SLOT)
            if u % (2 * half) >= half:
                qs.append(q[rows] * jnp.exp(b[rows] - r))
                ks.append(zero)
            else:
                qs.append(zero)
                ks.append(k[rows] * jnp.exp(r - b[rows]))
        factors.append((jnp.concatenate(qs, axis=0), jnp.concatenate(ks, axis=0)))
    return factors


def _pair_masks():
    r = lax.broadcasted_iota(jnp.int32, (CHUNK, CHUNK), 0)
    c = lax.broadcasted_iota(jnp.int32, (CHUNK, CHUNK), 1)
    sr, sc = r >> 3, c >> 3
    same_sublane = (r & (SLOT - 1)) == (c & (SLOT - 1))
    natural, interleaved = [], []
    for shift, half in ((1, 1), (2, 2), (3, 4)):
        m = jnp.logical_and((sr >> shift) == (sc >> shift),
                            jnp.logical_and((sr & half) != 0, (sc & half) == 0))
        natural.append(m)
        interleaved.append(jnp.logical_and(m, same_sublane))
    return r == c, natural, interleaved


def _decay_attention_chunk(q, k, v, la, st_ref, qkb_ref, vo_ref, heads, dv):
    dk = LANES
    tri = _tri(CHUNK, True).astype(BF16)
    la_1 = la.astype(BF16)
    rest = la - la_1.astype(F32)
    la_2 = rest.astype(BF16)
    la_3 = (rest - la_2.astype(F32)).astype(BF16)
    b = (jnp.dot(tri, la_1, preferred_element_type=F32)
         + (jnp.dot(tri, la_2, preferred_element_type=F32) + jnp.dot(tri, la_3, preferred_element_type=F32)))
    b_last = b[CHUNK - 1:CHUNK, :]
    qe = q * jnp.exp(b)
    kd = k * jnp.exp(b_last - b)

    _store_lane_tiles(qkb_ref.at[0], q)
    _store_lane_tiles(qkb_ref.at[1], k)
    _store_lane_tiles(qkb_ref.at[2], b)
    _store_lane_tiles(vo_ref.at[0], v)
    q_i, k_i, b_i = _interleaved(qkb_ref.at[0]), _interleaved(qkb_ref.at[1]), _interleaved(qkb_ref.at[2])
    v_i = _interleaved(vo_ref.at[0])

    eye, masks_n, masks_i = _pair_masks()
    fact_n = _pair_factors(q, k, b, lambda s: b[s * SLOT + SLOT - 1:(s + 1) * SLOT, :])
    fact_i = _pair_factors(q_i, k_i, b_i, lambda s: b_i[s * SLOT:(s + 1) * SLOT, :])

    outs = []
    for h in range(heads):
        kc = slice(h * dk, (h + 1) * dk)
        vc = slice(h * dv, (h + 1) * dv)
        att = jnp.where(eye, _bdot_nt(q[:, kc], k[:, kc]), 0.0)
        for (qf, kf), m in zip(fact_n, masks_n):
            att = att + jnp.where(m, _bdot_nt(qf[:, kc], kf[:, kc]), 0.0)
        att_i = jnp.zeros((CHUNK, CHUNK), F32)
        for (qf, kf), m in zip(fact_i, masks_i):
            att_i = att_i + jnp.where(m, _bdot_nt(qf[:, kc], kf[:, kc]), 0.0)
        outs.append(_bdot_nt(qe[:, kc], st_ref[h]) + _bdot(att, v[:, vc]))
        _store_lane_tiles(vo_ref.at[1, h * dv // LANES:(h + 1) * dv // LANES], _bdot(att_i, v_i[:, vc]))
        st_ref[h] = st_ref[h] * jnp.exp(b_last[:, kc]) + _bdot_tn(v[:, vc], kd[:, kc])
    o_i = _interleaved(vo_ref.at[1])
    return [o + o_i[:, h * dv:(h + 1) * dv] for h, o in enumerate(outs)]


def _head_norm_store(outs, y_ref, r0, ng_ref, gate_act, dv):
    for h, o in enumerate(outs):
        hn = o * lax.rsqrt(jnp.mean(o * o, axis=-1, keepdims=True) + LN_EPS)
        y_ref[r0:r0 + CHUNK, h * dv:(h + 1) * dv] = (
            hn * ng_ref[:, h * dv:(h + 1) * dv] * gate_act[:, h * dv:(h + 1) * dv])


def _gla_kernel(z_ref, wa2_ref, ba_ref, ng_ref, y_ref, st_ref, qkb_ref, vo_ref):
    @pl.when(pl.program_id(0) == 0)
    def _():
        st_ref[...] = jnp.zeros(st_ref.shape, F32)

    nk, nv = B_HEADS * B_DK, B_HEADS * B_DV
    for c in range(z_ref.shape[0] // CHUNK):
        r0 = c * CHUNK
        q = z_ref[r0:r0 + CHUNK, 0:nk] * (B_DK ** -0.5)
        k = z_ref[r0:r0 + CHUNK, nk:2 * nk]
        v = z_ref[r0:r0 + CHUNK, 2 * nk:2 * nk + nv]
        g = z_ref[r0:r0 + CHUNK, 2 * nk + nv:2 * nk + 2 * nv]
        a_lr = z_ref[r0:r0 + CHUNK, B_MAIN:B_MAIN + LANES]
        la = _log_sigmoid(jnp.dot(a_lr.astype(BF16), wa2_ref[...], preferred_element_type=F32)
                          + ba_ref[...]) / B_TAU
        outs = _decay_attention_chunk(q, k, v, la, st_ref, qkb_ref.at[c], vo_ref.at[c], B_HEADS, B_DV)
        _head_norm_store(outs, y_ref, r0, ng_ref, _silu(g), B_DV)


def _gla_mix(z, w_a2, b_a, norm_g, tt=256):
    t = z.shape[0]
    nk, nv = B_HEADS * B_DK, B_HEADS * B_DV
    wa2 = jnp.zeros((LANES, nk), BF16).at[0:B_RANK].set(w_a2.astype(BF16))
    return pl.pallas_call(
        _gla_kernel, grid=(t // tt,),
        in_specs=[pl.BlockSpec((tt, B_PROJ_PAD), lambda i: (i, 0)),
                  _const_spec((LANES, nk)), _const_spec((1, nk)), _const_spec((1, nv))],
        out_specs=pl.BlockSpec((tt, nv), lambda i: (i, 0)),
        out_shape=jax.ShapeDtypeStruct((t, nv), F32),
        scratch_shapes=[pltpu.VMEM((B_HEADS, B_DV, B_DK), F32),
                        pltpu.VMEM((tt // CHUNK, 3, nk // LANES, CHUNK, LANES), F32),
                        pltpu.VMEM((tt // CHUNK, 2, nv // LANES, CHUNK, LANES), F32)],
        compiler_params=_cparams("arbitrary"), name="gla_mix")(
            z, wa2, b_a.astype(F32).reshape(1, nk), norm_g.astype(F32).reshape(1, nv))


def _hgrn_kernel(layer, z_ref, lbp_ref, ng_ref, y_ref, st_ref, qkb_ref, vo_ref):
    @pl.when(pl.program_id(0) == 0)
    def _():
        st_ref[...] = jnp.zeros(st_ref.shape, F32)

    lbp = lbp_ref[0:DEPTH, :]
    e = jnp.exp(lbp - jnp.max(lbp, axis=0, keepdims=True))
    sm = e / jnp.sum(e, axis=0, keepdims=True)
    lb = jnp.zeros((1, sm.shape[1]), F32)
    for r in range(layer + 1):
        lb = lb + sm[r:r + 1, :]
    lb = lb - sm[0:1, :]

    nk, nv = C_HEADS * C_DK, C_HEADS * C_DV
    for c in range(z_ref.shape[0] // CHUNK):
        r0 = c * CHUNK
        q = _silu(z_ref[r0:r0 + CHUNK, 0:nk])
        fg = lb + (1.0 - lb) * _sigmoid(z_ref[r0:r0 + CHUNK, nk:2 * nk])
        v = z_ref[r0:r0 + CHUNK, 2 * nk:2 * nk + nv]
        g = z_ref[r0:r0 + CHUNK, 2 * nk + nv:2 * nk + 2 * nv]
        outs = _decay_attention_chunk(q, 1.0 - fg, v, jnp.log(fg), st_ref, qkb_ref.at[c], vo_ref.at[c],
                                      C_HEADS, C_DV)
        _head_norm_store(outs, y_ref, r0, ng_ref, _sigmoid(g), C_DV)


def _hgrn_mix(z, hgrn_lb, layer, norm_g, tt=256):
    t = z.shape[0]
    nk, nv = C_HEADS * C_DK, C_HEADS * C_DV
    lbp = jnp.zeros((8, nk), F32).at[0:DEPTH].set(hgrn_lb.astype(F32))
    return pl.pallas_call(
        functools.partial(_hgrn_kernel, layer), grid=(t // tt,),
        in_specs=[pl.BlockSpec((tt, C_PROJ), lambda i: (i, 0)),
                  _const_spec((8, nk)), _const_spec((1, nv))],
        out_specs=pl.BlockSpec((tt, nv), lambda i: (i, 0)),
        out_shape=jax.ShapeDtypeStruct((t, nv), F32),
        scratch_shapes=[pltpu.VMEM((C_HEADS, C_DV, C_DK), F32),
                        pltpu.VMEM((tt // CHUNK, 3, nk // LANES, CHUNK, LANES), F32),
                        pltpu.VMEM((tt // CHUNK, 2, nv // LANES, CHUNK, LANES), F32)],
        compiler_params=_cparams("arbitrary"), name="hgrn_mix")(
            z, lbp, norm_g.astype(F32).reshape(1, nv))


def _route(lt, rb_ref):
    rows = [lt[e:e + 1, :] for e in range(N_EXPERTS)]
    mx = rows[0]
    for r in rows[1:]:
        mx = jnp.maximum(mx, r)
    ex = [jnp.exp(r - mx) for r in rows]
    tot = ex[0]
    for r in ex[1:]:
        tot = tot + r
    scores = [r / tot for r in ex]
    sel = [scores[e] + rb_ref[e:e + 1, 0:1] for e in range(N_EXPERTS)]

    gscore = []
    for g in range(N_GROUPS):
        m = sel[g * EXPERTS_PER_GROUP:(g + 1) * EXPERTS_PER_GROUP]
        best = None
        for a in range(EXPERTS_PER_GROUP):
            for b2 in range(a + 1, EXPERTS_PER_GROUP):
                pair = m[a] + m[b2]
                best = pair if best is None else jnp.maximum(best, pair)
        gscore.append(best)
    g_best = gscore[0]
    g_idx = jnp.zeros_like(g_best, dtype=jnp.int32)
    for g in range(1, N_GROUPS):
        better = gscore[g] > g_best
        g_best = jnp.where(better, gscore[g], g_best)
        g_idx = jnp.where(better, g, g_idx)

    masked = [jnp.where(g_idx == (e // EXPERTS_PER_GROUP), sel[e], -jnp.inf) for e in range(N_EXPERTS)]

    def first_argmax(vals):
        best_v, best_i = vals[0], jnp.zeros_like(g_idx)
        for e in range(1, N_EXPERTS):
            better = vals[e] > best_v
            best_v = jnp.where(better, vals[e], best_v)
            best_i = jnp.where(better, e, best_i)
        return best_i

    i1 = first_argmax(masked)
    i2 = first_argmax([jnp.where(i1 == e, -jnp.inf, masked[e]) for e in range(N_EXPERTS)])
    w1 = jnp.zeros_like(mx)
    w2 = jnp.zeros_like(mx)
    for e in range(N_EXPERTS):
        w1 = jnp.where(i1 == e, scores[e], w1)
        w2 = jnp.where(i2 == e, scores[e], w2)
    wsum = w1 + w2
    return i1, i2, w1 / wsum, w2 / wsum


def _outproj_kernel(y_ref, x_ref, w_ref, g_ref, b_ref, rwh_ref, rwl_ref, rb_ref,
                    x1g_ref, meta_ref, cnt_ref, gt_scr, carry_ref):
    tm = x_ref.shape[0]

    @pl.when(pl.program_id(0) == 0)
    def _():
        carry_ref[...] = jnp.zeros(carry_ref.shape, F32)

    mix = jnp.dot(y_ref[...].astype(BF16), w_ref[...], preferred_element_type=F32)
    x1 = _layer_norm(DEEPNORM_ALPHA * x_ref[...] + mix, g_ref[...], b_ref[...])
    x1g_ref[:, 0:D_MODEL] = x1
    x_hi = x1.astype(BF16)
    x_lo = (x1 - x_hi.astype(F32)).astype(BF16)
    logits = (jnp.dot(x_hi, rwh_ref[...], preferred_element_type=F32)
              + (jnp.dot(x_lo, rwh_ref[...], preferred_element_type=F32)
                 + jnp.dot(x_hi, rwl_ref[...], preferred_element_type=F32)))
    i1, i2, w1, w2 = _route(logits.T, rb_ref)

    lo = jnp.minimum(i1, i2)
    hi = jnp.maximum(i1, i2)
    a = lo & (EXPERTS_PER_GROUP - 1)
    b2 = hi & (EXPERTS_PER_GROUP - 1)
    bucket = (lo >> 2) * PAIRS_PER_GROUP + ((a * (7 - a)) >> 1) + (b2 - a - 1)

    onehot = lax.broadcasted_iota(jnp.int32, (BUCKET_ROWS, tm), 0) == bucket
    r = lax.broadcasted_iota(jnp.int32, (tm, tm), 0)
    c = lax.broadcasted_iota(jnp.int32, (tm, tm), 1)
    before = jnp.dot(onehot.astype(BF16), (r < c).astype(BF16), preferred_element_type=F32)
    rank = jnp.sum(jnp.where(onehot, before + carry_ref[:, 0:1], 0.0), axis=0, keepdims=True)
    carry_ref[...] = carry_ref[...] + jnp.sum(onehot.astype(F32), axis=1, keepdims=True)
    cnt_ref[...] = carry_ref[...].astype(jnp.int32)

    meta_ref[...] = jnp.zeros(meta_ref.shape, jnp.int32)
    meta_ref[0:1, :] = bucket
    meta_ref[1:2, :] = rank.astype(jnp.int32)

    gt_scr[...] = jnp.zeros(gt_scr.shape, F32)
    gt_scr[0:1, :] = jnp.where(i1 == lo, w1, w2)
    gt_scr[1:2, :] = jnp.where(i1 == lo, w2, w1)
    x1g_ref[:, D_MODEL:] = gt_scr[...].T


def _outproj_ln_route(y, x, w_out, ln_g, ln_b, router_w, router_b, tm=512):
    t, d = x.shape
    rw = jnp.zeros((d, LANES), F32).at[:, 0:N_EXPERTS].set(router_w.astype(F32))
    rw_hi = rw.astype(BF16)
    rw_lo = (rw - rw_hi.astype(F32)).astype(BF16)
    rb = jnp.broadcast_to(router_b.astype(F32)[:, None], (N_EXPERTS, LANES))
    return pl.pallas_call(
        _outproj_kernel, grid=(t // tm,),
        in_specs=[pl.BlockSpec((tm, y.shape[1]), lambda i: (i, 0)), pl.BlockSpec((tm, d), lambda i: (i, 0)),
                  _const_spec(w_out.shape), _const_spec((1, d)), _const_spec((1, d)),
                  _const_spec((d, LANES)), _const_spec((d, LANES)), _const_spec((N_EXPERTS, LANES))],
        out_specs=[pl.BlockSpec((tm, d + LANES), lambda i: (i, 0)),
                   pl.BlockSpec((8, tm), lambda i: (0, i)), _const_spec((BUCKET_ROWS, LANES))],
        out_shape=[jax.ShapeDtypeStruct((t, d + LANES), F32),
                   jax.ShapeDtypeStruct((8, t), jnp.int32),
                   jax.ShapeDtypeStruct((BUCKET_ROWS, LANES), jnp.int32)],
        scratch_shapes=[pltpu.VMEM((LANES, tm), F32), pltpu.VMEM((BUCKET_ROWS, LANES), F32)],
        compiler_params=_cparams("arbitrary"), name="outproj_ln_route")(
            y, x, w_out, ln_g.astype(F32).reshape(1, d), ln_b.astype(F32).reshape(1, d), rw_hi, rw_lo, rb)


def _moe_plan(meta, cnt, tile_rows, n_tiles):
    counts = cnt[0:N_BUCKETS, 0]
    tiles = (counts + tile_rows - 1) // tile_rows
    tile_end = jnp.cumsum(tiles)
    offs = (tile_end - tiles) * tile_rows
    dest = (offs[meta[0]] + meta[1]).astype(jnp.int32)
    n_used = tile_end[-1]
    j = jnp.minimum(jnp.arange(n_tiles, dtype=jnp.int32), n_used - 1)
    bucket_of_tile = jnp.minimum(jnp.sum(tile_end[None, :] <= j[:, None], axis=1), N_BUCKETS - 1)
    pairs = [(a, b) for a in range(EXPERTS_PER_GROUP) for b in range(a + 1, EXPERTS_PER_GROUP)]
    lo_tab = jnp.array([g * EXPERTS_PER_GROUP + a for g in range(N_GROUPS) for a, _ in pairs], jnp.int32)
    hi_tab = jnp.array([g * EXPERTS_PER_GROUP + b for g in range(N_GROUPS) for _, b in pairs], jnp.int32)
    return dest, lo_tab[bucket_of_tile], hi_tab[bucket_of_tile], n_used.astype(jnp.int32).reshape(1)


def _rows_to_tiles(ref, x, row_tiles, first=0):
    n = x.shape[0]
    for c in range(x.shape[1] // LANES):
        ref[pl.ds(first + c, n, stride=row_tiles), :] = x[:, c * LANES:(c + 1) * LANES]


def _tiles_to_rows(ref, n, row_tiles, first=0, count=D_MODEL // LANES):
    return jnp.concatenate([ref[pl.ds(first + c, n, stride=row_tiles), :] for c in range(count)], axis=1)


def _dispatch_kernel(dest_ref, x1g_ref, xs_in_ref, xs_ref, comb_ref, sem):
    del xs_in_ref
    tm = x1g_ref.shape[0]
    base = pl.program_id(0) * tm
    _rows_to_tiles(comb_ref, x1g_ref[:, 0:D_MODEL], X_ROW_TILES)

    def row_copy(r):
        src = pl.multiple_of(r * X_ROW_TILES, X_ROW_TILES)
        dst = pl.multiple_of(dest_ref[base + r] * X_ROW_TILES, X_ROW_TILES)
        return pltpu.make_async_copy(comb_ref.at[pl.ds(src, X_ROW_TILES)], xs_ref.at[pl.ds(dst, X_ROW_TILES)], sem)

    def issue(grp, carry):
        for u in range(DMA_ISSUE_UNROLL):
            row_copy(grp * DMA_ISSUE_UNROLL + u).start(priority=u % 2)
        return carry

    lax.fori_loop(0, tm // DMA_ISSUE_UNROLL, issue, 0)
    pltpu.make_async_copy(comb_ref, xs_ref.at[pl.ds(0, tm * X_ROW_TILES)], sem).wait()


def _dispatch(dest, x1g, xs, tm=256):
    t = x1g.shape[0]
    return pl.pallas_call(
        _dispatch_kernel,
        grid_spec=pltpu.PrefetchScalarGridSpec(
            num_scalar_prefetch=1, grid=(t // tm,),
            in_specs=[pl.BlockSpec((tm, x1g.shape[1]), lambda i, dest: (i, 0)),
                      pl.BlockSpec(memory_space=pl.ANY)],
            out_specs=pl.BlockSpec(memory_space=pl.ANY),
            scratch_shapes=[pltpu.VMEM((tm * X_ROW_TILES, LANES), F32), pltpu.SemaphoreType.DMA(())]),
        out_shape=jax.ShapeDtypeStruct(xs.shape, F32), input_output_aliases={2: 0},
        compiler_params=_cparams("arbitrary"), name="moe_dispatch")(dest, x1g, xs)


def _expert(x, wg_ref, wu_ref, wd_ref):
    hg = jnp.dot(x, wg_ref[0, 0], preferred_element_type=F32)
    hu = jnp.dot(x, wu_ref[0, 0], preferred_element_type=F32)
    return jnp.dot((_silu(hg) * hu).astype(BF16), wd_ref[0, 0], preferred_element_type=F32)


def _experts_kernel(lo_ref, hi_ref, nused_ref, xs_ref, wg0, wu0, wd0, wg1, wu1, wd1, ys_ref):
    del lo_ref, hi_ref
    tm = xs_ref.shape[0] // X_ROW_TILES

    @pl.when(pl.program_id(0) < nused_ref[0])
    def _():
        x = _tiles_to_rows(xs_ref, tm, X_ROW_TILES).astype(BF16)
        ys_ref[:, 0:D_MODEL] = _expert(x, wg0, wu0, wd0)
        ys_ref[:, D_MODEL:] = _expert(x, wg1, wu1, wd1)

    @pl.when(pl.program_id(0) >= nused_ref[0])
    def _():
        ys_ref[...] = jnp.zeros(ys_ref.shape, F32)


def _experts(layer, tile_lo, tile_hi, n_used, xs, wg, wu, wd, tm):
    d = D_MODEL
    n_tiles = xs.shape[0] // (tm * X_ROW_TILES)
    used = lambda j, n: jnp.maximum(jnp.minimum(j, n[0] - 1), 0)
    w_in = lambda which: pl.BlockSpec(
        (1, 1, d, D_EXPERT), lambda j, lo, hi, n: (layer, (lo, hi)[which][used(j, n)], 0, 0))
    w_out = lambda which: pl.BlockSpec(
        (1, 1, D_EXPERT, d), lambda j, lo, hi, n: (layer, (lo, hi)[which][used(j, n)], 0, 0))
    return pl.pallas_call(
        _experts_kernel,
        grid_spec=pltpu.PrefetchScalarGridSpec(
            num_scalar_prefetch=3, grid=(n_tiles,),
            in_specs=[pl.BlockSpec((tm * X_ROW_TILES, LANES), lambda j, lo, hi, n: (used(j, n), 0)),
                      w_in(0), w_in(0), w_out(0), w_in(1), w_in(1), w_out(1)],
            out_specs=pl.BlockSpec((tm, 2 * d), lambda j, lo, hi, n: (j, 0))),
        out_shape=jax.ShapeDtypeStruct((n_tiles * tm, 2 * d), F32),
        compiler_params=_cparams("arbitrary"), name="moe_experts")(
            tile_lo, tile_hi, n_used, xs, wg, wu, wd, wg, wu, wd)


def _combine_kernel(dest_ref, x1g_ref, p_ref, ys_ref, g_ref, b_ref, wpg_ref, wpp_ref, o_ref, ybuf, sems):
    i = pl.program_id(0)
    tm = x1g_ref.shape[0]
    slot = i % 2

    def row_copy(tile, s, r):
        return pltpu.make_async_copy(ys_ref.at[pl.ds(dest_ref[tile * tm + r], 1)],
                                     ybuf.at[s].at[pl.ds(r, 1)], sems.at[s])

    def start_tile(tile, s):
        def issue(grp, carry):
            for u in range(DMA_ISSUE_UNROLL):
                row_copy(tile, s, grp * DMA_ISSUE_UNROLL + u).start(priority=u % 2)
            return carry
        lax.fori_loop(0, tm // DMA_ISSUE_UNROLL, issue, 0)

    @pl.when(i == 0)
    def _():
        start_tile(0, 0)

    @pl.when(i + 1 < pl.num_programs(0))
    def _():
        start_tile(i + 1, 1 - slot)

    pltpu.make_async_copy(ys_ref.at[pl.ds(0, tm)], ybuf.at[slot], sems.at[slot]).wait()
    g = x1g_ref[:, D_MODEL:]
    y = g[:, 0:1] * ybuf[slot, :, 0:D_MODEL] + g[:, 1:2] * ybuf[slot, :, D_MODEL:]
    x2 = _layer_norm(DEEPNORM_ALPHA * x1g_ref[:, 0:D_MODEL] + y, g_ref[...], b_ref[...])
    gate = _sigmoid(jnp.dot(x2.astype(BF16), wpg_ref[...], preferred_element_type=F32))
    o_ref[...] = x2 + gate * jnp.dot(p_ref[...].astype(BF16), wpp_ref[...], preferred_element_type=F32)


def _combine_ln_ple(layer, dest, x1g, p, ys, ln_g, ln_b, w_gate, w_proj, tm=256):
    t = x1g.shape[0]
    d = D_MODEL
    const = lambda shape: pl.BlockSpec(shape, lambda i, dest: (0,) * len(shape))
    return pl.pallas_call(
        _combine_kernel,
        grid_spec=pltpu.PrefetchScalarGridSpec(
            num_scalar_prefetch=1, grid=(t // tm,),
            in_specs=[pl.BlockSpec((tm, x1g.shape[1]), lambda i, dest: (i, 0)),
                      pl.BlockSpec((tm, PLE_DIM), lambda i, dest: (layer * (t // tm) + i, 0)),
                      pl.BlockSpec(memory_space=pl.ANY), const((1, d)), const((1, d)),
                      const((d, d)), const((PLE_DIM, d))],
            out_specs=pl.BlockSpec((tm, d), lambda i, dest: (i, 0)),
            scratch_shapes=[pltpu.VMEM((2, tm, 2 * d), F32), pltpu.SemaphoreType.DMA((2,))]),
        out_shape=jax.ShapeDtypeStruct((t, d), F32),
        compiler_params=_cparams("arbitrary"), name="moe_combine_ln_ple")(
            dest, x1g, p, ys, ln_g.astype(F32).reshape(1, d), ln_b.astype(F32).reshape(1, d), w_gate, w_proj)


def _pad_cols(w, n):
    return jnp.zeros((w.shape[0], n), w.dtype).at[:, 0:w.shape[1]].set(w)


def kernel(x, p, ln1_g, ln1_b, ln2_g, ln2_b, mlstm_w_in, mlstm_conv, mlstm_gate_b, mlstm_norm_g, mlstm_w_out, gla_w_in, gla_w_a2, gla_b_a, gla_norm_g, gla_w_out, hgrn_w_in, hgrn_lb, hgrn_norm_g, hgrn_w_out, router_w, router_b, exp_w_gate, exp_w_up, exp_w_down, ple_w_proj, ple_w_gate):
    bsz, seq, d = x.shape
    assert bsz == 1 and d == D_MODEL and seq % 1024 == 0
    xt = x.reshape(seq, d).astype(F32)
    n_tiles = seq // MOE_TILE + N_BUCKETS
    xs = jnp.zeros((n_tiles * MOE_TILE * X_ROW_TILES, LANES), F32)
    wg_all, wu_all, wd_all = exp_w_gate.astype(BF16), exp_w_up.astype(BF16), exp_w_down.astype(BF16)
    for i in range(DEPTH):
        kind, slot = i % N_MIXERS, i // N_MIXERS
        if kind == 0:
            w_in = mlstm_w_in[slot]
            w = _pad_cols(w_in.astype(BF16), A_PROJ_PAD)
            wgt = jnp.zeros((16, d), BF16).at[0:2 * A_HEADS].set(w_in[:, A_MAIN:].T.astype(BF16))
            z, gt = _inproj(xt, w, wgt)
            y = _mlstm_mix(z, gt, mlstm_conv[slot], mlstm_gate_b[slot], mlstm_norm_g[slot])
            w_out = mlstm_w_out[slot]
        elif kind == 1:
            z = _inproj(xt, _pad_cols(gla_w_in[slot].astype(BF16), B_PROJ_PAD))
            y = _gla_mix(z, gla_w_a2[slot], gla_b_a[slot], gla_norm_g[slot])
            w_out = gla_w_out[slot]
        else:
            z = _inproj(xt, hgrn_w_in[slot].astype(BF16))
            y = _hgrn_mix(z, hgrn_lb, i, hgrn_norm_g[slot])
            w_out = hgrn_w_out[slot]
        x1g, meta, cnt = _outproj_ln_route(y, xt, w_out.astype(BF16), ln1_g[i], ln1_b[i], router_w, router_b)
        dest, tile_lo, tile_hi, n_used = _moe_plan(meta, cnt, MOE_TILE, n_tiles)
        xs = _dispatch(dest, x1g, xs)
        ys = _experts(i, tile_lo, tile_hi, n_used, xs, wg_all, wu_all, wd_all, MOE_TILE)
        xt = _combine_ln_ple(i, dest, x1g, p.reshape(DEPTH * seq, PLE_DIM), ys, ln2_g[i], ln2_b[i],
                             ple_w_gate[i].astype(BF16), ple_w_proj[i].astype(BF16))
    return xt.reshape(bsz, seq, d)
```

```python
import functools

import jax
import jax.numpy as jnp
from jax import lax
from jax.experimental import pallas as pl
from jax.experimental.pallas import tpu as pltpu

F32 = jnp.float32
BF16 = jnp.bfloat16

D_MODEL = 1024
DEPTH = 4
CHUNK = 64
PLE_DIM = 256
N_MIXERS = 3
LN_EPS = 1e-5
DEEPNORM_ALPHA = (2 * DEPTH) ** 0.25
A_HEADS, A_DK, A_DV = 4, 128, 256
B_HEADS, B_DK, B_DV = 4, 128, 256
B_RANK = 16
B_TAU = 16.0
C_HEADS, C_DK, C_DV = 8, 128, 128
N_EXPERTS = 16
N_GROUPS = 4
EXPERTS_PER_GROUP = 4
D_EXPERT = 512
PAIRS_PER_GROUP = EXPERTS_PER_GROUP * (EXPERTS_PER_GROUP - 1) // 2
N_BUCKETS = N_GROUPS * PAIRS_PER_GROUP
BUCKET_ROWS = 32
MOE_TILE = 256
DMA_ISSUE_UNROLL = 64

LANES = 128
SLOT = 8
X_ROW_TILES = D_MODEL // LANES
Y_ROW_TILES = 2 * X_ROW_TILES
N_SLOTS = CHUNK // SLOT
A_MAIN = 2 * A_HEADS * A_DK + 2 * A_HEADS * A_DV
A_PROJ_PAD = A_MAIN + LANES
B_MAIN = 2 * B_HEADS * B_DK + 2 * B_HEADS * B_DV
B_PROJ_PAD = B_MAIN + LANES
C_PROJ = 2 * C_HEADS * C_DK + 2 * C_HEADS * C_DV
VMEM_LIMIT = 56 * 1024 * 1024

NT_DIMS = (((1,), (1,)), ((), ()))
TN_DIMS = (((0,), (0,)), ((), ()))


def _cparams(*sem):
    return pltpu.CompilerParams(dimension_semantics=sem, vmem_limit_bytes=VMEM_LIMIT)


def _sigmoid(x):
    return 0.5 * jnp.tanh(0.5 * x) + 0.5


def _silu(x):
    return x * _sigmoid(x)


def _log_sigmoid(x):
    return jnp.minimum(x, 0.0) - jnp.log(1.0 + jnp.exp(-jnp.abs(x)))


def _layer_norm(u, g, b):
    mu = jnp.mean(u, axis=-1, keepdims=True)
    d = u - mu
    var = jnp.mean(d * d, axis=-1, keepdims=True)
    return d * lax.rsqrt(var + LN_EPS) * g + b


def _tri(n, lower):
    r = lax.broadcasted_iota(jnp.int32, (n, n), 0)
    c = lax.broadcasted_iota(jnp.int32, (n, n), 1)
    return (r >= c) if lower else (r <= c)


def _bdot(a, b):
    return jnp.dot(a.astype(BF16), b.astype(BF16), preferred_element_type=F32)


def _bdot_nt(a, b):
    return lax.dot_general(a.astype(BF16), b.astype(BF16), NT_DIMS, preferred_element_type=F32)


def _bdot_tn(a, b):
    return lax.dot_general(a.astype(BF16), b.astype(BF16), TN_DIMS, preferred_element_type=F32)


def _const_spec(shape):
    return pl.BlockSpec(shape, lambda i: (0,) * len(shape))


def _inproj_kernel(x_ref, w_ref, z_ref):
    z_ref[...] = jnp.dot(x_ref[...].astype(BF16), w_ref[...], preferred_element_type=F32)


def _inproj_gates_kernel(x_ref, w_ref, wgt_ref, z_ref, gt_ref):
    xb = x_ref[...].astype(BF16)
    z_ref[...] = jnp.dot(xb, w_ref[...], preferred_element_type=F32)
    for c in range(gt_ref.shape[0]):
        gt_ref[c] = lax.dot_general(wgt_ref[...], xb[c * CHUNK:(c + 1) * CHUNK], NT_DIMS,
                                    preferred_element_type=F32)


def _inproj(x, w, wgt=None, tm=512):
    t, d = x.shape
    n = w.shape[1]
    grid = (t // tm,)
    x_spec = pl.BlockSpec((tm, d), lambda i: (i, 0))
    z_spec = pl.BlockSpec((tm, n), lambda i: (i, 0))
    if wgt is None:
        return pl.pallas_call(
            _inproj_kernel, grid=grid, in_specs=[x_spec, _const_spec((d, n))], out_specs=z_spec,
            out_shape=jax.ShapeDtypeStruct((t, n), F32), compiler_params=_cparams("parallel"),
            name="inproj")(x, w)
    return pl.pallas_call(
        _inproj_gates_kernel, grid=grid,
        in_specs=[x_spec, _const_spec((d, n)), _const_spec(wgt.shape)],
        out_specs=[z_spec, pl.BlockSpec((tm // CHUNK, 16, CHUNK), lambda i: (i, 0, 0))],
        out_shape=[jax.ShapeDtypeStruct((t, n), F32),
                   jax.ShapeDtypeStruct((t // CHUNK, 16, CHUNK), F32)],
        compiler_params=_cparams("parallel"), name="inproj_gates")(x, w, wgt)


def _mlstm_kernel(z_ref, gt_ref, conv_ref, gbr_ref, gbc_ref, ng_ref, y_ref,
                  ext_ref, ct_ref, n_ref, m_ref):
    tt = z_ref.shape[0]
    nqk = A_HEADS * A_DK

    @pl.when(pl.program_id(0) == 0)
    def _():
        ext_ref[0:8, :] = jnp.zeros((8, 2 * nqk), F32)
        ct_ref[...] = jnp.zeros(ct_ref.shape, F32)
        n_ref[...] = jnp.zeros(n_ref.shape, F32)
        m_ref[...] = jnp.zeros(m_ref.shape, F32)

    qk_raw = z_ref[:, 0:2 * nqk]
    ext_ref[8:8 + tt, :] = qk_raw
    acc = conv_ref[3:4, :] * qk_raw
    for s in range(1, 4):
        acc = acc + conv_ref[3 - s:4 - s, :] * ext_ref[pl.ds(8 - s, tt), :]
    ext_ref[0:8, :] = qk_raw[tt - 8:tt, :]
    qkc = _silu(acc)

    tri_lo = _tri(CHUNK, True)
    tri_lo_f = tri_lo.astype(F32)
    tri_up_f = _tri(CHUNK, False).astype(F32)

    for c in range(tt // CHUNK):
        r0 = c * CHUNK
        gc = z_ref[r0:r0 + CHUNK, A_MAIN:A_MAIN + LANES] + gbr_ref[...]
        a_col = jnp.dot(tri_lo_f, _log_sigmoid(gc), preferred_element_type=F32,
                        precision=lax.Precision.HIGHEST)
        gr = gt_ref[c] + gbc_ref[:, 0:1]
        a_row = jnp.dot(_log_sigmoid(gr), tri_up_f, preferred_element_type=F32,
                        precision=lax.Precision.HIGHEST)
        for h in range(A_HEADS):
            ac = a_col[:, A_HEADS + h:A_HEADS + h + 1]
            lic = gc[:, h:h + 1]
            ar = a_row[A_HEADS + h:A_HEADS + h + 1, :]
            lir = gr[h:h + 1, :]
            m_st = m_ref[h:h + 1, 0:1]
            qh = qkc[r0:r0 + CHUNK, h * A_DK:(h + 1) * A_DK]
            kh = qkc[r0:r0 + CHUNK, nqk + h * A_DK:nqk + (h + 1) * A_DK] * (A_DK ** -0.5)
            vh = z_ref[r0:r0 + CHUNK, 2 * nqk + h * A_DV:2 * nqk + (h + 1) * A_DV]
            oh = z_ref[r0:r0 + CHUNK, 2 * nqk + A_HEADS * A_DV + h * A_DV:
                       2 * nqk + A_HEADS * A_DV + (h + 1) * A_DV]

            d_log = jnp.where(tri_lo, ac - ar + lir, -jnp.inf)
            d_max = jnp.max(d_log, axis=-1, keepdims=True)
            s = _bdot_nt(qh, kh) * jnp.exp(d_log - d_max)
            sv = _bdot(s, vh)
            s_sum = jnp.sum(s, axis=-1, keepdims=True)
            a_tot = ar[:, CHUNK - 1:CHUNK]
            g_max = jnp.max(a_tot - ar + lir, axis=-1, keepdims=True)
            wk = jnp.exp(a_tot - ac + lic - g_max) * kh
            c_inc = _bdot_tn(vh, wk)
            n_inc = jnp.sum(wk, axis=0, keepdims=True)

            e_log = ac + m_st
            m_row = jnp.maximum(e_log, d_max)
            w_inter = jnp.exp(e_log - m_row)
            w_intra = jnp.exp(d_max - m_row)
            num = w_inter * _bdot_nt(qh, ct_ref[h]) + w_intra * sv
            den = (w_inter * jnp.sum(qh * n_ref[h:h + 1, :], axis=-1, keepdims=True) + w_intra * s_sum)
            hout = num / jnp.maximum(jnp.abs(den), jnp.exp(-m_row))
            m_new = jnp.maximum(a_tot + m_st, g_max)
            decay = jnp.exp(a_tot + m_st - m_new)
            inc_scale = jnp.exp(g_max - m_new)
            ct_ref[h] = decay * ct_ref[h] + inc_scale * c_inc
            n_ref[h:h + 1, :] = decay * n_ref[h:h + 1, :] + inc_scale * n_inc
            m_ref[h:h + 1, :] = jnp.broadcast_to(m_new, (1, LANES))

            hn = hout * lax.rsqrt(jnp.mean(hout * hout, axis=-1, keepdims=True) + LN_EPS)
            y_ref[r0:r0 + CHUNK, h * A_DV:(h + 1) * A_DV] = (
                hn * ng_ref[:, h * A_DV:(h + 1) * A_DV] * _sigmoid(oh))


def _mlstm_mix(z, gt, conv_w, gate_b, norm_g, tt=256):
    t = z.shape[0]
    nqk2 = 2 * A_HEADS * A_DK
    conv_p = jnp.zeros((8, nqk2), F32).at[0:4].set(conv_w.astype(F32))
    gb = gate_b.astype(F32)
    gb_row = jnp.zeros((1, LANES), F32).at[0, 0:2 * A_HEADS].set(gb)
    gb_col = jnp.zeros((16, LANES), F32).at[0:2 * A_HEADS, :].set(jnp.broadcast_to(gb[:, None], (2 * A_HEADS, LANES)))
    ng = norm_g.astype(F32).reshape(1, A_HEADS * A_DV)
    return pl.pallas_call(
        _mlstm_kernel, grid=(t // tt,),
        in_specs=[pl.BlockSpec((tt, A_PROJ_PAD), lambda i: (i, 0)),
                  pl.BlockSpec((tt // CHUNK, 16, CHUNK), lambda i: (i, 0, 0)),
                  _const_spec((8, nqk2)), _const_spec((1, LANES)), _const_spec((16, LANES)),
                  _const_spec((1, A_HEADS * A_DV))],
        out_specs=pl.BlockSpec((tt, A_HEADS * A_DV), lambda i: (i, 0)),
        out_shape=jax.ShapeDtypeStruct((t, A_HEADS * A_DV), F32),
        scratch_shapes=[pltpu.VMEM((tt + 8, nqk2), F32),
                        pltpu.VMEM((A_HEADS, A_DV, A_DK), F32),
                        pltpu.VMEM((8, A_DK), F32),
                        pltpu.VMEM((8, LANES), F32)],
        compiler_params=_cparams("arbitrary"), name="mlstm_mix")(z, gt, conv_p, gb_row, gb_col, ng)


def _store_lane_tiles(ref, x):
    for t in range(ref.shape[0]):
        ref[t] = x[:, t * LANES:(t + 1) * LANES]


def _interleaved(ref):
    return jnp.concatenate(
        [jnp.concatenate([ref[t, pl.ds(c, N_SLOTS, stride=SLOT), :] for t in range(ref.shape[0])], axis=1)
         for c in range(SLOT)], axis=0)


def _pair_factors(q, k, b, ref_of_slot):
    zero = jnp.zeros((SLOT, q.shape[1]), F32)
    factors = []
    for half in (1, 2, 4):
        qs, ks = [], []
        for u in range(N_SLOTS):
            r = ref_of_slot((u // (2 * half)) * 2 * half + half - 1)
            rows = slice(u * SLOT, (u + 1) * SLOT)
            if u % (2 * half) >= half:
                qs.append(q[rows] * jnp.exp(b[rows] - r))
                ks.append(zero)
            else:
                qs.append(zero)
                ks.append(k[rows] * jnp.exp(r - b[rows]))
        factors.append((jnp.concatenate(qs, axis=0), jnp.concatenate(ks, axis=0)))
    return factors


def _pair_masks():
    r = lax.broadcasted_iota(jnp.int32, (CHUNK, CHUNK), 0)
    c = lax.broadcasted_iota(jnp.int32, (CHUNK, CHUNK), 1)
    sr, sc = r >> 3, c >> 3
    same_sublane = (r & (SLOT - 1)) == (c & (SLOT - 1))
    natural, interleaved = [], []
    for shift, half in ((1, 1), (2, 2), (3, 4)):
        m = jnp.logical_and((sr >> shift) == (sc >> shift),
                            jnp.logical_and((sr & half) != 0, (sc & half) == 0))
        natural.append(m)
        interleaved.append(jnp.logical_and(m, same_sublane))
    return r == c, natural, interleaved


def _decay_attention_chunk(q, k, v, la, st_ref, qkb_ref, vo_ref, heads, dv):
    dk = LANES
    tri = _tri(CHUNK, True).astype(BF16)
    la_1 = la.astype(BF16)
    rest = la - la_1.astype(F32)
    la_2 = rest.astype(BF16)
    la_3 = (rest - la_2.astype(F32)).astype(BF16)
    b = (jnp.dot(tri, la_1, preferred_element_type=F32)
         + (jnp.dot(tri, la_2, preferred_element_type=F32) + jnp.dot(tri, la_3, preferred_element_type=F32)))
    b_last = b[CHUNK - 1:CHUNK, :]
    qe = q * jnp.exp(b)
    kd = k * jnp.exp(b_last - b)

    _store_lane_tiles(qkb_ref.at[0], q)
    _store_lane_tiles(qkb_ref.at[1], k)
    _store_lane_tiles(qkb_ref.at[2], b)
    _store_lane_tiles(vo_ref.at[0], v)
    q_i, k_i, b_i = _interleaved(qkb_ref.at[0]), _interleaved(qkb_ref.at[1]), _interleaved(qkb_ref.at[2])
    v_i = _interleaved(vo_ref.at[0])

    eye, masks_n, masks_i = _pair_masks()
    fact_n = _pair_factors(q, k, b, lambda s: b[s * SLOT + SLOT - 1:(s + 1) * SLOT, :])
    fact_i = _pair_factors(q_i, k_i, b_i, lambda s: b_i[s * SLOT:(s + 1) * SLOT, :])

    outs = []
    for h in range(heads):
        kc = slice(h * dk, (h + 1) * dk)
        vc = slice(h * dv, (h + 1) * dv)
        att = jnp.where(eye, _bdot_nt(q[:, kc], k[:, kc]), 0.0)
        for (qf, kf), m in zip(fact_n, masks_n):
            att = att + jnp.where(m, _bdot_nt(qf[:, kc], kf[:, kc]), 0.0)
        att_i = jnp.zeros((CHUNK, CHUNK), F32)
        for (qf, kf), m in zip(fact_i, masks_i):
            att_i = att_i + jnp.where(m, _bdot_nt(qf[:, kc], kf[:, kc]), 0.0)
        outs.append(_bdot_nt(qe[:, kc], st_ref[h]) + _bdot(att, v[:, vc]))
        _store_lane_tiles(vo_ref.at[1, h * dv // LANES:(h + 1) * dv // LANES], _bdot(att_i, v_i[:, vc]))
        st_ref[h] = st_ref[h] * jnp.exp(b_last[:, kc]) + _bdot_tn(v[:, vc], kd[:, kc])
    o_i = _interleaved(vo_ref.at[1])
    return [o + o_i[:, h * dv:(h + 1) * dv] for h, o in enumerate(outs)]


def _head_norm_store(outs, y_ref, r0, ng_ref, gate_act, dv):
    for h, o in enumerate(outs):
        hn = o * lax.rsqrt(jnp.mean(o * o, axis=-1, keepdims=True) + LN_EPS)
        y_ref[r0:r0 + CHUNK, h * dv:(h + 1) * dv] = (
            hn * ng_ref[:, h * dv:(h + 1) * dv] * gate_act[:, h * dv:(h + 1) * dv])


def _gla_kernel(z_ref, wa2_ref, ba_ref, ng_ref, y_ref, st_ref, qkb_ref, vo_ref):
    @pl.when(pl.program_id(0) == 0)
    def _():
        st_ref[...] = jnp.zeros(st_ref.shape, F32)

    nk, nv = B_HEADS * B_DK, B_HEADS * B_DV
    for c in range(z_ref.shape[0] // CHUNK):
        r0 = c * CHUNK
        q = z_ref[r0:r0 + CHUNK, 0:nk] * (B_DK ** -0.5)
        k = z_ref[r0:r0 + CHUNK, nk:2 * nk]
        v = z_ref[r0:r0 + CHUNK, 2 * nk:2 * nk + nv]
        g = z_ref[r0:r0 + CHUNK, 2 * nk + nv:2 * nk + 2 * nv]
        a_lr = z_ref[r0:r0 + CHUNK, B_MAIN:B_MAIN + LANES]
        la = _log_sigmoid(jnp.dot(a_lr.astype(BF16), wa2_ref[...], preferred_element_type=F32)
                          + ba_ref[...]) / B_TAU
        outs = _decay_attention_chunk(q, k, v, la, st_ref, qkb_ref.at[c], vo_ref.at[c], B_HEADS, B_DV)
        _head_norm_store(outs, y_ref, r0, ng_ref, _silu(g), B_DV)


def _gla_mix(z, w_a2, b_a, norm_g, tt=256):
    t = z.shape[0]
    nk, nv = B_HEADS * B_DK, B_HEADS * B_DV
    wa2 = jnp.zeros((LANES, nk), BF16).at[0:B_RANK].set(w_a2.astype(BF16))
    return pl.pallas_call(
        _gla_kernel, grid=(t // tt,),
        in_specs=[pl.BlockSpec((tt, B_PROJ_PAD), lambda i: (i, 0)),
                  _const_spec((LANES, nk)), _const_spec((1, nk)), _const_spec((1, nv))],
        out_specs=pl.BlockSpec((tt, nv), lambda i: (i, 0)),
        out_shape=jax.ShapeDtypeStruct((t, nv), F32),
        scratch_shapes=[pltpu.VMEM((B_HEADS, B_DV, B_DK), F32),
                        pltpu.VMEM((tt // CHUNK, 3, nk // LANES, CHUNK, LANES), F32),
                        pltpu.VMEM((tt // CHUNK, 2, nv // LANES, CHUNK, LANES), F32)],
        compiler_params=_cparams("arbitrary"), name="gla_mix")(
            z, wa2, b_a.astype(F32).reshape(1, nk), norm_g.astype(F32).reshape(1, nv))


def _hgrn_kernel(layer, z_ref, lbp_ref, ng_ref, y_ref, st_ref, qkb_ref, vo_ref):
    @pl.when(pl.program_id(0) == 0)
    def _():
        st_ref[...] = jnp.zeros(st_ref.shape, F32)

    lbp = lbp_ref[0:DEPTH, :]
    e = jnp.exp(lbp - jnp.max(lbp, axis=0, keepdims=True))
    sm = e / jnp.sum(e, axis=0, keepdims=True)
    lb = jnp.zeros((1, sm.shape[1]), F32)
    for r in range(layer + 1):
        lb = lb + sm[r:r + 1, :]
    lb = lb - sm[0:1, :]

    nk, nv = C_HEADS * C_DK, C_HEADS * C_DV
    for c in range(z_ref.shape[0] // CHUNK):
        r0 = c * CHUNK
        q = _silu(z_ref[r0:r0 + CHUNK, 0:nk])
        fg = lb + (1.0 - lb) * _sigmoid(z_ref[r0:r0 + CHUNK, nk:2 * nk])
        v = z_ref[r0:r0 + CHUNK, 2 * nk:2 * nk + nv]
        g = z_ref[r0:r0 + CHUNK, 2 * nk + nv:2 * nk + 2 * nv]
        outs = _decay_attention_chunk(q, 1.0 - fg, v, jnp.log(fg), st_ref, qkb_ref.at[c], vo_ref.at[c],
                                      C_HEADS, C_DV)
        _head_norm_store(outs, y_ref, r0, ng_ref, _sigmoid(g), C_DV)


def _hgrn_mix(z, hgrn_lb, layer, norm_g, tt=256):
    t = z.shape[0]
    nk, nv = C_HEADS * C_DK, C_HEADS * C_DV
    lbp = jnp.zeros((8, nk), F32).at[0:DEPTH].set(hgrn_lb.astype(F32))
    return pl.pallas_call(
        functools.partial(_hgrn_kernel, layer), grid=(t // tt,),
        in_specs=[pl.BlockSpec((tt, C_PROJ), lambda i: (i, 0)),
                  _const_spec((8, nk)), _const_spec((1, nv))],
        out_specs=pl.BlockSpec((tt, nv), lambda i: (i, 0)),
        out_shape=jax.ShapeDtypeStruct((t, nv), F32),
        scratch_shapes=[pltpu.VMEM((C_HEADS, C_DV, C_DK), F32),
                        pltpu.VMEM((tt // CHUNK, 3, nk // LANES, CHUNK, LANES), F32),
                        pltpu.VMEM((tt // CHUNK, 2, nv // LANES, CHUNK, LANES), F32)],
        compiler_params=_cparams("arbitrary"), name="hgrn_mix")(
            z, lbp, norm_g.astype(F32).reshape(1, nv))


def _route(lt, rb_ref):
    rows = [lt[e:e + 1, :] for e in range(N_EXPERTS)]
    mx = rows[0]
    for r in rows[1:]:
        mx = jnp.maximum(mx, r)
    ex = [jnp.exp(r - mx) for r in rows]
    tot = ex[0]
    for r in ex[1:]:
        tot = tot + r
    scores = [r / tot for r in ex]
    sel = [scores[e] + rb_ref[e:e + 1, 0:1] for e in range(N_EXPERTS)]

    gscore = []
    for g in range(N_GROUPS):
        m = sel[g * EXPERTS_PER_GROUP:(g + 1) * EXPERTS_PER_GROUP]
        best = None
        for a in range(EXPERTS_PER_GROUP):
            for b2 in range(a + 1, EXPERTS_PER_GROUP):
                pair = m[a] + m[b2]
                best = pair if best is None else jnp.maximum(best, pair)
        gscore.append(best)
    g_best = gscore[0]
    g_idx = jnp.zeros_like(g_best, dtype=jnp.int32)
    for g in range(1, N_GROUPS):
        better = gscore[g] > g_best
        g_best = jnp.where(better, gscore[g], g_best)
        g_idx = jnp.where(better, g, g_idx)

    masked = [jnp.where(g_idx == (e // EXPERTS_PER_GROUP), sel[e], -jnp.inf) for e in range(N_EXPERTS)]

    def first_argmax(vals):
        best_v, best_i = vals[0], jnp.zeros_like(g_idx)
        for e in range(1, N_EXPERTS):
            better = vals[e] > best_v
            best_v = jnp.where(better, vals[e], best_v)
            best_i = jnp.where(better, e, best_i)
        return best_i

    i1 = first_argmax(masked)
    i2 = first_argmax([jnp.where(i1 == e, -jnp.inf, masked[e]) for e in range(N_EXPERTS)])
    w1 = jnp.zeros_like(mx)
    w2 = jnp.zeros_like(mx)
    for e in range(N_EXPERTS):
        w1 = jnp.where(i1 == e, scores[e], w1)
        w2 = jnp.where(i2 == e, scores[e], w2)
    wsum = w1 + w2
    return i1, i2, w1 / wsum, w2 / wsum


def _outproj_kernel(y_ref, x_ref, w_ref, g_ref, b_ref, rwh_ref, rwl_ref, rb_ref,
                    x1g_ref, meta_ref, cnt_ref, gt_scr, carry_ref):
    tm = x_ref.shape[0]

    @pl.when(pl.program_id(0) == 0)
    def _():
        carry_ref[...] = jnp.zeros(carry_ref.shape, F32)

    mix = jnp.dot(y_ref[...].astype(BF16), w_ref[...], preferred_element_type=F32)
    x1 = _layer_norm(DEEPNORM_ALPHA * x_ref[...] + mix, g_ref[...], b_ref[...])
    x1g_ref[:, 0:D_MODEL] = x1
    x_hi = x1.astype(BF16)
    x_lo = (x1 - x_hi.astype(F32)).astype(BF16)
    logits = (jnp.dot(x_hi, rwh_ref[...], preferred_element_type=F32)
              + (jnp.dot(x_lo, rwh_ref[...], preferred_element_type=F32)
                 + jnp.dot(x_hi, rwl_ref[...], preferred_element_type=F32)))
    i1, i2, w1, w2 = _route(logits.T, rb_ref)

    lo = jnp.minimum(i1, i2)
    hi = jnp.maximum(i1, i2)
    a = lo & (EXPERTS_PER_GROUP - 1)
    b2 = hi & (EXPERTS_PER_GROUP - 1)
    bucket = (lo >> 2) * PAIRS_PER_GROUP + ((a * (7 - a)) >> 1) + (b2 - a - 1)

    onehot = lax.broadcasted_iota(jnp.int32, (BUCKET_ROWS, tm), 0) == bucket
    r = lax.broadcasted_iota(jnp.int32, (tm, tm), 0)
    c = lax.broadcasted_iota(jnp.int32, (tm, tm), 1)
    before = jnp.dot(onehot.astype(BF16), (r < c).astype(BF16), preferred_element_type=F32)
    rank = jnp.sum(jnp.where(onehot, before + carry_ref[:, 0:1], 0.0), axis=0, keepdims=True)
    carry_ref[...] = carry_ref[...] + jnp.sum(onehot.astype(F32), axis=1, keepdims=True)
    cnt_ref[...] = carry_ref[...].astype(jnp.int32)

    meta_ref[...] = jnp.zeros(meta_ref.shape, jnp.int32)
    meta_ref[0:1, :] = bucket
    meta_ref[1:2, :] = rank.astype(jnp.int32)

    gt_scr[...] = jnp.zeros(gt_scr.shape, F32)
    gt_scr[0:1, :] = jnp.where(i1 == lo, w1, w2)
    gt_scr[1:2, :] = jnp.where(i1 == lo, w2, w1)
    x1g_ref[:, D_MODEL:] = gt_scr[...].T


def _outproj_ln_route(y, x, w_out, ln_g, ln_b, router_w, router_b, tm=512):
    t, d = x.shape
    rw = jnp.zeros((d, LANES), F32).at[:, 0:N_EXPERTS].set(router_w.astype(F32))
    rw_hi = rw.astype(BF16)
    rw_lo = (rw - rw_hi.astype(F32)).astype(BF16)
    rb = jnp.broadcast_to(router_b.astype(F32)[:, None], (N_EXPERTS, LANES))
    return pl.pallas_call(
        _outproj_kernel, grid=(t // tm,),
        in_specs=[pl.BlockSpec((tm, y.shape[1]), lambda i: (i, 0)), pl.BlockSpec((tm, d), lambda i: (i, 0)),
                  _const_spec(w_out.shape), _const_spec((1, d)), _const_spec((1, d)),
                  _const_spec((d, LANES)), _const_spec((d, LANES)), _const_spec((N_EXPERTS, LANES))],
        out_specs=[pl.BlockSpec((tm, d + LANES), lambda i: (i, 0)),
                   pl.BlockSpec((8, tm), lambda i: (0, i)), _const_spec((BUCKET_ROWS, LANES))],
        out_shape=[jax.ShapeDtypeStruct((t, d + LANES), F32),
                   jax.ShapeDtypeStruct((8, t), jnp.int32),
                   jax.ShapeDtypeStruct((BUCKET_ROWS, LANES), jnp.int32)],
        scratch_shapes=[pltpu.VMEM((LANES, tm), F32), pltpu.VMEM((BUCKET_ROWS, LANES), F32)],
        compiler_params=_cparams("arbitrary"), name="outproj_ln_route")(
            y, x, w_out, ln_g.astype(F32).reshape(1, d), ln_b.astype(F32).reshape(1, d), rw_hi, rw_lo, rb)


def _moe_plan(meta, cnt, tile_rows, n_tiles):
    counts = cnt[0:N_BUCKETS, 0]
    tiles = (counts + tile_rows - 1) // tile_rows
    tile_end = jnp.cumsum(tiles)
    offs = (tile_end - tiles) * tile_rows
    dest = (offs[meta[0]] + meta[1]).astype(jnp.int32)
    n_used = tile_end[-1]
    j = jnp.minimum(jnp.arange(n_tiles, dtype=jnp.int32), n_used - 1)
    bucket_of_tile = jnp.minimum(jnp.sum(tile_end[None, :] <= j[:, None], axis=1), N_BUCKETS - 1)
    pairs = [(a, b) for a in range(EXPERTS_PER_GROUP) for b in range(a + 1, EXPERTS_PER_GROUP)]
    lo_tab = jnp.array([g * EXPERTS_PER_GROUP + a for g in range(N_GROUPS) for a, _ in pairs], jnp.int32)
    hi_tab = jnp.array([g * EXPERTS_PER_GROUP + b for g in range(N_GROUPS) for _, b in pairs], jnp.int32)
    return dest, lo_tab[bucket_of_tile], hi_tab[bucket_of_tile], n_used.astype(jnp.int32).reshape(1)


def _rows_to_tiles(ref, x, row_tiles, first=0):
    n = x.shape[0]
    for c in range(x.shape[1] // LANES):
        ref[pl.ds(first + c, n, stride=row_tiles), :] = x[:, c * LANES:(c + 1) * LANES]


def _tiles_to_rows(ref, n, row_tiles, first=0, count=D_MODEL // LANES):
    return jnp.concatenate([ref[pl.ds(first + c, n, stride=row_tiles), :] for c in range(count)], axis=1)


def _dispatch_kernel(dest_ref, x1g_ref, xs_in_ref, xs_ref, comb_ref, sem):
    del xs_in_ref
    tm = x1g_ref.shape[0]
    base = pl.program_id(0) * tm
    _rows_to_tiles(comb_ref, x1g_ref[:, 0:D_MODEL], X_ROW_TILES)

    def row_copy(r):
        src = pl.multiple_of(r * X_ROW_TILES, X_ROW_TILES)
        dst = pl.multiple_of(dest_ref[base + r] * X_ROW_TILES, X_ROW_TILES)
        return pltpu.make_async_copy(comb_ref.at[pl.ds(src, X_ROW_TILES)], xs_ref.at[pl.ds(dst, X_ROW_TILES)], sem)

    def issue(grp, carry):
        for u in range(DMA_ISSUE_UNROLL):
            row_copy(grp * DMA_ISSUE_UNROLL + u).start(priority=u % 2)
        return carry

    lax.fori_loop(0, tm // DMA_ISSUE_UNROLL, issue, 0)
    pltpu.make_async_copy(comb_ref, xs_ref.at[pl.ds(0, tm * X_ROW_TILES)], sem).wait()


def _dispatch(dest, x1g, xs, tm=512):
    t = x1g.shape[0]
    return pl.pallas_call(
        _dispatch_kernel,
        grid_spec=pltpu.PrefetchScalarGridSpec(
            num_scalar_prefetch=1, grid=(t // tm,),
            in_specs=[pl.BlockSpec((tm, x1g.shape[1]), lambda i, dest: (i, 0)),
                      pl.BlockSpec(memory_space=pl.ANY)],
            out_specs=pl.BlockSpec(memory_space=pl.ANY),
            scratch_shapes=[pltpu.VMEM((tm * X_ROW_TILES, LANES), F32), pltpu.SemaphoreType.DMA(())]),
        out_shape=jax.ShapeDtypeStruct(xs.shape, F32), input_output_aliases={2: 0},
        compiler_params=_cparams("arbitrary"), name="moe_dispatch")(dest, x1g, xs)


def _expert(x, wg_ref, wu_ref, wd_ref):
    hg = jnp.dot(x, wg_ref[0, 0], preferred_element_type=F32)
    hu = jnp.dot(x, wu_ref[0, 0], preferred_element_type=F32)
    return jnp.dot((_silu(hg) * hu).astype(BF16), wd_ref[0, 0], preferred_element_type=F32)


def _experts_kernel(lo_ref, hi_ref, nused_ref, xs_ref, wg0, wu0, wd0, wg1, wu1, wd1, ys_ref):
    del lo_ref, hi_ref
    tm = xs_ref.shape[0] // X_ROW_TILES

    @pl.when(pl.program_id(0) < nused_ref[0])
    def _():
        x = _tiles_to_rows(xs_ref, tm, X_ROW_TILES).astype(BF16)
        _rows_to_tiles(ys_ref, _expert(x, wg0, wu0, wd0), Y_ROW_TILES)
        _rows_to_tiles(ys_ref, _expert(x, wg1, wu1, wd1), Y_ROW_TILES, first=X_ROW_TILES)

    @pl.when(pl.program_id(0) >= nused_ref[0])
    def _():
        ys_ref[...] = jnp.zeros(ys_ref.shape, F32)


def _experts(layer, tile_lo, tile_hi, n_used, xs, wg, wu, wd, tm):
    d = D_MODEL
    n_tiles = xs.shape[0] // (tm * X_ROW_TILES)
    used = lambda j, n: jnp.maximum(jnp.minimum(j, n[0] - 1), 0)
    w_in = lambda which: pl.BlockSpec(
        (1, 1, d, D_EXPERT), lambda j, lo, hi, n: (layer, (lo, hi)[which][used(j, n)], 0, 0))
    w_out = lambda which: pl.BlockSpec(
        (1, 1, D_EXPERT, d), lambda j, lo, hi, n: (layer, (lo, hi)[which][used(j, n)], 0, 0))
    return pl.pallas_call(
        _experts_kernel,
        grid_spec=pltpu.PrefetchScalarGridSpec(
            num_scalar_prefetch=3, grid=(n_tiles,),
            in_specs=[pl.BlockSpec((tm * X_ROW_TILES, LANES), lambda j, lo, hi, n: (used(j, n), 0)),
                      w_in(0), w_in(0), w_out(0), w_in(1), w_in(1), w_out(1)],
            out_specs=pl.BlockSpec((tm * Y_ROW_TILES, LANES), lambda j, lo, hi, n: (j, 0))),
        out_shape=jax.ShapeDtypeStruct((n_tiles * tm * Y_ROW_TILES, LANES), F32),
        compiler_params=_cparams("arbitrary"), name="moe_experts")(
            tile_lo, tile_hi, n_used, xs, wg, wu, wd, wg, wu, wd)


def _combine_kernel(dest_ref, x1g_ref, p_ref, ys_ref, g_ref, b_ref, wpg_ref, wpp_ref, o_ref, ybuf, sems):
    i = pl.program_id(0)
    tm = x1g_ref.shape[0]
    slot = i % 2

    def row_copy(tile, s, r):
        src = pl.multiple_of(dest_ref[tile * tm + r] * Y_ROW_TILES, Y_ROW_TILES)
        dst = pl.multiple_of(r * Y_ROW_TILES, Y_ROW_TILES)
        return pltpu.make_async_copy(ys_ref.at[pl.ds(src, Y_ROW_TILES)],
                                     ybuf.at[s].at[pl.ds(dst, Y_ROW_TILES)], sems.at[s])

    def start_tile(tile, s):
        def issue(grp, carry):
            for u in range(DMA_ISSUE_UNROLL):
                row_copy(tile, s, grp * DMA_ISSUE_UNROLL + u).start(priority=u % 2)
            return carry
        lax.fori_loop(0, tm // DMA_ISSUE_UNROLL, issue, 0)

    @pl.when(i == 0)
    def _():
        start_tile(0, 0)

    @pl.when(i + 1 < pl.num_programs(0))
    def _():
        start_tile(i + 1, 1 - slot)

    pltpu.make_async_copy(ys_ref.at[pl.ds(0, tm * Y_ROW_TILES)], ybuf.at[slot], sems.at[slot]).wait()
    g = x1g_ref[:, D_MODEL:]
    y = (g[:, 0:1] * _tiles_to_rows(ybuf.at[slot], tm, Y_ROW_TILES)
         + g[:, 1:2] * _tiles_to_rows(ybuf.at[slot], tm, Y_ROW_TILES, first=X_ROW_TILES))
    x2 = _layer_norm(DEEPNORM_ALPHA * x1g_ref[:, 0:D_MODEL] + y, g_ref[...], b_ref[...])
    gate = _sigmoid(jnp.dot(x2.astype(BF16), wpg_ref[...], preferred_element_type=F32))
    o_ref[...] = x2 + gate * jnp.dot(p_ref[...].astype(BF16), wpp_ref[...], preferred_element_type=F32)


def _combine_ln_ple(layer, dest, x1g, p, ys, ln_g, ln_b, w_gate, w_proj, tm=512):
    t = x1g.shape[0]
    d = D_MODEL
    const = lambda shape: pl.BlockSpec(shape, lambda i, dest: (0,) * len(shape))
    return pl.pallas_call(
        _combine_kernel,
        grid_spec=pltpu.PrefetchScalarGridSpec(
            num_scalar_prefetch=1, grid=(t // tm,),
            in_specs=[pl.BlockSpec((tm, x1g.shape[1]), lambda i, dest: (i, 0)),
                      pl.BlockSpec((tm, PLE_DIM), lambda i, dest: (layer * (t // tm) + i, 0)),
                      pl.BlockSpec(memory_space=pl.ANY), const((1, d)), const((1, d)),
                      const((d, d)), const((PLE_DIM, d))],
            out_specs=pl.BlockSpec((tm, d), lambda i, dest: (i, 0)),
            scratch_shapes=[pltpu.VMEM((2, tm * Y_ROW_TILES, LANES), F32), pltpu.SemaphoreType.DMA((2,))]),
        out_shape=jax.ShapeDtypeStruct((t, d), F32),
        compiler_params=_cparams("arbitrary"), name="moe_combine_ln_ple")(
            dest, x1g, p, ys, ln_g.astype(F32).reshape(1, d), ln_b.astype(F32).reshape(1, d), w_gate, w_proj)


def _pad_cols(w, n):
    return jnp.zeros((w.shape[0], n), w.dtype).at[:, 0:w.shape[1]].set(w)


def kernel(x, p, ln1_g, ln1_b, ln2_g, ln2_b, mlstm_w_in, mlstm_conv, mlstm_gate_b, mlstm_norm_g, mlstm_w_out, gla_w_in, gla_w_a2, gla_b_a, gla_norm_g, gla_w_out, hgrn_w_in, hgrn_lb, hgrn_norm_g, hgrn_w_out, router_w, router_b, exp_w_gate, exp_w_up, exp_w_down, ple_w_proj, ple_w_gate):
    bsz, seq, d = x.shape
    assert bsz == 1 and d == D_MODEL and seq % 1024 == 0
    xt = x.reshape(seq, d).astype(F32)
    n_tiles = seq // MOE_TILE + N_BUCKETS
    xs = jnp.zeros((n_tiles * MOE_TILE * X_ROW_TILES, LANES), F32)
    wg_all, wu_all, wd_all = exp_w_gate.astype(BF16), exp_w_up.astype(BF16), exp_w_down.astype(BF16)
    for i in range(DEPTH):
        kind, slot = i % N_MIXERS, i // N_MIXERS
        if kind == 0:
            w_in = mlstm_w_in[slot]
            w = _pad_cols(w_in.astype(BF16), A_PROJ_PAD)
            wgt = jnp.zeros((16, d), BF16).at[0:2 * A_HEADS].set(w_in[:, A_MAIN:].T.astype(BF16))
            z, gt = _inproj(xt, w, wgt)
            y = _mlstm_mix(z, gt, mlstm_conv[slot], mlstm_gate_b[slot], mlstm_norm_g[slot])
            w_out = mlstm_w_out[slot]
        elif kind == 1:
            z = _inproj(xt, _pad_cols(gla_w_in[slot].astype(BF16), B_PROJ_PAD))
            y = _gla_mix(z, gla_w_a2[slot], gla_b_a[slot], gla_norm_g[slot])
            w_out = gla_w_out[slot]
        else:
            z = _inproj(xt, hgrn_w_in[slot].astype(BF16))
            y = _hgrn_mix(z, hgrn_lb, i, hgrn_norm_g[slot])
            w_out = hgrn_w_out[slot]
        x1g, meta, cnt = _outproj_ln_route(y, xt, w_out.astype(BF16), ln1_g[i], ln1_b[i], router_w, router_b)
        dest, tile_lo, tile_hi, n_used = _moe_plan(meta, cnt, MOE_TILE, n_tiles)
        xs = _dispatch(dest, x1g, xs)
        ys = _experts(i, tile_lo, tile_hi, n_used, xs, wg_all, wu_all, wd_all, MOE_TILE)
        xt = _combine_ln_ple(i, dest, x1g, p.reshape(DEPTH * seq, PLE_DIM), ys, ln2_g[i], ln2_b[i],
                             ple_w_gate[i].astype(BF16), ple_w_proj[i].astype(BF16))
    return xt.reshape(bsz, seq, d)
```

```python
import functools

import jax
import jax.numpy as jnp
from jax import lax
from jax.experimental import pallas as pl
from jax.experimental.pallas import tpu as pltpu

F32 = jnp.float32
BF16 = jnp.bfloat16

D_MODEL = 1024
DEPTH = 4
CHUNK = 64
PLE_DIM = 256
N_MIXERS = 3
LN_EPS = 1e-5
DEEPNORM_ALPHA = (2 * DEPTH) ** 0.25
A_HEADS, A_DK, A_DV = 4, 128, 256
B_HEADS, B_DK, B_DV = 4, 128, 256
B_RANK = 16
B_TAU = 16.0
C_HEADS, C_DK, C_DV = 8, 128, 128
N_EXPERTS = 16
N_GROUPS = 4
EXPERTS_PER_GROUP = 4
D_EXPERT = 512
PAIRS_PER_GROUP = EXPERTS_PER_GROUP * (EXPERTS_PER_GROUP - 1) // 2
N_BUCKETS = N_GROUPS * PAIRS_PER_GROUP
BUCKET_ROWS = 32
MOE_TILE = 256
DMA_ISSUE_UNROLL = 64

LANES = 128
SLOT = 8
X_ROW_TILES = D_MODEL // LANES
Y_ROW_TILES = 2 * X_ROW_TILES
N_SLOTS = CHUNK // SLOT
A_MAIN = 2 * A_HEADS * A_DK + 2 * A_HEADS * A_DV
A_PROJ_PAD = A_MAIN + LANES
B_MAIN = 2 * B_HEADS * B_DK + 2 * B_HEADS * B_DV
B_PROJ_PAD = B_MAIN + LANES
C_PROJ = 2 * C_HEADS * C_DK + 2 * C_HEADS * C_DV
VMEM_LIMIT = 56 * 1024 * 1024

NT_DIMS = (((1,), (1,)), ((), ()))
TN_DIMS = (((0,), (0,)), ((), ()))


def _cparams(*sem):
    return pltpu.CompilerParams(dimension_semantics=sem, vmem_limit_bytes=VMEM_LIMIT)


def _sigmoid(x):
    return 0.5 * jnp.tanh(0.5 * x) + 0.5


def _silu(x):
    return x * _sigmoid(x)


def _log_sigmoid(x):
    return jnp.minimum(x, 0.0) - jnp.log(1.0 + jnp.exp(-jnp.abs(x)))


def _layer_norm(u, g, b):
    mu = jnp.mean(u, axis=-1, keepdims=True)
    d = u - mu
    var = jnp.mean(d * d, axis=-1, keepdims=True)
    return d * lax.rsqrt(var + LN_EPS) * g + b


def _tri(n, lower):
    r = lax.broadcasted_iota(jnp.int32, (n, n), 0)
    c = lax.broadcasted_iota(jnp.int32, (n, n), 1)
    return (r >= c) if lower else (r <= c)


def _bdot(a, b):
    return jnp.dot(a.astype(BF16), b.astype(BF16), preferred_element_type=F32)


def _bdot_nt(a, b):
    return lax.dot_general(a.astype(BF16), b.astype(BF16), NT_DIMS, preferred_element_type=F32)


def _bdot_tn(a, b):
    return lax.dot_general(a.astype(BF16), b.astype(BF16), TN_DIMS, preferred_element_type=F32)


def _const_spec(shape):
    return pl.BlockSpec(shape, lambda i: (0,) * len(shape))


def _inproj_kernel(x_ref, w_ref, z_ref):
    z_ref[...] = jnp.dot(x_ref[...].astype(BF16), w_ref[...], preferred_element_type=F32)


def _inproj_gates_kernel(x_ref, w_ref, wgt_ref, z_ref, gt_ref):
    xb = x_ref[...].astype(BF16)
    z_ref[...] = jnp.dot(xb, w_ref[...], preferred_element_type=F32)
    for c in range(gt_ref.shape[0]):
        gt_ref[c] = lax.dot_general(wgt_ref[...], xb[c * CHUNK:(c + 1) * CHUNK], NT_DIMS,
                                    preferred_element_type=F32)


def _inproj(x, w, wgt=None, tm=512):
    t, d = x.shape
    n = w.shape[1]
    grid = (t // tm,)
    x_spec = pl.BlockSpec((tm, d), lambda i: (i, 0))
    z_spec = pl.BlockSpec((tm, n), lambda i: (i, 0))
    if wgt is None:
        return pl.pallas_call(
            _inproj_kernel, grid=grid, in_specs=[x_spec, _const_spec((d, n))], out_specs=z_spec,
            out_shape=jax.ShapeDtypeStruct((t, n), F32), compiler_params=_cparams("parallel"),
            name="inproj")(x, w)
    return pl.pallas_call(
        _inproj_gates_kernel, grid=grid,
        in_specs=[x_spec, _const_spec((d, n)), _const_spec(wgt.shape)],
        out_specs=[z_spec, pl.BlockSpec((tm // CHUNK, 16, CHUNK), lambda i: (i, 0, 0))],
        out_shape=[jax.ShapeDtypeStruct((t, n), F32),
                   jax.ShapeDtypeStruct((t // CHUNK, 16, CHUNK), F32)],
        compiler_params=_cparams("parallel"), name="inproj_gates")(x, w, wgt)


def _mlstm_kernel(z_ref, gt_ref, conv_ref, gbr_ref, gbc_ref, ng_ref, y_ref,
                  ext_ref, ct_ref, n_ref, m_ref):
    tt = z_ref.shape[0]
    nqk = A_HEADS * A_DK

    @pl.when(pl.program_id(0) == 0)
    def _():
        ext_ref[0:8, :] = jnp.zeros((8, 2 * nqk), F32)
        ct_ref[...] = jnp.zeros(ct_ref.shape, F32)
        n_ref[...] = jnp.zeros(n_ref.shape, F32)
        m_ref[...] = jnp.zeros(m_ref.shape, F32)

    qk_raw = z_ref[:, 0:2 * nqk]
    ext_ref[8:8 + tt, :] = qk_raw
    acc = conv_ref[3:4, :] * qk_raw
    for s in range(1, 4):
        acc = acc + conv_ref[3 - s:4 - s, :] * ext_ref[pl.ds(8 - s, tt), :]
    ext_ref[0:8, :] = qk_raw[tt - 8:tt, :]
    qkc = _silu(acc)

    tri_lo = _tri(CHUNK, True)
    tri_lo_f = tri_lo.astype(F32)
    tri_up_f = _tri(CHUNK, False).astype(F32)

    for c in range(tt // CHUNK):
        r0 = c * CHUNK
        gc = z_ref[r0:r0 + CHUNK, A_MAIN:A_MAIN + LANES] + gbr_ref[...]
        a_col = jnp.dot(tri_lo_f, _log_sigmoid(gc), preferred_element_type=F32,
                        precision=lax.Precision.HIGHEST)
        gr = gt_ref[c] + gbc_ref[:, 0:1]
        a_row = jnp.dot(_log_sigmoid(gr), tri_up_f, preferred_element_type=F32,
                        precision=lax.Precision.HIGHEST)
        for h in range(A_HEADS):
            ac = a_col[:, A_HEADS + h:A_HEADS + h + 1]
            lic = gc[:, h:h + 1]
            ar = a_row[A_HEADS + h:A_HEADS + h + 1, :]
            lir = gr[h:h + 1, :]
            m_st = m_ref[h:h + 1, 0:1]
            qh = qkc[r0:r0 + CHUNK, h * A_DK:(h + 1) * A_DK]
            kh = qkc[r0:r0 + CHUNK, nqk + h * A_DK:nqk + (h + 1) * A_DK] * (A_DK ** -0.5)
            vh = z_ref[r0:r0 + CHUNK, 2 * nqk + h * A_DV:2 * nqk + (h + 1) * A_DV]
            oh = z_ref[r0:r0 + CHUNK, 2 * nqk + A_HEADS * A_DV + h * A_DV:
                       2 * nqk + A_HEADS * A_DV + (h + 1) * A_DV]

            d_log = jnp.where(tri_lo, ac - ar + lir, -jnp.inf)
            d_max = jnp.max(d_log, axis=-1, keepdims=True)
            s = _bdot_nt(qh, kh) * jnp.exp(d_log - d_max)
            sv = _bdot(s, vh)
            s_sum = jnp.sum(s, axis=-1, keepdims=True)
            a_tot = ar[:, CHUNK - 1:CHUNK]
            g_max = jnp.max(a_tot - ar + lir, axis=-1, keepdims=True)
            wk = jnp.exp(a_tot - ac + lic - g_max) * kh
            c_inc = _bdot_tn(vh, wk)
            n_inc = jnp.sum(wk, axis=0, keepdims=True)

            e_log = ac + m_st
            m_row = jnp.maximum(e_log, d_max)
            w_inter = jnp.exp(e_log - m_row)
            w_intra = jnp.exp(d_max - m_row)
            num = w_inter * _bdot_nt(qh, ct_ref[h]) + w_intra * sv
            den = (w_inter * jnp.sum(qh * n_ref[h:h + 1, :], axis=-1, keepdims=True) + w_intra * s_sum)
            hout = num / jnp.maximum(jnp.abs(den), jnp.exp(-m_row))
            m_new = jnp.maximum(a_tot + m_st, g_max)
            decay = jnp.exp(a_tot + m_st - m_new)
            inc_scale = jnp.exp(g_max - m_new)
            ct_ref[h] = decay * ct_ref[h] + inc_scale * c_inc
            n_ref[h:h + 1, :] = decay * n_ref[h:h + 1, :] + inc_scale * n_inc
            m_ref[h:h + 1, :] = jnp.broadcast_to(m_new, (1, LANES))

            hn = hout * lax.rsqrt(jnp.mean(hout * hout, axis=-1, keepdims=True) + LN_EPS)
            y_ref[r0:r0 + CHUNK, h * A_DV:(h + 1) * A_DV] = (
                hn * ng_ref[:, h * A_DV:(h + 1) * A_DV] * _sigmoid(oh))


def _mlstm_mix(z, gt, conv_w, gate_b, norm_g, tt=256):
    t = z.shape[0]
    nqk2 = 2 * A_HEADS * A_DK
    conv_p = jnp.zeros((8, nqk2), F32).at[0:4].set(conv_w.astype(F32))
    gb = gate_b.astype(F32)
    gb_row = jnp.zeros((1, LANES), F32).at[0, 0:2 * A_HEADS].set(gb)
    gb_col = jnp.zeros((16, LANES), F32).at[0:2 * A_HEADS, :].set(jnp.broadcast_to(gb[:, None], (2 * A_HEADS, LANES)))
    ng = norm_g.astype(F32).reshape(1, A_HEADS * A_DV)
    return pl.pallas_call(
        _mlstm_kernel, grid=(t // tt,),
        in_specs=[pl.BlockSpec((tt, A_PROJ_PAD), lambda i: (i, 0)),
                  pl.BlockSpec((tt // CHUNK, 16, CHUNK), lambda i: (i, 0, 0)),
                  _const_spec((8, nqk2)), _const_spec((1, LANES)), _const_spec((16, LANES)),
                  _const_spec((1, A_HEADS * A_DV))],
        out_specs=pl.BlockSpec((tt, A_HEADS * A_DV), lambda i: (i, 0)),
        out_shape=jax.ShapeDtypeStruct((t, A_HEADS * A_DV), F32),
        scratch_shapes=[pltpu.VMEM((tt + 8, nqk2), F32),
                        pltpu.VMEM((A_HEADS, A_DV, A_DK), F32),
                        pltpu.VMEM((8, A_DK), F32),
                        pltpu.VMEM((8, LANES), F32)],
        compiler_params=_cparams("arbitrary"), name="mlstm_mix")(z, gt, conv_p, gb_row, gb_col, ng)


def _store_lane_tiles(ref, x):
    for t in range(ref.shape[0]):
        ref[t] = x[:, t * LANES:(t + 1) * LANES]


def _interleaved(ref):
    return jnp.concatenate(
        [jnp.concatenate([ref[t, pl.ds(c, N_SLOTS, stride=SLOT), :] for t in range(ref.shape[0])], axis=1)
         for c in range(SLOT)], axis=0)


def _pair_factors(q, k, b, ref_of_slot):
    zero = jnp.zeros((SLOT, q.shape[1]), F32)
    factors = []
    for half in (1, 2, 4):
        qs, ks = [], []
        for u in range(N_SLOTS):
            r = ref_of_slot((u // (2 * half)) * 2 * half + half - 1)
            rows = slice(u * SLOT, (u + 1) * SLOT)
            if u % (2 * half) >= half:
                qs.append(q[rows] * jnp.exp(b[rows] - r))
                ks.append(zero)
            else:
                qs.append(zero)
                ks.append(k[rows] * jnp.exp(r - b[rows]))
        factors.append((jnp.concatenate(qs, axis=0), jnp.concatenate(ks, axis=0)))
    return factors


def _pair_masks():
    r = lax.broadcasted_iota(jnp.int32, (CHUNK, CHUNK), 0)
    c = lax.broadcasted_iota(jnp.int32, (CHUNK, CHUNK), 1)
    sr, sc = r >> 3, c >> 3
    same_sublane = (r & (SLOT - 1)) == (c & (SLOT - 1))
    natural, interleaved = [], []
    for shift, half in ((1, 1), (2, 2), (3, 4)):
        m = jnp.logical_and((sr >> shift) == (sc >> shift),
                            jnp.logical_and((sr & half) != 0, (sc & half) == 0))
        natural.append(m)
        interleaved.append(jnp.logical_and(m, same_sublane))
    return r == c, natural, interleaved


def _decay_attention_chunk(q, k, v, la, st_ref, qkb_ref, vo_ref, heads, dv):
    dk = LANES
    tri = _tri(CHUNK, True).astype(BF16)
    la_1 = la.astype(BF16)
    rest = la - la_1.astype(F32)
    la_2 = rest.astype(BF16)
    la_3 = (rest - la_2.astype(F32)).astype(BF16)
    b = (jnp.dot(tri, la_1, preferred_element_type=F32)
         + (jnp.dot(tri, la_2, preferred_element_type=F32) + jnp.dot(tri, la_3, preferred_element_type=F32)))
    b_last = b[CHUNK - 1:CHUNK, :]
    qe = q * jnp.exp(b)
    kd = k * jnp.exp(b_last - b)

    _store_lane_tiles(qkb_ref.at[0], q)
    _store_lane_tiles(qkb_ref.at[1], k)
    _store_lane_tiles(qkb_ref.at[2], b)
    _store_lane_tiles(vo_ref.at[0], v)
    q_i, k_i, b_i = _interleaved(qkb_ref.at[0]), _interleaved(qkb_ref.at[1]), _interleaved(qkb_ref.at[2])
    v_i = _interleaved(vo_ref.at[0])

    eye, masks_n, masks_i = _pair_masks()
    fact_n = _pair_factors(q, k, b, lambda s: b[s * SLOT + SLOT - 1:(s + 1) * SLOT, :])
    fact_i = _pair_factors(q_i, k_i, b_i, lambda s: b_i[s * SLOT:(s + 1) * SLOT, :])

    outs = []
    for h in range(heads):
        kc = slice(h * dk, (h + 1) * dk)
        vc = slice(h * dv, (h + 1) * dv)
        att = jnp.where(eye, _bdot_nt(q[:, kc], k[:, kc]), 0.0)
        for (qf, kf), m in zip(fact_n, masks_n):
            att = att + jnp.where(m, _bdot_nt(qf[:, kc], kf[:, kc]), 0.0)
        att_i = jnp.zeros((CHUNK, CHUNK), F32)
        for (qf, kf), m in zip(fact_i, masks_i):
            att_i = att_i + jnp.where(m, _bdot_nt(qf[:, kc], kf[:, kc]), 0.0)
        outs.append(_bdot_nt(qe[:, kc], st_ref[h]) + _bdot(att, v[:, vc]))
        _store_lane_tiles(vo_ref.at[1, h * dv // LANES:(h + 1) * dv // LANES], _bdot(att_i, v_i[:, vc]))
        st_ref[h] = st_ref[h] * jnp.exp(b_last[:, kc]) + _bdot_tn(v[:, vc], kd[:, kc])
    o_i = _interleaved(vo_ref.at[1])
    return [o + o_i[:, h * dv:(h + 1) * dv] for h, o in enumerate(outs)]


def _head_norm_store(outs, y_ref, r0, ng_ref, gate_act, dv):
    for h, o in enumerate(outs):
        hn = o * lax.rsqrt(jnp.mean(o * o, axis=-1, keepdims=True) + LN_EPS)
        y_ref[r0:r0 + CHUNK, h * dv:(h + 1) * dv] = (
            hn * ng_ref[:, h * dv:(h + 1) * dv] * gate_act[:, h * dv:(h + 1) * dv])


def _gla_kernel(z_ref, wa2_ref, ba_ref, ng_ref, y_ref, st_ref, qkb_ref, vo_ref):
    @pl.when(pl.program_id(0) == 0)
    def _():
        st_ref[...] = jnp.zeros(st_ref.shape, F32)

    nk, nv = B_HEADS * B_DK, B_HEADS * B_DV
    for c in range(z_ref.shape[0] // CHUNK):
        r0 = c * CHUNK
        q = z_ref[r0:r0 + CHUNK, 0:nk] * (B_DK ** -0.5)
        k = z_ref[r0:r0 + CHUNK, nk:2 * nk]
        v = z_ref[r0:r0 + CHUNK, 2 * nk:2 * nk + nv]
        g = z_ref[r0:r0 + CHUNK, 2 * nk + nv:2 * nk + 2 * nv]
        a_lr = z_ref[r0:r0 + CHUNK, B_MAIN:B_MAIN + LANES]
        la = _log_sigmoid(jnp.dot(a_lr.astype(BF16), wa2_ref[...], preferred_element_type=F32)
                          + ba_ref[...]) / B_TAU
        outs = _decay_attention_chunk(q, k, v, la, st_ref, qkb_ref.at[c], vo_ref.at[c], B_HEADS, B_DV)
        _head_norm_store(outs, y_ref, r0, ng_ref, _silu(g), B_DV)


def _gla_mix(z, w_a2, b_a, norm_g, tt=256):
    t = z.shape[0]
    nk, nv = B_HEADS * B_DK, B_HEADS * B_DV
    wa2 = jnp.zeros((LANES, nk), BF16).at[0:B_RANK].set(w_a2.astype(BF16))
    return pl.pallas_call(
        _gla_kernel, grid=(t // tt,),
        in_specs=[pl.BlockSpec((tt, B_PROJ_PAD), lambda i: (i, 0)),
                  _const_spec((LANES, nk)), _const_spec((1, nk)), _const_spec((1, nv))],
        out_specs=pl.BlockSpec((tt, nv), lambda i: (i, 0)),
        out_shape=jax.ShapeDtypeStruct((t, nv), F32),
        scratch_shapes=[pltpu.VMEM((B_HEADS, B_DV, B_DK), F32),
                        pltpu.VMEM((tt // CHUNK, 3, nk // LANES, CHUNK, LANES), F32),
                        pltpu.VMEM((tt // CHUNK, 2, nv // LANES, CHUNK, LANES), F32)],
        compiler_params=_cparams("arbitrary"), name="gla_mix")(
            z, wa2, b_a.astype(F32).reshape(1, nk), norm_g.astype(F32).reshape(1, nv))


def _hgrn_kernel(layer, z_ref, lbp_ref, ng_ref, y_ref, st_ref, qkb_ref, vo_ref):
    @pl.when(pl.program_id(0) == 0)
    def _():
        st_ref[...] = jnp.zeros(st_ref.shape, F32)

    lbp = lbp_ref[0:DEPTH, :]
    e = jnp.exp(lbp - jnp.max(lbp, axis=0, keepdims=True))
    sm = e / jnp.sum(e, axis=0, keepdims=True)
    lb = jnp.zeros((1, sm.shape[1]), F32)
    for r in range(layer + 1):
        lb = lb + sm[r:r + 1, :]
    lb = lb - sm[0:1, :]

    nk, nv = C_HEADS * C_DK, C_HEADS * C_DV
    for c in range(z_ref.shape[0] // CHUNK):
        r0 = c * CHUNK
        q = _silu(z_ref[r0:r0 + CHUNK, 0:nk])
        fg = lb + (1.0 - lb) * _sigmoid(z_ref[r0:r0 + CHUNK, nk:2 * nk])
        v = z_ref[r0:r0 + CHUNK, 2 * nk:2 * nk + nv]
        g = z_ref[r0:r0 + CHUNK, 2 * nk + nv:2 * nk + 2 * nv]
        outs = _decay_attention_chunk(q, 1.0 - fg, v, jnp.log(fg), st_ref, qkb_ref.at[c], vo_ref.at[c],
                                      C_HEADS, C_DV)
        _head_norm_store(outs, y_ref, r0, ng_ref, _sigmoid(g), C_DV)


def _hgrn_mix(z, hgrn_lb, layer, norm_g, tt=256):
    t = z.shape[0]
    nk, nv = C_HEADS * C_DK, C_HEADS * C_DV
    lbp = jnp.zeros((8, nk), F32).at[0:DEPTH].set(hgrn_lb.astype(F32))
    return pl.pallas_call(
        functools.partial(_hgrn_kernel, layer), grid=(t // tt,),
        in_specs=[pl.BlockSpec((tt, C_PROJ), lambda i: (i, 0)),
                  _const_spec((8, nk)), _const_spec((1, nv))],
        out_specs=pl.BlockSpec((tt, nv), lambda i: (i, 0)),
        out_shape=jax.ShapeDtypeStruct((t, nv), F32),
        scratch_shapes=[pltpu.VMEM((C_HEADS, C_DV, C_DK), F32),
                        pltpu.VMEM((tt // CHUNK, 3, nk // LANES, CHUNK, LANES), F32),
                        pltpu.VMEM((tt // CHUNK, 2, nv // LANES, CHUNK, LANES), F32)],
        compiler_params=_cparams("arbitrary"), name="hgrn_mix")(
            z, lbp, norm_g.astype(F32).reshape(1, nv))


def _route(lt, rb_ref):
    rows = [lt[e:e + 1, :] for e in range(N_EXPERTS)]
    mx = rows[0]
    for r in rows[1:]:
        mx = jnp.maximum(mx, r)
    ex = [jnp.exp(r - mx) for r in rows]
    tot = ex[0]
    for r in ex[1:]:
        tot = tot + r
    scores = [r / tot for r in ex]
    sel = [scores[e] + rb_ref[e:e + 1, 0:1] for e in range(N_EXPERTS)]

    gscore = []
    for g in range(N_GROUPS):
        m = sel[g * EXPERTS_PER_GROUP:(g + 1) * EXPERTS_PER_GROUP]
        best = None
        for a in range(EXPERTS_PER_GROUP):
            for b2 in range(a + 1, EXPERTS_PER_GROUP):
                pair = m[a] + m[b2]
                best = pair if best is None else jnp.maximum(best, pair)
        gscore.append(best)
    g_best = gscore[0]
    g_idx = jnp.zeros_like(g_best, dtype=jnp.int32)
    for g in range(1, N_GROUPS):
        better = gscore[g] > g_best
        g_best = jnp.where(better, gscore[g], g_best)
        g_idx = jnp.where(better, g, g_idx)

    masked = [jnp.where(g_idx == (e // EXPERTS_PER_GROUP), sel[e], -jnp.inf) for e in range(N_EXPERTS)]

    def first_argmax(vals):
        best_v, best_i = vals[0], jnp.zeros_like(g_idx)
        for e in range(1, N_EXPERTS):
            better = vals[e] > best_v
            best_v = jnp.where(better, vals[e], best_v)
            best_i = jnp.where(better, e, best_i)
        return best_i

    i1 = first_argmax(masked)
    i2 = first_argmax([jnp.where(i1 == e, -jnp.inf, masked[e]) for e in range(N_EXPERTS)])
    w1 = jnp.zeros_like(mx)
    w2 = jnp.zeros_like(mx)
    for e in range(N_EXPERTS):
        w1 = jnp.where(i1 == e, scores[e], w1)
        w2 = jnp.where(i2 == e, scores[e], w2)
    wsum = w1 + w2
    return i1, i2, w1 / wsum, w2 / wsum


def _outproj_kernel(y_ref, x_ref, w_ref, g_ref, b_ref, rwh_ref, rwl_ref, rb_ref,
                    x1g_ref, meta_ref, cnt_ref, gt_scr, carry_ref):
    tm = x_ref.shape[0]

    @pl.when(pl.program_id(0) == 0)
    def _():
        carry_ref[...] = jnp.zeros(carry_ref.shape, F32)

    mix = jnp.dot(y_ref[...].astype(BF16), w_ref[...], preferred_element_type=F32)
    x1 = _layer_norm(DEEPNORM_ALPHA * x_ref[...] + mix, g_ref[...], b_ref[...])
    x1g_ref[:, 0:D_MODEL] = x1
    x_hi = x1.astype(BF16)
    x_lo = (x1 - x_hi.astype(F32)).astype(BF16)
    logits = (jnp.dot(x_hi, rwh_ref[...], preferred_element_type=F32)
              + (jnp.dot(x_lo, rwh_ref[...], preferred_element_type=F32)
                 + jnp.dot(x_hi, rwl_ref[...], preferred_element_type=F32)))
    i1, i2, w1, w2 = _route(logits.T, rb_ref)

    lo = jnp.minimum(i1, i2)
    hi = jnp.maximum(i1, i2)
    a = lo & (EXPERTS_PER_GROUP - 1)
    b2 = hi & (EXPERTS_PER_GROUP - 1)
    bucket = (lo >> 2) * PAIRS_PER_GROUP + ((a * (7 - a)) >> 1) + (b2 - a - 1)

    onehot = lax.broadcasted_iota(jnp.int32, (BUCKET_ROWS, tm), 0) == bucket
    r = lax.broadcasted_iota(jnp.int32, (tm, tm), 0)
    c = lax.broadcasted_iota(jnp.int32, (tm, tm), 1)
    before = jnp.dot(onehot.astype(BF16), (r < c).astype(BF16), preferred_element_type=F32)
    rank = jnp.sum(jnp.where(onehot, before + carry_ref[:, 0:1], 0.0), axis=0, keepdims=True)
    carry_ref[...] = carry_ref[...] + jnp.sum(onehot.astype(F32), axis=1, keepdims=True)
    cnt_ref[...] = carry_ref[...].astype(jnp.int32)

    meta_ref[...] = jnp.zeros(meta_ref.shape, jnp.int32)
    meta_ref[0:1, :] = bucket
    meta_ref[1:2, :] = rank.astype(jnp.int32)

    gt_scr[...] = jnp.zeros(gt_scr.shape, F32)
    gt_scr[0:1, :] = jnp.where(i1 == lo, w1, w2)
    gt_scr[1:2, :] = jnp.where(i1 == lo, w2, w1)
    x1g_ref[:, D_MODEL:] = gt_scr[...].T


def _outproj_ln_route(y, x, w_out, ln_g, ln_b, router_w, router_b, tm=512):
    t, d = x.shape
    rw = jnp.zeros((d, LANES), F32).at[:, 0:N_EXPERTS].set(router_w.astype(F32))
    rw_hi = rw.astype(BF16)
    rw_lo = (rw - rw_hi.astype(F32)).astype(BF16)
    rb = jnp.broadcast_to(router_b.astype(F32)[:, None], (N_EXPERTS, LANES))
    return pl.pallas_call(
        _outproj_kernel, grid=(t // tm,),
        in_specs=[pl.BlockSpec((tm, y.shape[1]), lambda i: (i, 0)), pl.BlockSpec((tm, d), lambda i: (i, 0)),
                  _const_spec(w_out.shape), _const_spec((1, d)), _const_spec((1, d)),
                  _const_spec((d, LANES)), _const_spec((d, LANES)), _const_spec((N_EXPERTS, LANES))],
        out_specs=[pl.BlockSpec((tm, d + LANES), lambda i: (i, 0)),
                   pl.BlockSpec((8, tm), lambda i: (0, i)), _const_spec((BUCKET_ROWS, LANES))],
        out_shape=[jax.ShapeDtypeStruct((t, d + LANES), F32),
                   jax.ShapeDtypeStruct((8, t), jnp.int32),
                   jax.ShapeDtypeStruct((BUCKET_ROWS, LANES), jnp.int32)],
        scratch_shapes=[pltpu.VMEM((LANES, tm), F32), pltpu.VMEM((BUCKET_ROWS, LANES), F32)],
        compiler_params=_cparams("arbitrary"), name="outproj_ln_route")(
            y, x, w_out, ln_g.astype(F32).reshape(1, d), ln_b.astype(F32).reshape(1, d), rw_hi, rw_lo, rb)


def _moe_plan(meta, cnt, tile_rows, n_tiles):
    counts = cnt[0:N_BUCKETS, 0]
    tiles = (counts + tile_rows - 1) // tile_rows
    tile_end = jnp.cumsum(tiles)
    offs = (tile_end - tiles) * tile_rows
    dest = (offs[meta[0]] + meta[1]).astype(jnp.int32)
    n_used = tile_end[-1]
    j = jnp.minimum(jnp.arange(n_tiles, dtype=jnp.int32), n_used - 1)
    bucket_of_tile = jnp.minimum(jnp.sum(tile_end[None, :] <= j[:, None], axis=1), N_BUCKETS - 1)
    pairs = [(a, b) for a in range(EXPERTS_PER_GROUP) for b in range(a + 1, EXPERTS_PER_GROUP)]
    lo_tab = jnp.array([g * EXPERTS_PER_GROUP + a for g in range(N_GROUPS) for a, _ in pairs], jnp.int32)
    hi_tab = jnp.array([g * EXPERTS_PER_GROUP + b for g in range(N_GROUPS) for _, b in pairs], jnp.int32)
    return dest, lo_tab[bucket_of_tile], hi_tab[bucket_of_tile], n_used.astype(jnp.int32).reshape(1)


def _rows_to_tiles(ref, x, row_tiles, first=0):
    n = x.shape[0]
    for c in range(x.shape[1] // LANES):
        ref[pl.ds(first + c, n, stride=row_tiles), :] = x[:, c * LANES:(c + 1) * LANES]


def _tiles_to_rows(ref, n, row_tiles, first=0, count=D_MODEL // LANES):
    return jnp.concatenate([ref[pl.ds(first + c, n, stride=row_tiles), :] for c in range(count)], axis=1)


def _dispatch_kernel(dest_ref, x1g_ref, xs_in_ref, xs_ref, comb_ref, sem):
    del xs_in_ref
    tm = x1g_ref.shape[0]
    base = pl.program_id(0) * tm
    _rows_to_tiles(comb_ref, x1g_ref[:, 0:D_MODEL], X_ROW_TILES)

    def row_copy(r):
        src = pl.multiple_of(r * X_ROW_TILES, X_ROW_TILES)
        dst = pl.multiple_of(dest_ref[base + r] * X_ROW_TILES, X_ROW_TILES)
        return pltpu.make_async_copy(comb_ref.at[pl.ds(src, X_ROW_TILES)], xs_ref.at[pl.ds(dst, X_ROW_TILES)], sem)

    def issue(grp, carry):
        for u in range(DMA_ISSUE_UNROLL):
            row_copy(grp * DMA_ISSUE_UNROLL + u).start(priority=u % 2)
        return carry

    lax.fori_loop(0, tm // DMA_ISSUE_UNROLL, issue, 0)
    pltpu.make_async_copy(comb_ref, xs_ref.at[pl.ds(0, tm * X_ROW_TILES)], sem).wait()


def _dispatch(dest, x1g, xs, tm=1024):
    t = x1g.shape[0]
    return pl.pallas_call(
        _dispatch_kernel,
        grid_spec=pltpu.PrefetchScalarGridSpec(
            num_scalar_prefetch=1, grid=(t // tm,),
            in_specs=[pl.BlockSpec((tm, x1g.shape[1]), lambda i, dest: (i, 0)),
                      pl.BlockSpec(memory_space=pl.ANY)],
            out_specs=pl.BlockSpec(memory_space=pl.ANY),
            scratch_shapes=[pltpu.VMEM((tm * X_ROW_TILES, LANES), F32), pltpu.SemaphoreType.DMA(())]),
        out_shape=jax.ShapeDtypeStruct(xs.shape, F32), input_output_aliases={2: 0},
        compiler_params=_cparams("arbitrary"), name="moe_dispatch")(dest, x1g, xs)


def _expert(x, wg_ref, wu_ref, wd_ref):
    hg = jnp.dot(x, wg_ref[0, 0], preferred_element_type=F32)
    hu = jnp.dot(x, wu_ref[0, 0], preferred_element_type=F32)
    return jnp.dot((_silu(hg) * hu).astype(BF16), wd_ref[0, 0], preferred_element_type=F32)


def _experts_kernel(lo_ref, hi_ref, nused_ref, xs_ref, wg0, wu0, wd0, wg1, wu1, wd1, ys_ref):
    del lo_ref, hi_ref
    tm = xs_ref.shape[0] // X_ROW_TILES

    @pl.when(pl.program_id(0) < nused_ref[0])
    def _():
        x = _tiles_to_rows(xs_ref, tm, X_ROW_TILES).astype(BF16)
        _rows_to_tiles(ys_ref, _expert(x, wg0, wu0, wd0), Y_ROW_TILES)
        _rows_to_tiles(ys_ref, _expert(x, wg1, wu1, wd1), Y_ROW_TILES, first=X_ROW_TILES)

    @pl.when(pl.program_id(0) >= nused_ref[0])
    def _():
        ys_ref[...] = jnp.zeros(ys_ref.shape, F32)


def _experts(layer, tile_lo, tile_hi, n_used, xs, wg, wu, wd, tm):
    d = D_MODEL
    n_tiles = xs.shape[0] // (tm * X_ROW_TILES)
    used = lambda j, n: jnp.maximum(jnp.minimum(j, n[0] - 1), 0)
    w_in = lambda which: pl.BlockSpec(
        (1, 1, d, D_EXPERT), lambda j, lo, hi, n: (layer, (lo, hi)[which][used(j, n)], 0, 0))
    w_out = lambda which: pl.BlockSpec(
        (1, 1, D_EXPERT, d), lambda j, lo, hi, n: (layer, (lo, hi)[which][used(j, n)], 0, 0))
    return pl.pallas_call(
        _experts_kernel,
        grid_spec=pltpu.PrefetchScalarGridSpec(
            num_scalar_prefetch=3, grid=(n_tiles,),
            in_specs=[pl.BlockSpec((tm * X_ROW_TILES, LANES), lambda j, lo, hi, n: (used(j, n), 0)),
                      w_in(0), w_in(0), w_out(0), w_in(1), w_in(1), w_out(1)],
            out_specs=pl.BlockSpec((tm * Y_ROW_TILES, LANES), lambda j, lo, hi, n: (j, 0))),
        out_shape=jax.ShapeDtypeStruct((n_tiles * tm * Y_ROW_TILES, LANES), F32),
        compiler_params=_cparams("arbitrary"), name="moe_experts")(
            tile_lo, tile_hi, n_used, xs, wg, wu, wd, wg, wu, wd)


def _combine_kernel(dest_ref, x1g_ref, p_ref, ys_ref, g_ref, b_ref, wpg_ref, wpp_ref, o_ref, ybuf, sems):
    i = pl.program_id(0)
    tm = x1g_ref.shape[0]
    slot = i % 2

    def row_copy(tile, s, r):
        src = pl.multiple_of(dest_ref[tile * tm + r] * Y_ROW_TILES, Y_ROW_TILES)
        dst = pl.multiple_of(r * Y_ROW_TILES, Y_ROW_TILES)
        return pltpu.make_async_copy(ys_ref.at[pl.ds(src, Y_ROW_TILES)],
                                     ybuf.at[s].at[pl.ds(dst, Y_ROW_TILES)], sems.at[s])

    def start_tile(tile, s):
        def issue(grp, carry):
            for u in range(DMA_ISSUE_UNROLL):
                row_copy(tile, s, grp * DMA_ISSUE_UNROLL + u).start(priority=u % 2)
            return carry
        lax.fori_loop(0, tm // DMA_ISSUE_UNROLL, issue, 0)

    @pl.when(i == 0)
    def _():
        start_tile(0, 0)

    @pl.when(i + 1 < pl.num_programs(0))
    def _():
        start_tile(i + 1, 1 - slot)

    pltpu.make_async_copy(ys_ref.at[pl.ds(0, tm * Y_ROW_TILES)], ybuf.at[slot], sems.at[slot]).wait()
    g = x1g_ref[:, D_MODEL:]
    y = (g[:, 0:1] * _tiles_to_rows(ybuf.at[slot], tm, Y_ROW_TILES)
         + g[:, 1:2] * _tiles_to_rows(ybuf.at[slot], tm, Y_ROW_TILES, first=X_ROW_TILES))
    x2 = _layer_norm(DEEPNORM_ALPHA * x1g_ref[:, 0:D_MODEL] + y, g_ref[...], b_ref[...])
    gate = _sigmoid(jnp.dot(x2.astype(BF16), wpg_ref[...], preferred_element_type=F32))
    o_ref[...] = x2 + gate * jnp.dot(p_ref[...].astype(BF16), wpp_ref[...], preferred_element_type=F32)


def _combine_ln_ple(layer, dest, x1g, p, ys, ln_g, ln_b, w_gate, w_proj, tm=1024):
    t = x1g.shape[0]
    d = D_MODEL
    const = lambda shape: pl.BlockSpec(shape, lambda i, dest: (0,) * len(shape))
    return pl.pallas_call(
        _combine_kernel,
        grid_spec=pltpu.PrefetchScalarGridSpec(
            num_scalar_prefetch=1, grid=(t // tm,),
            in_specs=[pl.BlockSpec((tm, x1g.shape[1]), lambda i, dest: (i, 0)),
                      pl.BlockSpec((tm, PLE_DIM), lambda i, dest: (layer * (t // tm) + i, 0)),
                      pl.BlockSpec(memory_space=pl.ANY), const((1, d)), const((1, d)),
                      const((d, d)), const((PLE_DIM, d))],
            out_specs=pl.BlockSpec((tm, d), lambda i, dest: (i, 0)),
            scratch_shapes=[pltpu.VMEM((2, tm * Y_ROW_TILES, LANES), F32), pltpu.SemaphoreType.DMA((2,))]),
        out_shape=jax.ShapeDtypeStruct((t, d), F32),
        compiler_params=_cparams("arbitrary"), name="moe_combine_ln_ple")(
            dest, x1g, p, ys, ln_g.astype(F32).reshape(1, d), ln_b.astype(F32).reshape(1, d), w_gate, w_proj)


def _pad_cols(w, n):
    return jnp.zeros((w.shape[0], n), w.dtype).at[:, 0:w.shape[1]].set(w)


def kernel(x, p, ln1_g, ln1_b, ln2_g, ln2_b, mlstm_w_in, mlstm_conv, mlstm_gate_b, mlstm_norm_g, mlstm_w_out, gla_w_in, gla_w_a2, gla_b_a, gla_norm_g, gla_w_out, hgrn_w_in, hgrn_lb, hgrn_norm_g, hgrn_w_out, router_w, router_b, exp_w_gate, exp_w_up, exp_w_down, ple_w_proj, ple_w_gate):
    bsz, seq, d = x.shape
    assert bsz == 1 and d == D_MODEL and seq % 1024 == 0
    xt = x.reshape(seq, d).astype(F32)
    n_tiles = seq // MOE_TILE + N_BUCKETS
    xs = jnp.zeros((n_tiles * MOE_TILE * X_ROW_TILES, LANES), F32)
    wg_all, wu_all, wd_all = exp_w_gate.astype(BF16), exp_w_up.astype(BF16), exp_w_down.astype(BF16)
    for i in range(DEPTH):
        kind, slot = i % N_MIXERS, i // N_MIXERS
        if kind == 0:
            w_in = mlstm_w_in[slot]
            w = _pad_cols(w_in.astype(BF16), A_PROJ_PAD)
            wgt = jnp.zeros((16, d), BF16).at[0:2 * A_HEADS].set(w_in[:, A_MAIN:].T.astype(BF16))
            z, gt = _inproj(xt, w, wgt)
            y = _mlstm_mix(z, gt, mlstm_conv[slot], mlstm_gate_b[slot], mlstm_norm_g[slot])
            w_out = mlstm_w_out[slot]
        elif kind == 1:
            z = _inproj(xt, _pad_cols(gla_w_in[slot].astype(BF16), B_PROJ_PAD))
            y = _gla_mix(z, gla_w_a2[slot], gla_b_a[slot], gla_norm_g[slot])
            w_out = gla_w_out[slot]
        else:
            z = _inproj(xt, hgrn_w_in[slot].astype(BF16))
            y = _hgrn_mix(z, hgrn_lb, i, hgrn_norm_g[slot])
            w_out = hgrn_w_out[slot]
        x1g, meta, cnt = _outproj_ln_route(y, xt, w_out.astype(BF16), ln1_g[i], ln1_b[i], router_w, router_b)
        dest, tile_lo, tile_hi, n_used = _moe_plan(meta, cnt, MOE_TILE, n_tiles)
        xs = _dispatch(dest, x1g, xs)
        ys = _experts(i, tile_lo, tile_hi, n_used, xs, wg_all, wu_all, wd_all, MOE_TILE)
        xt = _combine_ln_ple(i, dest, x1g, p.reshape(DEPTH * seq, PLE_DIM), ys, ln2_g[i], ln2_b[i],
                             ple_w_gate[i].astype(BF16), ple_w_proj[i].astype(BF16))
    return xt.reshape(bsz, seq, d)
```

```python
import functools

import jax
import jax.numpy as jnp
from jax import lax
from jax.experimental import pallas as pl
from jax.experimental.pallas import tpu as pltpu

F32 = jnp.float32
BF16 = jnp.bfloat16

D_MODEL = 1024
DEPTH = 4
CHUNK = 64
PLE_DIM = 256
N_MIXERS = 3
LN_EPS = 1e-5
DEEPNORM_ALPHA = (2 * DEPTH) ** 0.25
A_HEADS, A_DK, A_DV = 4, 128, 256
B_HEADS, B_DK, B_DV = 4, 128, 256
B_RANK = 16
B_TAU = 16.0
C_HEADS, C_DK, C_DV = 8, 128, 128
N_EXPERTS = 16
N_GROUPS = 4
EXPERTS_PER_GROUP = 4
D_EXPERT = 512
PAIRS_PER_GROUP = EXPERTS_PER_GROUP * (EXPERTS_PER_GROUP - 1) // 2
N_BUCKETS = N_GROUPS * PAIRS_PER_GROUP
BUCKET_ROWS = 32
MOE_TILE = 256
DMA_ISSUE_UNROLL = 64

LANES = 128
SLOT = 8
X_ROW_TILES = D_MODEL // LANES
Y_ROW_TILES = 2 * X_ROW_TILES
N_SLOTS = CHUNK // SLOT
A_MAIN = 2 * A_HEADS * A_DK + 2 * A_HEADS * A_DV
A_PROJ_PAD = A_MAIN + LANES
B_MAIN = 2 * B_HEADS * B_DK + 2 * B_HEADS * B_DV
B_PROJ_PAD = B_MAIN + LANES
C_PROJ = 2 * C_HEADS * C_DK + 2 * C_HEADS * C_DV
VMEM_LIMIT = 56 * 1024 * 1024

NT_DIMS = (((1,), (1,)), ((), ()))
TN_DIMS = (((0,), (0,)), ((), ()))


def _cparams(*sem):
    return pltpu.CompilerParams(dimension_semantics=sem, vmem_limit_bytes=VMEM_LIMIT)


def _sigmoid(x):
    return 0.5 * jnp.tanh(0.5 * x) + 0.5


def _silu(x):
    return x * _sigmoid(x)


def _log_sigmoid(x):
    return jnp.minimum(x, 0.0) - jnp.log(1.0 + jnp.exp(-jnp.abs(x)))


def _layer_norm(u, g, b):
    mu = jnp.mean(u, axis=-1, keepdims=True)
    d = u - mu
    var = jnp.mean(d * d, axis=-1, keepdims=True)
    return d * lax.rsqrt(var + LN_EPS) * g + b


def _tri(n, lower):
    r = lax.broadcasted_iota(jnp.int32, (n, n), 0)
    c = lax.broadcasted_iota(jnp.int32, (n, n), 1)
    return (r >= c) if lower else (r <= c)


def _bdot(a, b):
    return jnp.dot(a.astype(BF16), b.astype(BF16), preferred_element_type=F32)


def _bdot_nt(a, b):
    return lax.dot_general(a.astype(BF16), b.astype(BF16), NT_DIMS, preferred_element_type=F32)


def _bdot_tn(a, b):
    return lax.dot_general(a.astype(BF16), b.astype(BF16), TN_DIMS, preferred_element_type=F32)


def _const_spec(shape):
    return pl.BlockSpec(shape, lambda i: (0,) * len(shape))


def _inproj_kernel(x_ref, w_ref, z_ref):
    z_ref[...] = jnp.dot(x_ref[...].astype(BF16), w_ref[...], preferred_element_type=F32)


def _inproj_gates_kernel(x_ref, w_ref, wgt_ref, z_ref, gt_ref):
    xb = x_ref[...].astype(BF16)
    z_ref[...] = jnp.dot(xb, w_ref[...], preferred_element_type=F32)
    for c in range(gt_ref.shape[0]):
        gt_ref[c] = lax.dot_general(wgt_ref[...], xb[c * CHUNK:(c + 1) * CHUNK], NT_DIMS,
                                    preferred_element_type=F32)


def _inproj(x, w, wgt=None, tm=512):
    t, d = x.shape
    n = w.shape[1]
    grid = (t // tm,)
    x_spec = pl.BlockSpec((tm, d), lambda i: (i, 0))
    z_spec = pl.BlockSpec((tm, n), lambda i: (i, 0))
    if wgt is None:
        return pl.pallas_call(
            _inproj_kernel, grid=grid, in_specs=[x_spec, _const_spec((d, n))], out_specs=z_spec,
            out_shape=jax.ShapeDtypeStruct((t, n), F32), compiler_params=_cparams("parallel"),
            name="inproj")(x, w)
    return pl.pallas_call(
        _inproj_gates_kernel, grid=grid,
        in_specs=[x_spec, _const_spec((d, n)), _const_spec(wgt.shape)],
        out_specs=[z_spec, pl.BlockSpec((tm // CHUNK, 16, CHUNK), lambda i: (i, 0, 0))],
        out_shape=[jax.ShapeDtypeStruct((t, n), F32),
                   jax.ShapeDtypeStruct((t // CHUNK, 16, CHUNK), F32)],
        compiler_params=_cparams("parallel"), name="inproj_gates")(x, w, wgt)


def _mlstm_kernel(z_ref, gt_ref, conv_ref, gbr_ref, gbc_ref, ng_ref, y_ref,
                  ext_ref, ct_ref, n_ref, m_ref):
    tt = z_ref.shape[0]
    nqk = A_HEADS * A_DK

    @pl.when(pl.program_id(0) == 0)
    def _():
        ext_ref[0:8, :] = jnp.zeros((8, 2 * nqk), F32)
        ct_ref[...] = jnp.zeros(ct_ref.shape, F32)
        n_ref[...] = jnp.zeros(n_ref.shape, F32)
        m_ref[...] = jnp.zeros(m_ref.shape, F32)

    qk_raw = z_ref[:, 0:2 * nqk]
    ext_ref[8:8 + tt, :] = qk_raw
    acc = conv_ref[3:4, :] * qk_raw
    for s in range(1, 4):
        acc = acc + conv_ref[3 - s:4 - s, :] * ext_ref[pl.ds(8 - s, tt), :]
    ext_ref[0:8, :] = qk_raw[tt - 8:tt, :]
    qkc = _silu(acc)

    tri_lo = _tri(CHUNK, True)
    tri_lo_f = tri_lo.astype(F32)
    tri_up_f = _tri(CHUNK, False).astype(F32)

    for c in range(tt // CHUNK):
        r0 = c * CHUNK
        gc = z_ref[r0:r0 + CHUNK, A_MAIN:A_MAIN + LANES] + gbr_ref[...]
        a_col = jnp.dot(tri_lo_f, _log_sigmoid(gc), preferred_element_type=F32,
                        precision=lax.Precision.HIGHEST)
        gr = gt_ref[c] + gbc_ref[:, 0:1]
        a_row = jnp.dot(_log_sigmoid(gr), tri_up_f, preferred_element_type=F32,
                        precision=lax.Precision.HIGHEST)
        for h in range(A_HEADS):
            ac = a_col[:, A_HEADS + h:A_HEADS + h + 1]
            lic = gc[:, h:h + 1]
            ar = a_row[A_HEADS + h:A_HEADS + h + 1, :]
            lir = gr[h:h + 1, :]
            m_st = m_ref[h:h + 1, 0:1]
            qh = qkc[r0:r0 + CHUNK, h * A_DK:(h + 1) * A_DK]
            kh = qkc[r0:r0 + CHUNK, nqk + h * A_DK:nqk + (h + 1) * A_DK] * (A_DK ** -0.5)
            vh = z_ref[r0:r0 + CHUNK, 2 * nqk + h * A_DV:2 * nqk + (h + 1) * A_DV]
            oh = z_ref[r0:r0 + CHUNK, 2 * nqk + A_HEADS * A_DV + h * A_DV:
                       2 * nqk + A_HEADS * A_DV + (h + 1) * A_DV]

            d_log = jnp.where(tri_lo, ac - ar + lir, -jnp.inf)
            d_max = jnp.max(d_log, axis=-1, keepdims=True)
            s = _bdot_nt(qh, kh) * jnp.exp(d_log - d_max)
            sv = _bdot(s, vh)
            s_sum = jnp.sum(s, axis=-1, keepdims=True)
            a_tot = ar[:, CHUNK - 1:CHUNK]
            g_max = jnp.max(a_tot - ar + lir, axis=-1, keepdims=True)
            wk = jnp.exp(a_tot - ac + lic - g_max) * kh
            c_inc = _bdot_tn(vh, wk)
            n_inc = jnp.sum(wk, axis=0, keepdims=True)

            e_log = ac + m_st
            m_row = jnp.maximum(e_log, d_max)
            w_inter = jnp.exp(e_log - m_row)
            w_intra = jnp.exp(d_max - m_row)
            num = w_inter * _bdot_nt(qh, ct_ref[h]) + w_intra * sv
            den = (w_inter * jnp.sum(qh * n_ref[h:h + 1, :], axis=-1, keepdims=True) + w_intra * s_sum)
            hout = num / jnp.maximum(jnp.abs(den), jnp.exp(-m_row))
            m_new = jnp.maximum(a_tot + m_st, g_max)
            decay = jnp.exp(a_tot + m_st - m_new)
            inc_scale = jnp.exp(g_max - m_new)
            ct_ref[h] = decay * ct_ref[h] + inc_scale * c_inc
            n_ref[h:h + 1, :] = decay * n_ref[h:h + 1, :] + inc_scale * n_inc
            m_ref[h:h + 1, :] = jnp.broadcast_to(m_new, (1, LANES))

            hn = hout * lax.rsqrt(jnp.mean(hout * hout, axis=-1, keepdims=True) + LN_EPS)
            y_ref[r0:r0 + CHUNK, h * A_DV:(h + 1) * A_DV] = (
                hn * ng_ref[:, h * A_DV:(h + 1) * A_DV] * _sigmoid(oh))


def _mlstm_mix(z, gt, conv_w, gate_b, norm_g, tt=256):
    t = z.shape[0]
    nqk2 = 2 * A_HEADS * A_DK
    conv_p = jnp.zeros((8, nqk2), F32).at[0:4].set(conv_w.astype(F32))
    gb = gate_b.astype(F32)
    gb_row = jnp.zeros((1, LANES), F32).at[0, 0:2 * A_HEADS].set(gb)
    gb_col = jnp.zeros((16, LANES), F32).at[0:2 * A_HEADS, :].set(jnp.broadcast_to(gb[:, None], (2 * A_HEADS, LANES)))
    ng = norm_g.astype(F32).reshape(1, A_HEADS * A_DV)
    return pl.pallas_call(
        _mlstm_kernel, grid=(t // tt,),
        in_specs=[pl.BlockSpec((tt, A_PROJ_PAD), lambda i: (i, 0)),
                  pl.BlockSpec((tt // CHUNK, 16, CHUNK), lambda i: (i, 0, 0)),
                  _const_spec((8, nqk2)), _const_spec((1, LANES)), _const_spec((16, LANES)),
                  _const_spec((1, A_HEADS * A_DV))],
        out_specs=pl.BlockSpec((tt, A_HEADS * A_DV), lambda i: (i, 0)),
        out_shape=jax.ShapeDtypeStruct((t, A_HEADS * A_DV), F32),
        scratch_shapes=[pltpu.VMEM((tt + 8, nqk2), F32),
                        pltpu.VMEM((A_HEADS, A_DV, A_DK), F32),
                        pltpu.VMEM((8, A_DK), F32),
                        pltpu.VMEM((8, LANES), F32)],
        compiler_params=_cparams("arbitrary"), name="mlstm_mix")(z, gt, conv_p, gb_row, gb_col, ng)


def _store_lane_tiles(ref, x):
    for t in range(ref.shape[0]):
        ref[t] = x[:, t * LANES:(t + 1) * LANES]


def _interleaved(ref):
    return jnp.concatenate(
        [jnp.concatenate([ref[t, pl.ds(c, N_SLOTS, stride=SLOT), :] for t in range(ref.shape[0])], axis=1)
         for c in range(SLOT)], axis=0)


def _pair_factors(q, k, b, ref_of_slot):
    zero = jnp.zeros((SLOT, q.shape[1]), F32)
    factors = []
    for half in (1, 2, 4):
        qs, ks = [], []
        for u in range(N_SLOTS):
            r = ref_of_slot((u // (2 * half)) * 2 * half + half - 1)
            rows = slice(u * SLOT, (u + 1) * SLOT)
            if u % (2 * half) >= half:
                qs.append(q[rows] * jnp.exp(b[rows] - r))
                ks.append(zero)
            else:
                qs.append(zero)
                ks.append(k[rows] * jnp.exp(r - b[rows]))
        factors.append((jnp.concatenate(qs, axis=0), jnp.concatenate(ks, axis=0)))
    return factors


def _pair_masks():
    r = lax.broadcasted_iota(jnp.int32, (CHUNK, CHUNK), 0)
    c = lax.broadcasted_iota(jnp.int32, (CHUNK, CHUNK), 1)
    sr, sc = r >> 3, c >> 3
    same_sublane = (r & (SLOT - 1)) == (c & (SLOT - 1))
    natural, interleaved = [], []
    for shift, half in ((1, 1), (2, 2), (3, 4)):
        m = jnp.logical_and((sr >> shift) == (sc >> shift),
                            jnp.logical_and((sr & half) != 0, (sc & half) == 0))
        natural.append(m)
        interleaved.append(jnp.logical_and(m, same_sublane))
    return r == c, natural, interleaved


def _decay_attention_chunk(q, k, v, la, st_ref, qkb_ref, vo_ref, heads, dv):
    dk = LANES
    tri = _tri(CHUNK, True).astype(BF16)
    la_1 = la.astype(BF16)
    rest = la - la_1.astype(F32)
    la_2 = rest.astype(BF16)
    la_3 = (rest - la_2.astype(F32)).astype(BF16)
    b = (jnp.dot(tri, la_1, preferred_element_type=F32)
         + (jnp.dot(tri, la_2, preferred_element_type=F32) + jnp.dot(tri, la_3, preferred_element_type=F32)))
    b_last = b[CHUNK - 1:CHUNK, :]
    qe = q * jnp.exp(b)
    kd = k * jnp.exp(b_last - b)

    _store_lane_tiles(qkb_ref.at[0], q)
    _store_lane_tiles(qkb_ref.at[1], k)
    _store_lane_tiles(qkb_ref.at[2], b)
    _store_lane_tiles(vo_ref.at[0], v)
    q_i, k_i, b_i = _interleaved(qkb_ref.at[0]), _interleaved(qkb_ref.at[1]), _interleaved(qkb_ref.at[2])
    v_i = _interleaved(vo_ref.at[0])

    eye, masks_n, masks_i = _pair_masks()
    fact_n = _pair_factors(q, k, b, lambda s: b[s * SLOT + SLOT - 1:(s + 1) * SLOT, :])
    fact_i = _pair_factors(q_i, k_i, b_i, lambda s: b_i[s * SLOT:(s + 1) * SLOT, :])

    outs = []
    for h in range(heads):
        kc = slice(h * dk, (h + 1) * dk)
        vc = slice(h * dv, (h + 1) * dv)
        att = jnp.where(eye, _bdot_nt(q[:, kc], k[:, kc]), 0.0)
        for (qf, kf), m in zip(fact_n, masks_n):
            att = att + jnp.where(m, _bdot_nt(qf[:, kc], kf[:, kc]), 0.0)
        att_i = jnp.zeros((CHUNK, CHUNK), F32)
        for (qf, kf), m in zip(fact_i, masks_i):
            att_i = att_i + jnp.where(m, _bdot_nt(qf[:, kc], kf[:, kc]), 0.0)
        outs.append(_bdot_nt(qe[:, kc], st_ref[h]) + _bdot(att, v[:, vc]))
        _store_lane_tiles(vo_ref.at[1, h * dv // LANES:(h + 1) * dv // LANES], _bdot(att_i, v_i[:, vc]))
        st_ref[h] = st_ref[h] * jnp.exp(b_last[:, kc]) + _bdot_tn(v[:, vc], kd[:, kc])
    o_i = _interleaved(vo_ref.at[1])
    return [o + o_i[:, h * dv:(h + 1) * dv] for h, o in enumerate(outs)]


def _head_norm_store(outs, y_ref, r0, ng_ref, gate_act, dv):
    for h, o in enumerate(outs):
        hn = o * lax.rsqrt(jnp.mean(o * o, axis=-1, keepdims=True) + LN_EPS)
        y_ref[r0:r0 + CHUNK, h * dv:(h + 1) * dv] = (
            hn * ng_ref[:, h * dv:(h + 1) * dv] * gate_act[:, h * dv:(h + 1) * dv])


def _gla_kernel(z_ref, wa2_ref, ba_ref, ng_ref, y_ref, st_ref, qkb_ref, vo_ref):
    @pl.when(pl.program_id(0) == 0)
    def _():
        st_ref[...] = jnp.zeros(st_ref.shape, F32)

    nk, nv = B_HEADS * B_DK, B_HEADS * B_DV
    for c in range(z_ref.shape[0] // CHUNK):
        r0 = c * CHUNK
        q = z_ref[r0:r0 + CHUNK, 0:nk] * (B_DK ** -0.5)
        k = z_ref[r0:r0 + CHUNK, nk:2 * nk]
        v = z_ref[r0:r0 + CHUNK, 2 * nk:2 * nk + nv]
        g = z_ref[r0:r0 + CHUNK, 2 * nk + nv:2 * nk + 2 * nv]
        a_lr = z_ref[r0:r0 + CHUNK, B_MAIN:B_MAIN + LANES]
        la = _log_sigmoid(jnp.dot(a_lr.astype(BF16), wa2_ref[...], preferred_element_type=F32)
                          + ba_ref[...]) / B_TAU
        outs = _decay_attention_chunk(q, k, v, la, st_ref, qkb_ref.at[c], vo_ref.at[c], B_HEADS, B_DV)
        _head_norm_store(outs, y_ref, r0, ng_ref, _silu(g), B_DV)


def _gla_mix(z, w_a2, b_a, norm_g, tt=256):
    t = z.shape[0]
    nk, nv = B_HEADS * B_DK, B_HEADS * B_DV
    wa2 = jnp.zeros((LANES, nk), BF16).at[0:B_RANK].set(w_a2.astype(BF16))
    return pl.pallas_call(
        _gla_kernel, grid=(t // tt,),
        in_specs=[pl.BlockSpec((tt, B_PROJ_PAD), lambda i: (i, 0)),
                  _const_spec((LANES, nk)), _const_spec((1, nk)), _const_spec((1, nv))],
        out_specs=pl.BlockSpec((tt, nv), lambda i: (i, 0)),
        out_shape=jax.ShapeDtypeStruct((t, nv), F32),
        scratch_shapes=[pltpu.VMEM((B_HEADS, B_DV, B_DK), F32),
                        pltpu.VMEM((tt // CHUNK, 3, nk // LANES, CHUNK, LANES), F32),
                        pltpu.VMEM((tt // CHUNK, 2, nv // LANES, CHUNK, LANES), F32)],
        compiler_params=_cparams("arbitrary"), name="gla_mix")(
            z, wa2, b_a.astype(F32).reshape(1, nk), norm_g.astype(F32).reshape(1, nv))


def _hgrn_kernel(layer, z_ref, lbp_ref, ng_ref, y_ref, st_ref, qkb_ref, vo_ref):
    @pl.when(pl.program_id(0) == 0)
    def _():
        st_ref[...] = jnp.zeros(st_ref.shape, F32)

    lbp = lbp_ref[0:DEPTH, :]
    e = jnp.exp(lbp - jnp.max(lbp, axis=0, keepdims=True))
    sm = e / jnp.sum(e, axis=0, keepdims=True)
    lb = jnp.zeros((1, sm.shape[1]), F32)
    for r in range(layer + 1):
        lb = lb + sm[r:r + 1, :]
    lb = lb - sm[0:1, :]

    nk, nv = C_HEADS * C_DK, C_HEADS * C_DV
    for c in range(z_ref.shape[0] // CHUNK):
        r0 = c * CHUNK
        q = _silu(z_ref[r0:r0 + CHUNK, 0:nk])
        fg = lb + (1.0 - lb) * _sigmoid(z_ref[r0:r0 + CHUNK, nk:2 * nk])
        v = z_ref[r0:r0 + CHUNK, 2 * nk:2 * nk + nv]
        g = z_ref[r0:r0 + CHUNK, 2 * nk + nv:2 * nk + 2 * nv]
        outs = _decay_attention_chunk(q, 1.0 - fg, v, jnp.log(fg), st_ref, qkb_ref.at[c], vo_ref.at[c],
                                      C_HEADS, C_DV)
        _head_norm_store(outs, y_ref, r0, ng_ref, _sigmoid(g), C_DV)


def _hgrn_mix(z, hgrn_lb, layer, norm_g, tt=256):
    t = z.shape[0]
    nk, nv = C_HEADS * C_DK, C_HEADS * C_DV
    lbp = jnp.zeros((8, nk), F32).at[0:DEPTH].set(hgrn_lb.astype(F32))
    return pl.pallas_call(
        functools.partial(_hgrn_kernel, layer), grid=(t // tt,),
        in_specs=[pl.BlockSpec((tt, C_PROJ), lambda i: (i, 0)),
                  _const_spec((8, nk)), _const_spec((1, nv))],
        out_specs=pl.BlockSpec((tt, nv), lambda i: (i, 0)),
        out_shape=jax.ShapeDtypeStruct((t, nv), F32),
        scratch_shapes=[pltpu.VMEM((C_HEADS, C_DV, C_DK), F32),
                        pltpu.VMEM((tt // CHUNK, 3, nk // LANES, CHUNK, LANES), F32),
                        pltpu.VMEM((tt // CHUNK, 2, nv // LANES, CHUNK, LANES), F32)],
        compiler_params=_cparams("arbitrary"), name="hgrn_mix")(
            z, lbp, norm_g.astype(F32).reshape(1, nv))


def _route(lt, rb_ref):
    rows = [lt[e:e + 1, :] for e in range(N_EXPERTS)]
    mx = rows[0]
    for r in rows[1:]:
        mx = jnp.maximum(mx, r)
    ex = [jnp.exp(r - mx) for r in rows]
    tot = ex[0]
    for r in ex[1:]:
        tot = tot + r
    scores = [r / tot for r in ex]
    sel = [scores[e] + rb_ref[e:e + 1, 0:1] for e in range(N_EXPERTS)]

    gscore = []
    for g in range(N_GROUPS):
        m = sel[g * EXPERTS_PER_GROUP:(g + 1) * EXPERTS_PER_GROUP]
        best = None
        for a in range(EXPERTS_PER_GROUP):
            for b2 in range(a + 1, EXPERTS_PER_GROUP):
                pair = m[a] + m[b2]
                best = pair if best is None else jnp.maximum(best, pair)
        gscore.append(best)
    g_best = gscore[0]
    g_idx = jnp.zeros_like(g_best, dtype=jnp.int32)
    for g in range(1, N_GROUPS):
        better = gscore[g] > g_best
        g_best = jnp.where(better, gscore[g], g_best)
        g_idx = jnp.where(better, g, g_idx)

    masked = [jnp.where(g_idx == (e // EXPERTS_PER_GROUP), sel[e], -jnp.inf) for e in range(N_EXPERTS)]

    def first_argmax(vals):
        best_v, best_i = vals[0], jnp.zeros_like(g_idx)
        for e in range(1, N_EXPERTS):
            better = vals[e] > best_v
            best_v = jnp.where(better, vals[e], best_v)
            best_i = jnp.where(better, e, best_i)
        return best_i

    i1 = first_argmax(masked)
    i2 = first_argmax([jnp.where(i1 == e, -jnp.inf, masked[e]) for e in range(N_EXPERTS)])
    w1 = jnp.zeros_like(mx)
    w2 = jnp.zeros_like(mx)
    for e in range(N_EXPERTS):
        w1 = jnp.where(i1 == e, scores[e], w1)
        w2 = jnp.where(i2 == e, scores[e], w2)
    wsum = w1 + w2
    return i1, i2, w1 / wsum, w2 / wsum


def _outproj_kernel(y_ref, x_ref, w_ref, g_ref, b_ref, rwh_ref, rwl_ref, rb_ref,
                    x1g_ref, meta_ref, cnt_ref, gt_scr, carry_ref):
    tm = x_ref.shape[0]

    @pl.when(pl.program_id(0) == 0)
    def _():
        carry_ref[...] = jnp.zeros(carry_ref.shape, F32)

    mix = jnp.dot(y_ref[...].astype(BF16), w_ref[...], preferred_element_type=F32)
    x1 = _layer_norm(DEEPNORM_ALPHA * x_ref[...] + mix, g_ref[...], b_ref[...])
    x1g_ref[:, 0:D_MODEL] = x1
    x_hi = x1.astype(BF16)
    x_lo = (x1 - x_hi.astype(F32)).astype(BF16)
    logits = (jnp.dot(x_hi, rwh_ref[...], preferred_element_type=F32)
              + (jnp.dot(x_lo, rwh_ref[...], preferred_element_type=F32)
                 + jnp.dot(x_hi, rwl_ref[...], preferred_element_type=F32)))
    i1, i2, w1, w2 = _route(logits.T, rb_ref)

    lo = jnp.minimum(i1, i2)
    hi = jnp.maximum(i1, i2)
    a = lo & (EXPERTS_PER_GROUP - 1)
    b2 = hi & (EXPERTS_PER_GROUP - 1)
    bucket = (lo >> 2) * PAIRS_PER_GROUP + ((a * (7 - a)) >> 1) + (b2 - a - 1)

    onehot = lax.broadcasted_iota(jnp.int32, (BUCKET_ROWS, tm), 0) == bucket
    r = lax.broadcasted_iota(jnp.int32, (tm, tm), 0)
    c = lax.broadcasted_iota(jnp.int32, (tm, tm), 1)
    before = jnp.dot(onehot.astype(BF16), (r < c).astype(BF16), preferred_element_type=F32)
    rank = jnp.sum(jnp.where(onehot, before + carry_ref[:, 0:1], 0.0), axis=0, keepdims=True)
    carry_ref[...] = carry_ref[...] + jnp.sum(onehot.astype(F32), axis=1, keepdims=True)
    cnt_ref[...] = carry_ref[...].astype(jnp.int32)

    meta_ref[...] = jnp.zeros(meta_ref.shape, jnp.int32)
    meta_ref[0:1, :] = bucket
    meta_ref[1:2, :] = rank.astype(jnp.int32)

    gt_scr[...] = jnp.zeros(gt_scr.shape, F32)
    gt_scr[0:1, :] = jnp.where(i1 == lo, w1, w2)
    gt_scr[1:2, :] = jnp.where(i1 == lo, w2, w1)
    x1g_ref[:, D_MODEL:] = gt_scr[...].T


def _outproj_ln_route(y, x, w_out, ln_g, ln_b, router_w, router_b, tm=512):
    t, d = x.shape
    rw = jnp.zeros((d, LANES), F32).at[:, 0:N_EXPERTS].set(router_w.astype(F32))
    rw_hi = rw.astype(BF16)
    rw_lo = (rw - rw_hi.astype(F32)).astype(BF16)
    rb = jnp.broadcast_to(router_b.astype(F32)[:, None], (N_EXPERTS, LANES))
    return pl.pallas_call(
        _outproj_kernel, grid=(t // tm,),
        in_specs=[pl.BlockSpec((tm, y.shape[1]), lambda i: (i, 0)), pl.BlockSpec((tm, d), lambda i: (i, 0)),
                  _const_spec(w_out.shape), _const_spec((1, d)), _const_spec((1, d)),
                  _const_spec((d, LANES)), _const_spec((d, LANES)), _const_spec((N_EXPERTS, LANES))],
        out_specs=[pl.BlockSpec((tm, d + LANES), lambda i: (i, 0)),
                   pl.BlockSpec((8, tm), lambda i: (0, i)), _const_spec((BUCKET_ROWS, LANES))],
        out_shape=[jax.ShapeDtypeStruct((t, d + LANES), F32),
                   jax.ShapeDtypeStruct((8, t), jnp.int32),
                   jax.ShapeDtypeStruct((BUCKET_ROWS, LANES), jnp.int32)],
        scratch_shapes=[pltpu.VMEM((LANES, tm), F32), pltpu.VMEM((BUCKET_ROWS, LANES), F32)],
        compiler_params=_cparams("arbitrary"), name="outproj_ln_route")(
            y, x, w_out, ln_g.astype(F32).reshape(1, d), ln_b.astype(F32).reshape(1, d), rw_hi, rw_lo, rb)


def _moe_plan(meta, cnt, tile_rows, n_tiles):
    counts = cnt[0:N_BUCKETS, 0]
    tiles = (counts + tile_rows - 1) // tile_rows
    tile_end = jnp.cumsum(tiles)
    offs = (tile_end - tiles) * tile_rows
    dest = (offs[meta[0]] + meta[1]).astype(jnp.int32)
    n_used = tile_end[-1]
    j = jnp.minimum(jnp.arange(n_tiles, dtype=jnp.int32), n_used - 1)
    bucket_of_tile = jnp.minimum(jnp.sum(tile_end[None, :] <= j[:, None], axis=1), N_BUCKETS - 1)
    pairs = [(a, b) for a in range(EXPERTS_PER_GROUP) for b in range(a + 1, EXPERTS_PER_GROUP)]
    lo_tab = jnp.array([g * EXPERTS_PER_GROUP + a for g in range(N_GROUPS) for a, _ in pairs], jnp.int32)
    hi_tab = jnp.array([g * EXPERTS_PER_GROUP + b for g in range(N_GROUPS) for _, b in pairs], jnp.int32)
    return dest, lo_tab[bucket_of_tile], hi_tab[bucket_of_tile], n_used.astype(jnp.int32).reshape(1)


def _rows_to_tiles(ref, x, row_tiles, first=0):
    n = x.shape[0]
    for c in range(x.shape[1] // LANES):
        ref[pl.ds(first + c, n, stride=row_tiles), :] = x[:, c * LANES:(c + 1) * LANES]


def _tiles_to_rows(ref, n, row_tiles, first=0, count=D_MODEL // LANES):
    return jnp.concatenate([ref[pl.ds(first + c, n, stride=row_tiles), :] for c in range(count)], axis=1)


def _dispatch_kernel(dest_ref, x1g_ref, xs_in_ref, xs_ref, comb_ref, sem):
    del xs_in_ref
    tm = x1g_ref.shape[0]
    base = pl.program_id(0) * tm
    _rows_to_tiles(comb_ref, x1g_ref[:, 0:D_MODEL], X_ROW_TILES)

    def row_copy(r):
        src = pl.multiple_of(r * X_ROW_TILES, X_ROW_TILES)
        dst = pl.multiple_of(dest_ref[base + r] * X_ROW_TILES, X_ROW_TILES)
        return pltpu.make_async_copy(comb_ref.at[pl.ds(src, X_ROW_TILES)], xs_ref.at[pl.ds(dst, X_ROW_TILES)], sem)

    def issue(grp, carry):
        for u in range(DMA_ISSUE_UNROLL):
            row_copy(grp * DMA_ISSUE_UNROLL + u).start(priority=u % 2)
        return carry

    lax.fori_loop(0, tm // DMA_ISSUE_UNROLL, issue, 0)
    pltpu.make_async_copy(comb_ref, xs_ref.at[pl.ds(0, tm * X_ROW_TILES)], sem).wait()


def _dispatch(dest, x1g, xs, tm=2048):
    t = x1g.shape[0]
    return pl.pallas_call(
        _dispatch_kernel,
        grid_spec=pltpu.PrefetchScalarGridSpec(
            num_scalar_prefetch=1, grid=(t // tm,),
            in_specs=[pl.BlockSpec((tm, x1g.shape[1]), lambda i, dest: (i, 0)),
                      pl.BlockSpec(memory_space=pl.ANY)],
            out_specs=pl.BlockSpec(memory_space=pl.ANY),
            scratch_shapes=[pltpu.VMEM((tm * X_ROW_TILES, LANES), F32), pltpu.SemaphoreType.DMA(())]),
        out_shape=jax.ShapeDtypeStruct(xs.shape, F32), input_output_aliases={2: 0},
        compiler_params=_cparams("arbitrary"), name="moe_dispatch")(dest, x1g, xs)


def _expert(x, wg_ref, wu_ref, wd_ref):
    hg = jnp.dot(x, wg_ref[0, 0], preferred_element_type=F32)
    hu = jnp.dot(x, wu_ref[0, 0], preferred_element_type=F32)
    return jnp.dot((_silu(hg) * hu).astype(BF16), wd_ref[0, 0], preferred_element_type=F32)


def _experts_kernel(lo_ref, hi_ref, nused_ref, xs_ref, wg0, wu0, wd0, wg1, wu1, wd1, ys_ref):
    del lo_ref, hi_ref
    tm = xs_ref.shape[0] // X_ROW_TILES

    @pl.when(pl.program_id(0) < nused_ref[0])
    def _():
        x = _tiles_to_rows(xs_ref, tm, X_ROW_TILES).astype(BF16)
        _rows_to_tiles(ys_ref, _expert(x, wg0, wu0, wd0), Y_ROW_TILES)
        _rows_to_tiles(ys_ref, _expert(x, wg1, wu1, wd1), Y_ROW_TILES, first=X_ROW_TILES)

    @pl.when(pl.program_id(0) >= nused_ref[0])
    def _():
        ys_ref[...] = jnp.zeros(ys_ref.shape, F32)


def _experts(layer, tile_lo, tile_hi, n_used, xs, wg, wu, wd, tm):
    d = D_MODEL
    n_tiles = xs.shape[0] // (tm * X_ROW_TILES)
    used = lambda j, n: jnp.maximum(jnp.minimum(j, n[0] - 1), 0)
    w_in = lambda which: pl.BlockSpec(
        (1, 1, d, D_EXPERT), lambda j, lo, hi, n: (layer, (lo, hi)[which][used(j, n)], 0, 0))
    w_out = lambda which: pl.BlockSpec(
        (1, 1, D_EXPERT, d), lambda j, lo, hi, n: (layer, (lo, hi)[which][used(j, n)], 0, 0))
    return pl.pallas_call(
        _experts_kernel,
        grid_spec=pltpu.PrefetchScalarGridSpec(
            num_scalar_prefetch=3, grid=(n_tiles,),
            in_specs=[pl.BlockSpec((tm * X_ROW_TILES, LANES), lambda j, lo, hi, n: (used(j, n), 0)),
                      w_in(0), w_in(0), w_out(0), w_in(1), w_in(1), w_out(1)],
            out_specs=pl.BlockSpec((tm * Y_ROW_TILES, LANES), lambda j, lo, hi, n: (j, 0))),
        out_shape=jax.ShapeDtypeStruct((n_tiles * tm * Y_ROW_TILES, LANES), F32),
        compiler_params=_cparams("arbitrary"), name="moe_experts")(
            tile_lo, tile_hi, n_used, xs, wg, wu, wd, wg, wu, wd)


def _combine_kernel(dest_ref, x1g_ref, p_ref, ys_ref, g_ref, b_ref, wpg_ref, wpp_ref, o_ref, ybuf, sems):
    i = pl.program_id(0)
    tm = x1g_ref.shape[0]
    slot = i % 2

    def row_copy(tile, s, r):
        src = pl.multiple_of(dest_ref[tile * tm + r] * Y_ROW_TILES, Y_ROW_TILES)
        dst = pl.multiple_of(r * Y_ROW_TILES, Y_ROW_TILES)
        return pltpu.make_async_copy(ys_ref.at[pl.ds(src, Y_ROW_TILES)],
                                     ybuf.at[s].at[pl.ds(dst, Y_ROW_TILES)], sems.at[s])

    def start_tile(tile, s):
        def issue(grp, carry):
            for u in range(DMA_ISSUE_UNROLL):
                row_copy(tile, s, grp * DMA_ISSUE_UNROLL + u).start(priority=u % 2)
            return carry
        lax.fori_loop(0, tm // DMA_ISSUE_UNROLL, issue, 0)

    @pl.when(i == 0)
    def _():
        start_tile(0, 0)

    @pl.when(i + 1 < pl.num_programs(0))
    def _():
        start_tile(i + 1, 1 - slot)

    pltpu.make_async_copy(ys_ref.at[pl.ds(0, tm * Y_ROW_TILES)], ybuf.at[slot], sems.at[slot]).wait()
    g = x1g_ref[:, D_MODEL:]
    y = (g[:, 0:1] * _tiles_to_rows(ybuf.at[slot], tm, Y_ROW_TILES)
         + g[:, 1:2] * _tiles_to_rows(ybuf.at[slot], tm, Y_ROW_TILES, first=X_ROW_TILES))
    x2 = _layer_norm(DEEPNORM_ALPHA * x1g_ref[:, 0:D_MODEL] + y, g_ref[...], b_ref[...])
    gate = _sigmoid(jnp.dot(x2.astype(BF16), wpg_ref[...], preferred_element_type=F32))
    o_ref[...] = x2 + gate * jnp.dot(p_ref[...].astype(BF16), wpp_ref[...], preferred_element_type=F32)


def _combine_ln_ple(layer, dest, x1g, p, ys, ln_g, ln_b, w_gate, w_proj, tm=512):
    t = x1g.shape[0]
    d = D_MODEL
    const = lambda shape: pl.BlockSpec(shape, lambda i, dest: (0,) * len(shape))
    return pl.pallas_call(
        _combine_kernel,
        grid_spec=pltpu.PrefetchScalarGridSpec(
            num_scalar_prefetch=1, grid=(t // tm,),
            in_specs=[pl.BlockSpec((tm, x1g.shape[1]), lambda i, dest: (i, 0)),
                      pl.BlockSpec((tm, PLE_DIM), lambda i, dest: (layer * (t // tm) + i, 0)),
                      pl.BlockSpec(memory_space=pl.ANY), const((1, d)), const((1, d)),
                      const((d, d)), const((PLE_DIM, d))],
            out_specs=pl.BlockSpec((tm, d), lambda i, dest: (i, 0)),
            scratch_shapes=[pltpu.VMEM((2, tm * Y_ROW_TILES, LANES), F32), pltpu.SemaphoreType.DMA((2,))]),
        out_shape=jax.ShapeDtypeStruct((t, d), F32),
        compiler_params=_cparams("arbitrary"), name="moe_combine_ln_ple")(
            dest, x1g, p, ys, ln_g.astype(F32).reshape(1, d), ln_b.astype(F32).reshape(1, d), w_gate, w_proj)


def _pad_cols(w, n):
    return jnp.zeros((w.shape[0], n), w.dtype).at[:, 0:w.shape[1]].set(w)


def kernel(x, p, ln1_g, ln1_b, ln2_g, ln2_b, mlstm_w_in, mlstm_conv, mlstm_gate_b, mlstm_norm_g, mlstm_w_out, gla_w_in, gla_w_a2, gla_b_a, gla_norm_g, gla_w_out, hgrn_w_in, hgrn_lb, hgrn_norm_g, hgrn_w_out, router_w, router_b, exp_w_gate, exp_w_up, exp_w_down, ple_w_proj, ple_w_gate):
    bsz, seq, d = x.shape
    assert bsz == 1 and d == D_MODEL and seq % 2048 == 0
    xt = x.reshape(seq, d).astype(F32)
    n_tiles = seq // MOE_TILE + N_BUCKETS
    xs = jnp.zeros((n_tiles * MOE_TILE * X_ROW_TILES, LANES), F32)
    wg_all, wu_all, wd_all = exp_w_gate.astype(BF16), exp_w_up.astype(BF16), exp_w_down.astype(BF16)
    for i in range(DEPTH):
        kind, slot = i % N_MIXERS, i // N_MIXERS
        if kind == 0:
            w_in = mlstm_w_in[slot]
            w = _pad_cols(w_in.astype(BF16), A_PROJ_PAD)
            wgt = jnp.zeros((16, d), BF16).at[0:2 * A_HEADS].set(w_in[:, A_MAIN:].T.astype(BF16))
            z, gt = _inproj(xt, w, wgt)
            y = _mlstm_mix(z, gt, mlstm_conv[slot], mlstm_gate_b[slot], mlstm_norm_g[slot])
            w_out = mlstm_w_out[slot]
        elif kind == 1:
            z = _inproj(xt, _pad_cols(gla_w_in[slot].astype(BF16), B_PROJ_PAD))
            y = _gla_mix(z, gla_w_a2[slot], gla_b_a[slot], gla_norm_g[slot])
            w_out = gla_w_out[slot]
        else:
            z = _inproj(xt, hgrn_w_in[slot].astype(BF16))
            y = _hgrn_mix(z, hgrn_lb, i, hgrn_norm_g[slot])
            w_out = hgrn_w_out[slot]
        x1g, meta, cnt = _outproj_ln_route(y, xt, w_out.astype(BF16), ln1_g[i], ln1_b[i], router_w, router_b)
        dest, tile_lo, tile_hi, n_used = _moe_plan(meta, cnt, MOE_TILE, n_tiles)
        xs = _dispatch(dest, x1g, xs)
        ys = _experts(i, tile_lo, tile_hi, n_used, xs, wg_all, wu_all, wd_all, MOE_TILE)
        xt = _combine_ln_ple(i, dest, x1g, p.reshape(DEPTH * seq, PLE_DIM), ys, ln2_g[i], ln2_b[i],
                             ple_w_gate[i].astype(BF16), ple_w_proj[i].astype(BF16))
    return xt.reshape(bsz, seq, d)
```

```python
import functools

import jax
import jax.numpy as jnp
from jax import lax
from jax.experimental import pallas as pl
from jax.experimental.pallas import tpu as pltpu

F32 = jnp.float32
BF16 = jnp.bfloat16

D_MODEL = 1024
DEPTH = 4
CHUNK = 64
PLE_DIM = 256
N_MIXERS = 3
LN_EPS = 1e-5
DEEPNORM_ALPHA = (2 * DEPTH) ** 0.25
A_HEADS, A_DK, A_DV = 4, 128, 256
B_HEADS, B_DK, B_DV = 4, 128, 256
B_RANK = 16
B_TAU = 16.0
C_HEADS, C_DK, C_DV = 8, 128, 128
N_EXPERTS = 16
N_GROUPS = 4
EXPERTS_PER_GROUP = 4
D_EXPERT = 512
PAIRS_PER_GROUP = EXPERTS_PER_GROUP * (EXPERTS_PER_GROUP - 1) // 2
N_BUCKETS = N_GROUPS * PAIRS_PER_GROUP
BUCKET_ROWS = 32
MOE_TILE = 256
DMA_ISSUE_UNROLL = 64

LANES = 128
SLOT = 8
X_ROW_TILES = D_MODEL // LANES
Y_ROW_TILES = 2 * X_ROW_TILES
N_SLOTS = CHUNK // SLOT
A_MAIN = 2 * A_HEADS * A_DK + 2 * A_HEADS * A_DV
A_PROJ_PAD = A_MAIN + LANES
B_MAIN = 2 * B_HEADS * B_DK + 2 * B_HEADS * B_DV
B_PROJ_PAD = B_MAIN + LANES
C_PROJ = 2 * C_HEADS * C_DK + 2 * C_HEADS * C_DV
VMEM_LIMIT = 56 * 1024 * 1024

NT_DIMS = (((1,), (1,)), ((), ()))
TN_DIMS = (((0,), (0,)), ((), ()))


def _cparams(*sem):
    return pltpu.CompilerParams(dimension_semantics=sem, vmem_limit_bytes=VMEM_LIMIT)


def _sigmoid(x):
    return 0.5 * jnp.tanh(0.5 * x) + 0.5


def _silu(x):
    return x * _sigmoid(x)


def _log_sigmoid(x):
    return jnp.minimum(x, 0.0) - jnp.log(1.0 + jnp.exp(-jnp.abs(x)))


def _layer_norm(u, g, b):
    mu = jnp.mean(u, axis=-1, keepdims=True)
    d = u - mu
    var = jnp.mean(d * d, axis=-1, keepdims=True)
    return d * lax.rsqrt(var + LN_EPS) * g + b


def _tri(n, lower):
    r = lax.broadcasted_iota(jnp.int32, (n, n), 0)
    c = lax.broadcasted_iota(jnp.int32, (n, n), 1)
    return (r >= c) if lower else (r <= c)


def _bdot(a, b):
    return jnp.dot(a.astype(BF16), b.astype(BF16), preferred_element_type=F32)


def _bdot_nt(a, b):
    return lax.dot_general(a.astype(BF16), b.astype(BF16), NT_DIMS, preferred_element_type=F32)


def _bdot_tn(a, b):
    return lax.dot_general(a.astype(BF16), b.astype(BF16), TN_DIMS, preferred_element_type=F32)


def _const_spec(shape):
    return pl.BlockSpec(shape, lambda i: (0,) * len(shape))


def _inproj_kernel(x_ref, w_ref, z_ref):
    z_ref[...] = jnp.dot(x_ref[...].astype(BF16), w_ref[...], preferred_element_type=F32)


def _inproj_gates_kernel(x_ref, w_ref, wgt_ref, z_ref, gt_ref):
    xb = x_ref[...].astype(BF16)
    z_ref[...] = jnp.dot(xb, w_ref[...], preferred_element_type=F32)
    for c in range(gt_ref.shape[0]):
        gt_ref[c] = lax.dot_general(wgt_ref[...], xb[c * CHUNK:(c + 1) * CHUNK], NT_DIMS,
                                    preferred_element_type=F32)


def _inproj(x, w, wgt=None, tm=512):
    t, d = x.shape
    n = w.shape[1]
    grid = (t // tm,)
    x_spec = pl.BlockSpec((tm, d), lambda i: (i, 0))
    z_spec = pl.BlockSpec((tm, n), lambda i: (i, 0))
    if wgt is None:
        return pl.pallas_call(
            _inproj_kernel, grid=grid, in_specs=[x_spec, _const_spec((d, n))], out_specs=z_spec,
            out_shape=jax.ShapeDtypeStruct((t, n), F32), compiler_params=_cparams("parallel"),
            name="inproj")(x, w)
    return pl.pallas_call(
        _inproj_gates_kernel, grid=grid,
        in_specs=[x_spec, _const_spec((d, n)), _const_spec(wgt.shape)],
        out_specs=[z_spec, pl.BlockSpec((tm // CHUNK, 16, CHUNK), lambda i: (i, 0, 0))],
        out_shape=[jax.ShapeDtypeStruct((t, n), F32),
                   jax.ShapeDtypeStruct((t // CHUNK, 16, CHUNK), F32)],
        compiler_params=_cparams("parallel"), name="inproj_gates")(x, w, wgt)


def _mlstm_kernel(z_ref, gt_ref, conv_ref, gbr_ref, gbc_ref, ng_ref, y_ref,
                  ext_ref, ct_ref, n_ref, m_ref):
    tt = z_ref.shape[0]
    nqk = A_HEADS * A_DK

    @pl.when(pl.program_id(0) == 0)
    def _():
        ext_ref[0:8, :] = jnp.zeros((8, 2 * nqk), F32)
        ct_ref[...] = jnp.zeros(ct_ref.shape, F32)
        n_ref[...] = jnp.zeros(n_ref.shape, F32)
        m_ref[...] = jnp.zeros(m_ref.shape, F32)

    qk_raw = z_ref[:, 0:2 * nqk]
    ext_ref[8:8 + tt, :] = qk_raw
    acc = conv_ref[3:4, :] * qk_raw
    for s in range(1, 4):
        acc = acc + conv_ref[3 - s:4 - s, :] * ext_ref[pl.ds(8 - s, tt), :]
    ext_ref[0:8, :] = qk_raw[tt - 8:tt, :]
    qkc = _silu(acc)

    tri_lo = _tri(CHUNK, True)
    tri_lo_f = tri_lo.astype(F32)
    tri_up_f = _tri(CHUNK, False).astype(F32)

    for c in range(tt // CHUNK):
        r0 = c * CHUNK
        gc = z_ref[r0:r0 + CHUNK, A_MAIN:A_MAIN + LANES] + gbr_ref[...]
        a_col = jnp.dot(tri_lo_f, _log_sigmoid(gc), preferred_element_type=F32,
                        precision=lax.Precision.HIGHEST)
        gr = gt_ref[c] + gbc_ref[:, 0:1]
        a_row = jnp.dot(_log_sigmoid(gr), tri_up_f, preferred_element_type=F32,
                        precision=lax.Precision.HIGHEST)
        for h in range(A_HEADS):
            ac = a_col[:, A_HEADS + h:A_HEADS + h + 1]
            lic = gc[:, h:h + 1]
            ar = a_row[A_HEADS + h:A_HEADS + h + 1, :]
            lir = gr[h:h + 1, :]
            m_st = m_ref[h:h + 1, 0:1]
            qh = qkc[r0:r0 + CHUNK, h * A_DK:(h + 1) * A_DK]
            kh = qkc[r0:r0 + CHUNK, nqk + h * A_DK:nqk + (h + 1) * A_DK] * (A_DK ** -0.5)
            vh = z_ref[r0:r0 + CHUNK, 2 * nqk + h * A_DV:2 * nqk + (h + 1) * A_DV]
            oh = z_ref[r0:r0 + CHUNK, 2 * nqk + A_HEADS * A_DV + h * A_DV:
                       2 * nqk + A_HEADS * A_DV + (h + 1) * A_DV]

            d_log = jnp.where(tri_lo, ac - ar + lir, -jnp.inf)
            d_max = jnp.max(d_log, axis=-1, keepdims=True)
            s = _bdot_nt(qh, kh) * jnp.exp(d_log - d_max)
            sv = _bdot(s, vh)
            s_sum = jnp.sum(s, axis=-1, keepdims=True)
            a_tot = ar[:, CHUNK - 1:CHUNK]
            g_max = jnp.max(a_tot - ar + lir, axis=-1, keepdims=True)
            wk = jnp.exp(a_tot - ac + lic - g_max) * kh
            c_inc = _bdot_tn(vh, wk)
            n_inc = jnp.sum(wk, axis=0, keepdims=True)

            e_log = ac + m_st
            m_row = jnp.maximum(e_log, d_max)
            w_inter = jnp.exp(e_log - m_row)
            w_intra = jnp.exp(d_max - m_row)
            num = w_inter * _bdot_nt(qh, ct_ref[h]) + w_intra * sv
            den = (w_inter * jnp.sum(qh * n_ref[h:h + 1, :], axis=-1, keepdims=True) + w_intra * s_sum)
            hout = num / jnp.maximum(jnp.abs(den), jnp.exp(-m_row))
            m_new = jnp.maximum(a_tot + m_st, g_max)
            decay = jnp.exp(a_tot + m_st - m_new)
            inc_scale = jnp.exp(g_max - m_new)
            ct_ref[h] = decay * ct_ref[h] + inc_scale * c_inc
            n_ref[h:h + 1, :] = decay * n_ref[h:h + 1, :] + inc_scale * n_inc
            m_ref[h:h + 1, :] = jnp.broadcast_to(m_new, (1, LANES))

            hn = hout * lax.rsqrt(jnp.mean(hout * hout, axis=-1, keepdims=True) + LN_EPS)
            y_ref[r0:r0 + CHUNK, h * A_DV:(h + 1) * A_DV] = (
                hn * ng_ref[:, h * A_DV:(h + 1) * A_DV] * _sigmoid(oh))


def _mlstm_mix(z, gt, conv_w, gate_b, norm_g, tt=256):
    t = z.shape[0]
    nqk2 = 2 * A_HEADS * A_DK
    conv_p = jnp.zeros((8, nqk2), F32).at[0:4].set(conv_w.astype(F32))
    gb = gate_b.astype(F32)
    gb_row = jnp.zeros((1, LANES), F32).at[0, 0:2 * A_HEADS].set(gb)
    gb_col = jnp.zeros((16, LANES), F32).at[0:2 * A_HEADS, :].set(jnp.broadcast_to(gb[:, None], (2 * A_HEADS, LANES)))
    ng = norm_g.astype(F32).reshape(1, A_HEADS * A_DV)
    return pl.pallas_call(
        _mlstm_kernel, grid=(t // tt,),
        in_specs=[pl.BlockSpec((tt, A_PROJ_PAD), lambda i: (i, 0)),
                  pl.BlockSpec((tt // CHUNK, 16, CHUNK), lambda i: (i, 0, 0)),
                  _const_spec((8, nqk2)), _const_spec((1, LANES)), _const_spec((16, LANES)),
                  _const_spec((1, A_HEADS * A_DV))],
        out_specs=pl.BlockSpec((tt, A_HEADS * A_DV), lambda i: (i, 0)),
        out_shape=jax.ShapeDtypeStruct((t, A_HEADS * A_DV), F32),
        scratch_shapes=[pltpu.VMEM((tt + 8, nqk2), F32),
                        pltpu.VMEM((A_HEADS, A_DV, A_DK), F32),
                        pltpu.VMEM((8, A_DK), F32),
                        pltpu.VMEM((8, LANES), F32)],
        compiler_params=_cparams("arbitrary"), name="mlstm_mix")(z, gt, conv_p, gb_row, gb_col, ng)


def _store_lane_tiles(ref, x):
    for t in range(ref.shape[0]):
        ref[t] = x[:, t * LANES:(t + 1) * LANES]


def _interleaved(ref):
    return jnp.concatenate(
        [jnp.concatenate([ref[t, pl.ds(c, N_SLOTS, stride=SLOT), :] for t in range(ref.shape[0])], axis=1)
         for c in range(SLOT)], axis=0)


def _pair_factors(q, k, b, ref_of_slot):
    zero = jnp.zeros((SLOT, q.shape[1]), F32)
    factors = []
    for half in (1, 2, 4):
        qs, ks = [], []
        for u in range(N_SLOTS):
            r = ref_of_slot((u // (2 * half)) * 2 * half + half - 1)
            rows = slice(u * SLOT, (u + 1) * SLOT)
            if u % (2 * half) >= half:
                qs.append(q[rows] * jnp.exp(b[rows] - r))
                ks.append(zero)
            else:
                qs.append(zero)
                ks.append(k[rows] * jnp.exp(r - b[rows]))
        factors.append((jnp.concatenate(qs, axis=0), jnp.concatenate(ks, axis=0)))
    return factors


def _pair_masks(copies=1):
    r = lax.broadcasted_iota(jnp.int32, (CHUNK, CHUNK * copies), 0)
    c = lax.broadcasted_iota(jnp.int32, (CHUNK, CHUNK * copies), 1) & (CHUNK - 1)
    sr, sc = r >> 3, c >> 3
    same_sublane = (r & (SLOT - 1)) == (c & (SLOT - 1))
    natural, interleaved = [], []
    for shift, half in ((1, 1), (2, 2), (3, 4)):
        m = jnp.logical_and((sr >> shift) == (sc >> shift),
                            jnp.logical_and((sr & half) != 0, (sc & half) == 0))
        natural.append(m)
        interleaved.append(jnp.logical_and(m, same_sublane))
    return r == c, natural, interleaved


def _decay_attention_chunk(q, k, v, la, st_ref, qkb_ref, vo_ref, heads, dv):
    dk = LANES
    tri = _tri(CHUNK, True).astype(BF16)
    la_1 = la.astype(BF16)
    rest = la - la_1.astype(F32)
    la_2 = rest.astype(BF16)
    la_3 = (rest - la_2.astype(F32)).astype(BF16)
    b = (jnp.dot(tri, la_1, preferred_element_type=F32)
         + (jnp.dot(tri, la_2, preferred_element_type=F32) + jnp.dot(tri, la_3, preferred_element_type=F32)))
    b_last = b[CHUNK - 1:CHUNK, :]
    qe = q * jnp.exp(b)
    kd = k * jnp.exp(b_last - b)

    _store_lane_tiles(qkb_ref.at[0], q)
    _store_lane_tiles(qkb_ref.at[1], k)
    _store_lane_tiles(qkb_ref.at[2], b)
    _store_lane_tiles(vo_ref.at[0], v)
    q_i, k_i, b_i = _interleaved(qkb_ref.at[0]), _interleaved(qkb_ref.at[1]), _interleaved(qkb_ref.at[2])
    v_i = _interleaved(vo_ref.at[0])

    fact_n = _pair_factors(q, k, b, lambda s: b[s * SLOT + SLOT - 1:(s + 1) * SLOT, :])
    fact_i = _pair_factors(q_i, k_i, b_i, lambda s: b_i[s * SLOT:(s + 1) * SLOT, :])

    def pair_scores(qf, kf, h):
        kc = slice(h * dk, (h + 2) * dk)
        zero = jnp.zeros((CHUNK, dk), F32)
        kb = kf[:, kc]
        k_diag = jnp.concatenate([jnp.concatenate([kb[:, 0:dk], zero], axis=1),
                                  jnp.concatenate([zero, kb[:, dk:]], axis=1)], axis=0)
        return _bdot_nt(qf[:, kc], k_diag)

    def pair_apply(att, vv, h):
        vc = slice(h * dv, (h + 2) * dv)
        zero = jnp.zeros((CHUNK, dv), F32)
        vb = vv[:, vc]
        v_diag = jnp.concatenate([jnp.concatenate([vb[:, 0:dv], zero], axis=1),
                                  jnp.concatenate([zero, vb[:, dv:]], axis=1)], axis=0)
        return _bdot(att, v_diag)

    eye2, masks_n2, masks_i2 = _pair_masks(2)
    outs = []
    for h in range(0, heads, 2):
        att = jnp.where(eye2, pair_scores(q, k, h), 0.0)
        for (qf, kf), m in zip(fact_n, masks_n2):
            att = att + jnp.where(m, pair_scores(qf, kf, h), 0.0)
        att_i = jnp.zeros((CHUNK, 2 * CHUNK), F32)
        for (qf, kf), m in zip(fact_i, masks_i2):
            att_i = att_i + jnp.where(m, pair_scores(qf, kf, h), 0.0)
        o_pair = pair_apply(att, v, h)
        _store_lane_tiles(vo_ref.at[1, h * dv // LANES:(h + 2) * dv // LANES], pair_apply(att_i, v_i, h))
        for g in (h, h + 1):
            kc = slice(g * dk, (g + 1) * dk)
            vc = slice(g * dv, (g + 1) * dv)
            outs.append(_bdot_nt(qe[:, kc], st_ref[g]) + o_pair[:, (g - h) * dv:(g - h + 1) * dv])
            st_ref[g] = st_ref[g] * jnp.exp(b_last[:, kc]) + _bdot_tn(v[:, vc], kd[:, kc])
    o_i = _interleaved(vo_ref.at[1])
    return [o + o_i[:, h * dv:(h + 1) * dv] for h, o in enumerate(outs)]


def _head_norm_store(outs, y_ref, r0, ng_ref, gate_act, dv):
    for h, o in enumerate(outs):
        hn = o * lax.rsqrt(jnp.mean(o * o, axis=-1, keepdims=True) + LN_EPS)
        y_ref[r0:r0 + CHUNK, h * dv:(h + 1) * dv] = (
            hn * ng_ref[:, h * dv:(h + 1) * dv] * gate_act[:, h * dv:(h + 1) * dv])


def _gla_kernel(z_ref, wa2_ref, ba_ref, ng_ref, y_ref, st_ref, qkb_ref, vo_ref):
    @pl.when(pl.program_id(0) == 0)
    def _():
        st_ref[...] = jnp.zeros(st_ref.shape, F32)

    nk, nv = B_HEADS * B_DK, B_HEADS * B_DV
    for c in range(z_ref.shape[0] // CHUNK):
        r0 = c * CHUNK
        q = z_ref[r0:r0 + CHUNK, 0:nk] * (B_DK ** -0.5)
        k = z_ref[r0:r0 + CHUNK, nk:2 * nk]
        v = z_ref[r0:r0 + CHUNK, 2 * nk:2 * nk + nv]
        g = z_ref[r0:r0 + CHUNK, 2 * nk + nv:2 * nk + 2 * nv]
        a_lr = z_ref[r0:r0 + CHUNK, B_MAIN:B_MAIN + LANES]
        la = _log_sigmoid(jnp.dot(a_lr.astype(BF16), wa2_ref[...], preferred_element_type=F32)
                          + ba_ref[...]) / B_TAU
        outs = _decay_attention_chunk(q, k, v, la, st_ref, qkb_ref.at[c], vo_ref.at[c], B_HEADS, B_DV)
        _head_norm_store(outs, y_ref, r0, ng_ref, _silu(g), B_DV)


def _gla_mix(z, w_a2, b_a, norm_g, tt=256):
    t = z.shape[0]
    nk, nv = B_HEADS * B_DK, B_HEADS * B_DV
    wa2 = jnp.zeros((LANES, nk), BF16).at[0:B_RANK].set(w_a2.astype(BF16))
    return pl.pallas_call(
        _gla_kernel, grid=(t // tt,),
        in_specs=[pl.BlockSpec((tt, B_PROJ_PAD), lambda i: (i, 0)),
                  _const_spec((LANES, nk)), _const_spec((1, nk)), _const_spec((1, nv))],
        out_specs=pl.BlockSpec((tt, nv), lambda i: (i, 0)),
        out_shape=jax.ShapeDtypeStruct((t, nv), F32),
        scratch_shapes=[pltpu.VMEM((B_HEADS, B_DV, B_DK), F32),
                        pltpu.VMEM((tt // CHUNK, 3, nk // LANES, CHUNK, LANES), F32),
                        pltpu.VMEM((tt // CHUNK, 2, nv // LANES, CHUNK, LANES), F32)],
        compiler_params=_cparams("arbitrary"), name="gla_mix")(
            z, wa2, b_a.astype(F32).reshape(1, nk), norm_g.astype(F32).reshape(1, nv))


def _hgrn_kernel(layer, z_ref, lbp_ref, ng_ref, y_ref, st_ref, qkb_ref, vo_ref):
    @pl.when(pl.program_id(0) == 0)
    def _():
        st_ref[...] = jnp.zeros(st_ref.shape, F32)

    lbp = lbp_ref[0:DEPTH, :]
    e = jnp.exp(lbp - jnp.max(lbp, axis=0, keepdims=True))
    sm = e / jnp.sum(e, axis=0, keepdims=True)
    lb = jnp.zeros((1, sm.shape[1]), F32)
    for r in range(layer + 1):
        lb = lb + sm[r:r + 1, :]
    lb = lb - sm[0:1, :]

    nk, nv = C_HEADS * C_DK, C_HEADS * C_DV
    for c in range(z_ref.shape[0] // CHUNK):
        r0 = c * CHUNK
        q = _silu(z_ref[r0:r0 + CHUNK, 0:nk])
        fg = lb + (1.0 - lb) * _sigmoid(z_ref[r0:r0 + CHUNK, nk:2 * nk])
        v = z_ref[r0:r0 + CHUNK, 2 * nk:2 * nk + nv]
        g = z_ref[r0:r0 + CHUNK, 2 * nk + nv:2 * nk + 2 * nv]
        outs = _decay_attention_chunk(q, 1.0 - fg, v, jnp.log(fg), st_ref, qkb_ref.at[c], vo_ref.at[c],
                                      C_HEADS, C_DV)
        _head_norm_store(outs, y_ref, r0, ng_ref, _sigmoid(g), C_DV)


def _hgrn_mix(z, hgrn_lb, layer, norm_g, tt=256):
    t = z.shape[0]
    nk, nv = C_HEADS * C_DK, C_HEADS * C_DV
    lbp = jnp.zeros((8, nk), F32).at[0:DEPTH].set(hgrn_lb.astype(F32))
    return pl.pallas_call(
        functools.partial(_hgrn_kernel, layer), grid=(t // tt,),
        in_specs=[pl.BlockSpec((tt, C_PROJ), lambda i: (i, 0)),
                  _const_spec((8, nk)), _const_spec((1, nv))],
        out_specs=pl.BlockSpec((tt, nv), lambda i: (i, 0)),
        out_shape=jax.ShapeDtypeStruct((t, nv), F32),
        scratch_shapes=[pltpu.VMEM((C_HEADS, C_DV, C_DK), F32),
                        pltpu.VMEM((tt // CHUNK, 3, nk // LANES, CHUNK, LANES), F32),
                        pltpu.VMEM((tt // CHUNK, 2, nv // LANES, CHUNK, LANES), F32)],
        compiler_params=_cparams("arbitrary"), name="hgrn_mix")(
            z, lbp, norm_g.astype(F32).reshape(1, nv))


def _route(lt, rb_ref):
    rows = [lt[e:e + 1, :] for e in range(N_EXPERTS)]
    mx = rows[0]
    for r in rows[1:]:
        mx = jnp.maximum(mx, r)
    ex = [jnp.exp(r - mx) for r in rows]
    tot = ex[0]
    for r in ex[1:]:
        tot = tot + r
    scores = [r / tot for r in ex]
    sel = [scores[e] + rb_ref[e:e + 1, 0:1] for e in range(N_EXPERTS)]

    gscore = []
    for g in range(N_GROUPS):
        m = sel[g * EXPERTS_PER_GROUP:(g + 1) * EXPERTS_PER_GROUP]
        best = None
        for a in range(EXPERTS_PER_GROUP):
            for b2 in range(a + 1, EXPERTS_PER_GROUP):
                pair = m[a] + m[b2]
                best = pair if best is None else jnp.maximum(best, pair)
        gscore.append(best)
    g_best = gscore[0]
    g_idx = jnp.zeros_like(g_best, dtype=jnp.int32)
    for g in range(1, N_GROUPS):
        better = gscore[g] > g_best
        g_best = jnp.where(better, gscore[g], g_best)
        g_idx = jnp.where(better, g, g_idx)

    masked = [jnp.where(g_idx == (e // EXPERTS_PER_GROUP), sel[e], -jnp.inf) for e in range(N_EXPERTS)]

    def first_argmax(vals):
        best_v, best_i = vals[0], jnp.zeros_like(g_idx)
        for e in range(1, N_EXPERTS):
            better = vals[e] > best_v
            best_v = jnp.where(better, vals[e], best_v)
            best_i = jnp.where(better, e, best_i)
        return best_i

    i1 = first_argmax(masked)
    i2 = first_argmax([jnp.where(i1 == e, -jnp.inf, masked[e]) for e in range(N_EXPERTS)])
    w1 = jnp.zeros_like(mx)
    w2 = jnp.zeros_like(mx)
    for e in range(N_EXPERTS):
        w1 = jnp.where(i1 == e, scores[e], w1)
        w2 = jnp.where(i2 == e, scores[e], w2)
    wsum = w1 + w2
    return i1, i2, w1 / wsum, w2 / wsum


def _outproj_kernel(y_ref, x_ref, w_ref, g_ref, b_ref, rwh_ref, rwl_ref, rb_ref,
                    x1g_ref, meta_ref, cnt_ref, gt_scr, carry_ref):
    tm = x_ref.shape[0]

    @pl.when(pl.program_id(0) == 0)
    def _():
        carry_ref[...] = jnp.zeros(carry_ref.shape, F32)

    mix = jnp.dot(y_ref[...].astype(BF16), w_ref[...], preferred_element_type=F32)
    x1 = _layer_norm(DEEPNORM_ALPHA * x_ref[...] + mix, g_ref[...], b_ref[...])
    x1g_ref[:, 0:D_MODEL] = x1
    x_hi = x1.astype(BF16)
    x_lo = (x1 - x_hi.astype(F32)).astype(BF16)
    logits = (jnp.dot(x_hi, rwh_ref[...], preferred_element_type=F32)
              + (jnp.dot(x_lo, rwh_ref[...], preferred_element_type=F32)
                 + jnp.dot(x_hi, rwl_ref[...], preferred_element_type=F32)))
    i1, i2, w1, w2 = _route(logits.T, rb_ref)

    lo = jnp.minimum(i1, i2)
    hi = jnp.maximum(i1, i2)
    a = lo & (EXPERTS_PER_GROUP - 1)
    b2 = hi & (EXPERTS_PER_GROUP - 1)
    bucket = (lo >> 2) * PAIRS_PER_GROUP + ((a * (7 - a)) >> 1) + (b2 - a - 1)

    onehot = lax.broadcasted_iota(jnp.int32, (BUCKET_ROWS, tm), 0) == bucket
    r = lax.broadcasted_iota(jnp.int32, (tm, tm), 0)
    c = lax.broadcasted_iota(jnp.int32, (tm, tm), 1)
    before = jnp.dot(onehot.astype(BF16), (r < c).astype(BF16), preferred_element_type=F32)
    rank = jnp.sum(jnp.where(onehot, before + carry_ref[:, 0:1], 0.0), axis=0, keepdims=True)
    carry_ref[...] = carry_ref[...] + jnp.sum(onehot.astype(F32), axis=1, keepdims=True)
    cnt_ref[...] = carry_ref[...].astype(jnp.int32)

    meta_ref[...] = jnp.zeros(meta_ref.shape, jnp.int32)
    meta_ref[0:1, :] = bucket
    meta_ref[1:2, :] = rank.astype(jnp.int32)

    gt_scr[...] = jnp.zeros(gt_scr.shape, F32)
    gt_scr[0:1, :] = jnp.where(i1 == lo, w1, w2)
    gt_scr[1:2, :] = jnp.where(i1 == lo, w2, w1)
    x1g_ref[:, D_MODEL:] = gt_scr[...].T


def _outproj_ln_route(y, x, w_out, ln_g, ln_b, router_w, router_b, tm=512):
    t, d = x.shape
    rw = jnp.zeros((d, LANES), F32).at[:, 0:N_EXPERTS].set(router_w.astype(F32))
    rw_hi = rw.astype(BF16)
    rw_lo = (rw - rw_hi.astype(F32)).astype(BF16)
    rb = jnp.broadcast_to(router_b.astype(F32)[:, None], (N_EXPERTS, LANES))
    return pl.pallas_call(
        _outproj_kernel, grid=(t // tm,),
        in_specs=[pl.BlockSpec((tm, y.shape[1]), lambda i: (i, 0)), pl.BlockSpec((tm, d), lambda i: (i, 0)),
                  _const_spec(w_out.shape), _const_spec((1, d)), _const_spec((1, d)),
                  _const_spec((d, LANES)), _const_spec((d, LANES)), _const_spec((N_EXPERTS, LANES))],
        out_specs=[pl.BlockSpec((tm, d + LANES), lambda i: (i, 0)),
                   pl.BlockSpec((8, tm), lambda i: (0, i)), _const_spec((BUCKET_ROWS, LANES))],
        out_shape=[jax.ShapeDtypeStruct((t, d + LANES), F32),
                   jax.ShapeDtypeStruct((8, t), jnp.int32),
                   jax.ShapeDtypeStruct((BUCKET_ROWS, LANES), jnp.int32)],
        scratch_shapes=[pltpu.VMEM((LANES, tm), F32), pltpu.VMEM((BUCKET_ROWS, LANES), F32)],
        compiler_params=_cparams("arbitrary"), name="outproj_ln_route")(
            y, x, w_out, ln_g.astype(F32).reshape(1, d), ln_b.astype(F32).reshape(1, d), rw_hi, rw_lo, rb)


def _moe_plan(meta, cnt, tile_rows, n_tiles):
    counts = cnt[0:N_BUCKETS, 0]
    tiles = (counts + tile_rows - 1) // tile_rows
    tile_end = jnp.cumsum(tiles)
    offs = (tile_end - tiles) * tile_rows
    dest = (offs[meta[0]] + meta[1]).astype(jnp.int32)
    n_used = tile_end[-1]
    j = jnp.minimum(jnp.arange(n_tiles, dtype=jnp.int32), n_used - 1)
    bucket_of_tile = jnp.minimum(jnp.sum(tile_end[None, :] <= j[:, None], axis=1), N_BUCKETS - 1)
    pairs = [(a, b) for a in range(EXPERTS_PER_GROUP) for b in range(a + 1, EXPERTS_PER_GROUP)]
    lo_tab = jnp.array([g * EXPERTS_PER_GROUP + a for g in range(N_GROUPS) for a, _ in pairs], jnp.int32)
    hi_tab = jnp.array([g * EXPERTS_PER_GROUP + b for g in range(N_GROUPS) for _, b in pairs], jnp.int32)
    return dest, lo_tab[bucket_of_tile], hi_tab[bucket_of_tile], n_used.astype(jnp.int32).reshape(1)


def _rows_to_tiles(ref, x, row_tiles, first=0):
    n = x.shape[0]
    for c in range(x.shape[1] // LANES):
        ref[pl.ds(first + c, n, stride=row_tiles), :] = x[:, c * LANES:(c + 1) * LANES]


def _tiles_to_rows(ref, n, row_tiles, first=0, count=D_MODEL // LANES):
    return jnp.concatenate([ref[pl.ds(first + c, n, stride=row_tiles), :] for c in range(count)], axis=1)


def _dispatch_kernel(dest_ref, x1g_ref, xs_in_ref, xs_ref, comb_ref, sem):
    del xs_in_ref
    tm = x1g_ref.shape[0]
    base = pl.program_id(0) * tm
    _rows_to_tiles(comb_ref, x1g_ref[:, 0:D_MODEL], X_ROW_TILES)

    def row_copy(r):
        src = pl.multiple_of(r * X_ROW_TILES, X_ROW_TILES)
        dst = pl.multiple_of(dest_ref[base + r] * X_ROW_TILES, X_ROW_TILES)
        return pltpu.make_async_copy(comb_ref.at[pl.ds(src, X_ROW_TILES)], xs_ref.at[pl.ds(dst, X_ROW_TILES)], sem)

    def issue(grp, carry):
        for u in range(DMA_ISSUE_UNROLL):
            row_copy(grp * DMA_ISSUE_UNROLL + u).start(priority=u % 2)
        return carry

    lax.fori_loop(0, tm // DMA_ISSUE_UNROLL, issue, 0)
    pltpu.make_async_copy(comb_ref, xs_ref.at[pl.ds(0, tm * X_ROW_TILES)], sem).wait()


def _dispatch(dest, x1g, xs, tm=2048):
    t = x1g.shape[0]
    return pl.pallas_call(
        _dispatch_kernel,
        grid_spec=pltpu.PrefetchScalarGridSpec(
            num_scalar_prefetch=1, grid=(t // tm,),
            in_specs=[pl.BlockSpec((tm, x1g.shape[1]), lambda i, dest: (i, 0)),
                      pl.BlockSpec(memory_space=pl.ANY)],
            out_specs=pl.BlockSpec(memory_space=pl.ANY),
            scratch_shapes=[pltpu.VMEM((tm * X_ROW_TILES, LANES), F32), pltpu.SemaphoreType.DMA(())]),
        out_shape=jax.ShapeDtypeStruct(xs.shape, F32), input_output_aliases={2: 0},
        compiler_params=_cparams("arbitrary"), name="moe_dispatch")(dest, x1g, xs)


def _expert(x, wg_ref, wu_ref, wd_ref):
    hg = jnp.dot(x, wg_ref[0, 0], preferred_element_type=F32)
    hu = jnp.dot(x, wu_ref[0, 0], preferred_element_type=F32)
    return jnp.dot((_silu(hg) * hu).astype(BF16), wd_ref[0, 0], preferred_element_type=F32)


def _experts_kernel(lo_ref, hi_ref, nused_ref, xs_ref, wg0, wu0, wd0, wg1, wu1, wd1, ys_ref):
    del lo_ref, hi_ref
    tm = xs_ref.shape[0] // X_ROW_TILES

    @pl.when(pl.program_id(0) < nused_ref[0])
    def _():
        x = _tiles_to_rows(xs_ref, tm, X_ROW_TILES).astype(BF16)
        _rows_to_tiles(ys_ref, _expert(x, wg0, wu0, wd0), Y_ROW_TILES)
        _rows_to_tiles(ys_ref, _expert(x, wg1, wu1, wd1), Y_ROW_TILES, first=X_ROW_TILES)

    @pl.when(pl.program_id(0) >= nused_ref[0])
    def _():
        ys_ref[...] = jnp.zeros(ys_ref.shape, F32)


def _experts(layer, tile_lo, tile_hi, n_used, xs, wg, wu, wd, tm):
    d = D_MODEL
    n_tiles = xs.shape[0] // (tm * X_ROW_TILES)
    used = lambda j, n: jnp.maximum(jnp.minimum(j, n[0] - 1), 0)
    w_in = lambda which: pl.BlockSpec(
        (1, 1, d, D_EXPERT), lambda j, lo, hi, n: (layer, (lo, hi)[which][used(j, n)], 0, 0))
    w_out = lambda which: pl.BlockSpec(
        (1, 1, D_EXPERT, d), lambda j, lo, hi, n: (layer, (lo, hi)[which][used(j, n)], 0, 0))
    return pl.pallas_call(
        _experts_kernel,
        grid_spec=pltpu.PrefetchScalarGridSpec(
            num_scalar_prefetch=3, grid=(n_tiles,),
            in_specs=[pl.BlockSpec((tm * X_ROW_TILES, LANES), lambda j, lo, hi, n: (used(j, n), 0)),
                      w_in(0), w_in(0), w_out(0), w_in(1), w_in(1), w_out(1)],
            out_specs=pl.BlockSpec((tm * Y_ROW_TILES, LANES), lambda j, lo, hi, n: (j, 0))),
        out_shape=jax.ShapeDtypeStruct((n_tiles * tm * Y_ROW_TILES, LANES), F32),
        compiler_params=_cparams("arbitrary"), name="moe_experts")(
            tile_lo, tile_hi, n_used, xs, wg, wu, wd, wg, wu, wd)


def _combine_kernel(dest_ref, x1g_ref, p_ref, ys_ref, g_ref, b_ref, wpg_ref, wpp_ref, o_ref, ybuf, sems):
    i = pl.program_id(0)
    tm = x1g_ref.shape[0]
    slot = i % 2

    def row_copy(tile, s, r):
        src = pl.multiple_of(dest_ref[tile * tm + r] * Y_ROW_TILES, Y_ROW_TILES)
        dst = pl.multiple_of(r * Y_ROW_TILES, Y_ROW_TILES)
        return pltpu.make_async_copy(ys_ref.at[pl.ds(src, Y_ROW_TILES)],
                                     ybuf.at[s].at[pl.ds(dst, Y_ROW_TILES)], sems.at[s])

    def start_tile(tile, s):
        def issue(grp, carry):
            for u in range(DMA_ISSUE_UNROLL):
                row_copy(tile, s, grp * DMA_ISSUE_UNROLL + u).start(priority=u % 2)
            return carry
        lax.fori_loop(0, tm // DMA_ISSUE_UNROLL, issue, 0)

    @pl.when(i == 0)
    def _():
        start_tile(0, 0)

    @pl.when(i + 1 < pl.num_programs(0))
    def _():
        start_tile(i + 1, 1 - slot)

    pltpu.make_async_copy(ys_ref.at[pl.ds(0, tm * Y_ROW_TILES)], ybuf.at[slot], sems.at[slot]).wait()
    g = x1g_ref[:, D_MODEL:]
    y = (g[:, 0:1] * _tiles_to_rows(ybuf.at[slot], tm, Y_ROW_TILES)
         + g[:, 1:2] * _tiles_to_rows(ybuf.at[slot], tm, Y_ROW_TILES, first=X_ROW_TILES))
    x2 = _layer_norm(DEEPNORM_ALPHA * x1g_ref[:, 0:D_MODEL] + y, g_ref[...], b_ref[...])
    gate = _sigmoid(jnp.dot(x2.astype(BF16), wpg_ref[...], preferred_element_type=F32))
    o_ref[...] = x2 + gate * jnp.dot(p_ref[...].astype(BF16), wpp_ref[...], preferred_element_type=F32)


def _combine_ln_ple(layer, dest, x1g, p, ys, ln_g, ln_b, w_gate, w_proj, tm=512):
    t = x1g.shape[0]
    d = D_MODEL
    const = lambda shape: pl.BlockSpec(shape, lambda i, dest: (0,) * len(shape))
    return pl.pallas_call(
        _combine_kernel,
        grid_spec=pltpu.PrefetchScalarGridSpec(
            num_scalar_prefetch=1, grid=(t // tm,),
            in_specs=[pl.BlockSpec((tm, x1g.shape[1]), lambda i, dest: (i, 0)),
                      pl.BlockSpec((tm, PLE_DIM), lambda i, dest: (layer * (t // tm) + i, 0)),
                      pl.BlockSpec(memory_space=pl.ANY), const((1, d)), const((1, d)),
                      const((d, d)), const((PLE_DIM, d))],
            out_specs=pl.BlockSpec((tm, d), lambda i, dest: (i, 0)),
            scratch_shapes=[pltpu.VMEM((2, tm * Y_ROW_TILES, LANES), F32), pltpu.SemaphoreType.DMA((2,))]),
        out_shape=jax.ShapeDtypeStruct((t, d), F32),
        compiler_params=_cparams("arbitrary"), name="moe_combine_ln_ple")(
            dest, x1g, p, ys, ln_g.astype(F32).reshape(1, d), ln_b.astype(F32).reshape(1, d), w_gate, w_proj)


def _pad_cols(w, n):
    return jnp.zeros((w.shape[0], n), w.dtype).at[:, 0:w.shape[1]].set(w)


def kernel(x, p, ln1_g, ln1_b, ln2_g, ln2_b, mlstm_w_in, mlstm_conv, mlstm_gate_b, mlstm_norm_g, mlstm_w_out, gla_w_in, gla_w_a2, gla_b_a, gla_norm_g, gla_w_out, hgrn_w_in, hgrn_lb, hgrn_norm_g, hgrn_w_out, router_w, router_b, exp_w_gate, exp_w_up, exp_w_down, ple_w_proj, ple_w_gate):
    bsz, seq, d = x.shape
    assert bsz == 1 and d == D_MODEL and seq % 2048 == 0
    xt = x.reshape(seq, d).astype(F32)
    n_tiles = seq // MOE_TILE + N_BUCKETS
    xs = jnp.zeros((n_tiles * MOE_TILE * X_ROW_TILES, LANES), F32)
    wg_all, wu_all, wd_all = exp_w_gate.astype(BF16), exp_w_up.astype(BF16), exp_w_down.astype(BF16)
    for i in range(DEPTH):
        kind, slot = i % N_MIXERS, i // N_MIXERS
        if kind == 0:
            w_in = mlstm_w_in[slot]
            w = _pad_cols(w_in.astype(BF16), A_PROJ_PAD)
            wgt = jnp.zeros((16, d), BF16).at[0:2 * A_HEADS].set(w_in[:, A_MAIN:].T.astype(BF16))
            z, gt = _inproj(xt, w, wgt)
            y = _mlstm_mix(z, gt, mlstm_conv[slot], mlstm_gate_b[slot], mlstm_norm_g[slot])
            w_out = mlstm_w_out[slot]
        elif kind == 1:
            z = _inproj(xt, _pad_cols(gla_w_in[slot].astype(BF16), B_PROJ_PAD))
            y = _gla_mix(z, gla_w_a2[slot], gla_b_a[slot], gla_norm_g[slot])
            w_out = gla_w_out[slot]
        else:
            z = _inproj(xt, hgrn_w_in[slot].astype(BF16))
            y = _hgrn_mix(z, hgrn_lb, i, hgrn_norm_g[slot])
            w_out = hgrn_w_out[slot]
        x1g, meta, cnt = _outproj_ln_route(y, xt, w_out.astype(BF16), ln1_g[i], ln1_b[i], router_w, router_b)
        dest, tile_lo, tile_hi, n_used = _moe_plan(meta, cnt, MOE_TILE, n_tiles)
        xs = _dispatch(dest, x1g, xs)
        ys = _experts(i, tile_lo, tile_hi, n_used, xs, wg_all, wu_all, wd_all, MOE_TILE)
        xt = _combine_ln_ple(i, dest, x1g, p.reshape(DEPTH * seq, PLE_DIM), ys, ln2_g[i], ln2_b[i],
                             ple_w_gate[i].astype(BF16), ple_w_proj[i].astype(BF16))
    return xt.reshape(bsz, seq, d)
```

```python
import functools

import jax
import jax.numpy as jnp
from jax import lax
from jax.experimental import pallas as pl
from jax.experimental.pallas import tpu as pltpu

F32 = jnp.float32
BF16 = jnp.bfloat16

D_MODEL = 1024
DEPTH = 4
CHUNK = 64
PLE_DIM = 256
N_MIXERS = 3
LN_EPS = 1e-5
DEEPNORM_ALPHA = (2 * DEPTH) ** 0.25
A_HEADS, A_DK, A_DV = 4, 128, 256
B_HEADS, B_DK, B_DV = 4, 128, 256
B_RANK = 16
B_TAU = 16.0
C_HEADS, C_DK, C_DV = 8, 128, 128
N_EXPERTS = 16
N_GROUPS = 4
EXPERTS_PER_GROUP = 4
D_EXPERT = 512
PAIRS_PER_GROUP = EXPERTS_PER_GROUP * (EXPERTS_PER_GROUP - 1) // 2
N_BUCKETS = N_GROUPS * PAIRS_PER_GROUP
BUCKET_ROWS = 32
MOE_TILE = 256
DMA_ISSUE_UNROLL = 64

LANES = 128
SLOT = 8
X_ROW_TILES = D_MODEL // LANES
Y_ROW_TILES = 2 * X_ROW_TILES
N_SLOTS = CHUNK // SLOT
A_MAIN = 2 * A_HEADS * A_DK + 2 * A_HEADS * A_DV
A_PROJ_PAD = A_MAIN + LANES
B_MAIN = 2 * B_HEADS * B_DK + 2 * B_HEADS * B_DV
B_PROJ_PAD = B_MAIN + LANES
C_PROJ = 2 * C_HEADS * C_DK + 2 * C_HEADS * C_DV
VMEM_LIMIT = 56 * 1024 * 1024

NT_DIMS = (((1,), (1,)), ((), ()))
TN_DIMS = (((0,), (0,)), ((), ()))


def _cparams(*sem):
    return pltpu.CompilerParams(dimension_semantics=sem, vmem_limit_bytes=VMEM_LIMIT)


def _sigmoid(x):
    return 0.5 * jnp.tanh(0.5 * x) + 0.5


def _silu(x):
    return x * _sigmoid(x)


def _log_sigmoid(x):
    return jnp.minimum(x, 0.0) - jnp.log(1.0 + jnp.exp(-jnp.abs(x)))


def _layer_norm(u, g, b):
    mu = jnp.mean(u, axis=-1, keepdims=True)
    d = u - mu
    var = jnp.mean(d * d, axis=-1, keepdims=True)
    return d * lax.rsqrt(var + LN_EPS) * g + b


def _tri(n, lower):
    r = lax.broadcasted_iota(jnp.int32, (n, n), 0)
    c = lax.broadcasted_iota(jnp.int32, (n, n), 1)
    return (r >= c) if lower else (r <= c)


def _bdot(a, b):
    return jnp.dot(a.astype(BF16), b.astype(BF16), preferred_element_type=F32)


def _bdot_nt(a, b):
    return lax.dot_general(a.astype(BF16), b.astype(BF16), NT_DIMS, preferred_element_type=F32)


def _bdot_tn(a, b):
    return lax.dot_general(a.astype(BF16), b.astype(BF16), TN_DIMS, preferred_element_type=F32)


def _const_spec(shape):
    return pl.BlockSpec(shape, lambda i: (0,) * len(shape))


def _inproj_kernel(x_ref, w_ref, z_ref):
    z_ref[...] = jnp.dot(x_ref[...].astype(BF16), w_ref[...], preferred_element_type=F32)


def _inproj_gates_kernel(x_ref, w_ref, wgt_ref, z_ref, gt_ref):
    xb = x_ref[...].astype(BF16)
    z_ref[...] = jnp.dot(xb, w_ref[...], preferred_element_type=F32)
    for c in range(gt_ref.shape[0]):
        gt_ref[c] = lax.dot_general(wgt_ref[...], xb[c * CHUNK:(c + 1) * CHUNK], NT_DIMS,
                                    preferred_element_type=F32)


def _inproj(x, w, wgt=None, tm=512):
    t, d = x.shape
    n = w.shape[1]
    grid = (t // tm,)
    x_spec = pl.BlockSpec((tm, d), lambda i: (i, 0))
    z_spec = pl.BlockSpec((tm, n), lambda i: (i, 0))
    if wgt is None:
        return pl.pallas_call(
            _inproj_kernel, grid=grid, in_specs=[x_spec, _const_spec((d, n))], out_specs=z_spec,
            out_shape=jax.ShapeDtypeStruct((t, n), F32), compiler_params=_cparams("parallel"),
            name="inproj")(x, w)
    return pl.pallas_call(
        _inproj_gates_kernel, grid=grid,
        in_specs=[x_spec, _const_spec((d, n)), _const_spec(wgt.shape)],
        out_specs=[z_spec, pl.BlockSpec((tm // CHUNK, 16, CHUNK), lambda i: (i, 0, 0))],
        out_shape=[jax.ShapeDtypeStruct((t, n), F32),
                   jax.ShapeDtypeStruct((t // CHUNK, 16, CHUNK), F32)],
        compiler_params=_cparams("parallel"), name="inproj_gates")(x, w, wgt)


def _mlstm_kernel(z_ref, gt_ref, conv_ref, gbr_ref, gbc_ref, ng_ref, y_ref,
                  ext_ref, ct_ref, n_ref, m_ref):
    tt = z_ref.shape[0]
    nqk = A_HEADS * A_DK

    @pl.when(pl.program_id(0) == 0)
    def _():
        ext_ref[0:8, :] = jnp.zeros((8, 2 * nqk), F32)
        ct_ref[...] = jnp.zeros(ct_ref.shape, F32)
        n_ref[...] = jnp.zeros(n_ref.shape, F32)
        m_ref[...] = jnp.zeros(m_ref.shape, F32)

    qk_raw = z_ref[:, 0:2 * nqk]
    ext_ref[8:8 + tt, :] = qk_raw
    acc = conv_ref[3:4, :] * qk_raw
    for s in range(1, 4):
        acc = acc + conv_ref[3 - s:4 - s, :] * ext_ref[pl.ds(8 - s, tt), :]
    ext_ref[0:8, :] = qk_raw[tt - 8:tt, :]
    qkc = _silu(acc)

    tri_lo = _tri(CHUNK, True)
    tri_lo_f = tri_lo.astype(F32)
    tri_up_f = _tri(CHUNK, False).astype(F32)

    for c in range(tt // CHUNK):
        r0 = c * CHUNK
        gc = z_ref[r0:r0 + CHUNK, A_MAIN:A_MAIN + LANES] + gbr_ref[...]
        a_col = jnp.dot(tri_lo_f, _log_sigmoid(gc), preferred_element_type=F32,
                        precision=lax.Precision.HIGHEST)
        gr = gt_ref[c] + gbc_ref[:, 0:1]
        a_row = jnp.dot(_log_sigmoid(gr), tri_up_f, preferred_element_type=F32,
                        precision=lax.Precision.HIGHEST)
        for h in range(A_HEADS):
            ac = a_col[:, A_HEADS + h:A_HEADS + h + 1]
            lic = gc[:, h:h + 1]
            ar = a_row[A_HEADS + h:A_HEADS + h + 1, :]
            lir = gr[h:h + 1, :]
            m_st = m_ref[h:h + 1, 0:1]
            qh = qkc[r0:r0 + CHUNK, h * A_DK:(h + 1) * A_DK]
            kh = qkc[r0:r0 + CHUNK, nqk + h * A_DK:nqk + (h + 1) * A_DK] * (A_DK ** -0.5)
            vh = z_ref[r0:r0 + CHUNK, 2 * nqk + h * A_DV:2 * nqk + (h + 1) * A_DV]
            oh = z_ref[r0:r0 + CHUNK, 2 * nqk + A_HEADS * A_DV + h * A_DV:
                       2 * nqk + A_HEADS * A_DV + (h + 1) * A_DV]

            d_log = jnp.where(tri_lo, ac - ar + lir, -jnp.inf)
            d_max = jnp.max(d_log, axis=-1, keepdims=True)
            s = _bdot_nt(qh, kh) * jnp.exp(d_log - d_max)
            sv = _bdot(s, vh)
            s_sum = jnp.sum(s, axis=-1, keepdims=True)
            a_tot = ar[:, CHUNK - 1:CHUNK]
            g_max = jnp.max(a_tot - ar + lir, axis=-1, keepdims=True)
            wk = jnp.exp(a_tot - ac + lic - g_max) * kh
            c_inc = _bdot_tn(vh, wk)
            n_inc = jnp.sum(wk, axis=0, keepdims=True)

            e_log = ac + m_st
            m_row = jnp.maximum(e_log, d_max)
            w_inter = jnp.exp(e_log - m_row)
            w_intra = jnp.exp(d_max - m_row)
            num = w_inter * _bdot_nt(qh, ct_ref[h]) + w_intra * sv
            den = (w_inter * jnp.sum(qh * n_ref[h:h + 1, :], axis=-1, keepdims=True) + w_intra * s_sum)
            hout = num / jnp.maximum(jnp.abs(den), jnp.exp(-m_row))
            m_new = jnp.maximum(a_tot + m_st, g_max)
            decay = jnp.exp(a_tot + m_st - m_new)
            inc_scale = jnp.exp(g_max - m_new)
            ct_ref[h] = decay * ct_ref[h] + inc_scale * c_inc
            n_ref[h:h + 1, :] = decay * n_ref[h:h + 1, :] + inc_scale * n_inc
            m_ref[h:h + 1, :] = jnp.broadcast_to(m_new, (1, LANES))

            hn = hout * lax.rsqrt(jnp.mean(hout * hout, axis=-1, keepdims=True) + LN_EPS)
            y_ref[r0:r0 + CHUNK, h * A_DV:(h + 1) * A_DV] = (
                hn * ng_ref[:, h * A_DV:(h + 1) * A_DV] * _sigmoid(oh))


def _mlstm_mix(z, gt, conv_w, gate_b, norm_g, tt=256):
    t = z.shape[0]
    nqk2 = 2 * A_HEADS * A_DK
    conv_p = jnp.zeros((8, nqk2), F32).at[0:4].set(conv_w.astype(F32))
    gb = gate_b.astype(F32)
    gb_row = jnp.zeros((1, LANES), F32).at[0, 0:2 * A_HEADS].set(gb)
    gb_col = jnp.zeros((16, LANES), F32).at[0:2 * A_HEADS, :].set(jnp.broadcast_to(gb[:, None], (2 * A_HEADS, LANES)))
    ng = norm_g.astype(F32).reshape(1, A_HEADS * A_DV)
    return pl.pallas_call(
        _mlstm_kernel, grid=(t // tt,),
        in_specs=[pl.BlockSpec((tt, A_PROJ_PAD), lambda i: (i, 0)),
                  pl.BlockSpec((tt // CHUNK, 16, CHUNK), lambda i: (i, 0, 0)),
                  _const_spec((8, nqk2)), _const_spec((1, LANES)), _const_spec((16, LANES)),
                  _const_spec((1, A_HEADS * A_DV))],
        out_specs=pl.BlockSpec((tt, A_HEADS * A_DV), lambda i: (i, 0)),
        out_shape=jax.ShapeDtypeStruct((t, A_HEADS * A_DV), F32),
        scratch_shapes=[pltpu.VMEM((tt + 8, nqk2), F32),
                        pltpu.VMEM((A_HEADS, A_DV, A_DK), F32),
                        pltpu.VMEM((8, A_DK), F32),
                        pltpu.VMEM((8, LANES), F32)],
        compiler_params=_cparams("arbitrary"), name="mlstm_mix")(z, gt, conv_p, gb_row, gb_col, ng)


def _store_lane_tiles(ref, x):
    for t in range(ref.shape[0]):
        ref[t] = x[:, t * LANES:(t + 1) * LANES]


def _interleaved(ref):
    return jnp.concatenate(
        [jnp.concatenate([ref[t, pl.ds(c, N_SLOTS, stride=SLOT), :] for t in range(ref.shape[0])], axis=1)
         for c in range(SLOT)], axis=0)


def _pair_factors(q, k, b, ref_of_slot):
    zero = jnp.zeros((SLOT, q.shape[1]), F32)
    factors = []
    for half in (1, 2, 4):
        qs, ks = [], []
        for u in range(N_SLOTS):
            r = ref_of_slot((u // (2 * half)) * 2 * half + half - 1)
            rows = slice(u * SLOT, (u + 1) * SLOT)
            if u % (2 * half) >= half:
                qs.append(q[rows] * jnp.exp(b[rows] - r))
                ks.append(zero)
            else:
                qs.append(zero)
                ks.append(k[rows] * jnp.exp(r - b[rows]))
        factors.append((jnp.concatenate(qs, axis=0), jnp.concatenate(ks, axis=0)))
    return factors


def _pair_masks(copies=1):
    r = lax.broadcasted_iota(jnp.int32, (CHUNK, CHUNK * copies), 0)
    c = lax.broadcasted_iota(jnp.int32, (CHUNK, CHUNK * copies), 1) & (CHUNK - 1)
    sr, sc = r >> 3, c >> 3
    same_sublane = (r & (SLOT - 1)) == (c & (SLOT - 1))
    natural, interleaved = [], []
    for shift, half in ((1, 1), (2, 2), (3, 4)):
        m = jnp.logical_and((sr >> shift) == (sc >> shift),
                            jnp.logical_and((sr & half) != 0, (sc & half) == 0))
        natural.append(m)
        interleaved.append(jnp.logical_and(m, same_sublane))
    return r == c, natural, interleaved


def _decay_attention_chunk(q, k, v, la, st_ref, qkb_ref, vo_ref, heads, dv):
    dk = LANES
    tri = _tri(CHUNK, True).astype(BF16)
    la_1 = la.astype(BF16)
    rest = la - la_1.astype(F32)
    la_2 = rest.astype(BF16)
    la_3 = (rest - la_2.astype(F32)).astype(BF16)
    b = (jnp.dot(tri, la_1, preferred_element_type=F32)
         + (jnp.dot(tri, la_2, preferred_element_type=F32) + jnp.dot(tri, la_3, preferred_element_type=F32)))
    b_last = b[CHUNK - 1:CHUNK, :]
    qe = q * jnp.exp(b)
    kd = k * jnp.exp(b_last - b)

    _store_lane_tiles(qkb_ref.at[0], q)
    _store_lane_tiles(qkb_ref.at[1], k)
    _store_lane_tiles(qkb_ref.at[2], b)
    _store_lane_tiles(vo_ref.at[0], v)
    q_i, k_i, b_i = _interleaved(qkb_ref.at[0]), _interleaved(qkb_ref.at[1]), _interleaved(qkb_ref.at[2])
    v_i = _interleaved(vo_ref.at[0])

    fact_n = _pair_factors(q, k, b, lambda s: b[s * SLOT + SLOT - 1:(s + 1) * SLOT, :])
    fact_i = _pair_factors(q_i, k_i, b_i, lambda s: b_i[s * SLOT:(s + 1) * SLOT, :])

    def pair_scores(qf, kf, h):
        kc = slice(h * dk, (h + 2) * dk)
        zero = jnp.zeros((CHUNK, dk), F32)
        kb = kf[:, kc]
        k_diag = jnp.concatenate([jnp.concatenate([kb[:, 0:dk], zero], axis=1),
                                  jnp.concatenate([zero, kb[:, dk:]], axis=1)], axis=0)
        return _bdot_nt(qf[:, kc], k_diag)

    def pair_apply(att, vv, h):
        vc = slice(h * dv, (h + 2) * dv)
        zero = jnp.zeros((CHUNK, dv), F32)
        vb = vv[:, vc]
        v_diag = jnp.concatenate([jnp.concatenate([vb[:, 0:dv], zero], axis=1),
                                  jnp.concatenate([zero, vb[:, dv:]], axis=1)], axis=0)
        return _bdot(att, v_diag)

    eye2, masks_n2, masks_i2 = _pair_masks(2)
    outs = []
    for h in range(0, heads, 2):
        att = jnp.where(eye2, pair_scores(q, k, h), 0.0)
        for (qf, kf), m in zip(fact_n, masks_n2):
            att = att + jnp.where(m, pair_scores(qf, kf, h), 0.0)
        att_i = jnp.zeros((CHUNK, 2 * CHUNK), F32)
        for (qf, kf), m in zip(fact_i, masks_i2):
            att_i = att_i + jnp.where(m, pair_scores(qf, kf, h), 0.0)
        o_pair = pair_apply(att, v, h)
        _store_lane_tiles(vo_ref.at[1, h * dv // LANES:(h + 2) * dv // LANES], pair_apply(att_i, v_i, h))
        for g in (h, h + 1):
            kc = slice(g * dk, (g + 1) * dk)
            vc = slice(g * dv, (g + 1) * dv)
            outs.append(_bdot_nt(qe[:, kc], st_ref[g]) + o_pair[:, (g - h) * dv:(g - h + 1) * dv])
            st_ref[g] = st_ref[g] * jnp.exp(b_last[:, kc]) + _bdot_tn(v[:, vc], kd[:, kc])
    o_i = _interleaved(vo_ref.at[1])
    return [o + o_i[:, h * dv:(h + 1) * dv] for h, o in enumerate(outs)]


def _head_norm_store(outs, y_ref, r0, ng_ref, gate_act, dv):
    for h, o in enumerate(outs):
        hn = o * lax.rsqrt(jnp.mean(o * o, axis=-1, keepdims=True) + LN_EPS)
        y_ref[r0:r0 + CHUNK, h * dv:(h + 1) * dv] = (
            hn * ng_ref[:, h * dv:(h + 1) * dv] * gate_act[:, h * dv:(h + 1) * dv])


def _gla_kernel(z_ref, wa2_ref, ba_ref, ng_ref, y_ref, st_ref, qkb_ref, vo_ref):
    @pl.when(pl.program_id(0) == 0)
    def _():
        st_ref[...] = jnp.zeros(st_ref.shape, F32)

    nk, nv = B_HEADS * B_DK, B_HEADS * B_DV
    for c in range(z_ref.shape[0] // CHUNK):
        r0 = c * CHUNK
        q = z_ref[r0:r0 + CHUNK, 0:nk] * (B_DK ** -0.5)
        k = z_ref[r0:r0 + CHUNK, nk:2 * nk]
        v = z_ref[r0:r0 + CHUNK, 2 * nk:2 * nk + nv]
        g = z_ref[r0:r0 + CHUNK, 2 * nk + nv:2 * nk + 2 * nv]
        a_lr = z_ref[r0:r0 + CHUNK, B_MAIN:B_MAIN + LANES]
        la = _log_sigmoid(jnp.dot(a_lr.astype(BF16), wa2_ref[...], preferred_element_type=F32)
                          + ba_ref[...]) / B_TAU
        outs = _decay_attention_chunk(q, k, v, la, st_ref, qkb_ref.at[c], vo_ref.at[c], B_HEADS, B_DV)
        _head_norm_store(outs, y_ref, r0, ng_ref, _silu(g), B_DV)


def _gla_mix(z, w_a2, b_a, norm_g, tt=256):
    t = z.shape[0]
    nk, nv = B_HEADS * B_DK, B_HEADS * B_DV
    wa2 = jnp.zeros((LANES, nk), BF16).at[0:B_RANK].set(w_a2.astype(BF16))
    return pl.pallas_call(
        _gla_kernel, grid=(t // tt,),
        in_specs=[pl.BlockSpec((tt, B_PROJ_PAD), lambda i: (i, 0)),
                  _const_spec((LANES, nk)), _const_spec((1, nk)), _const_spec((1, nv))],
        out_specs=pl.BlockSpec((tt, nv), lambda i: (i, 0)),
        out_shape=jax.ShapeDtypeStruct((t, nv), F32),
        scratch_shapes=[pltpu.VMEM((B_HEADS, B_DV, B_DK), F32),
                        pltpu.VMEM((tt // CHUNK, 3, nk // LANES, CHUNK, LANES), F32),
                        pltpu.VMEM((tt // CHUNK, 2, nv // LANES, CHUNK, LANES), F32)],
        compiler_params=_cparams("arbitrary"), name="gla_mix")(
            z, wa2, b_a.astype(F32).reshape(1, nk), norm_g.astype(F32).reshape(1, nv))


def _hgrn_kernel(layer, z_ref, lbp_ref, ng_ref, y_ref, st_ref, qkb_ref, vo_ref):
    @pl.when(pl.program_id(0) == 0)
    def _():
        st_ref[...] = jnp.zeros(st_ref.shape, F32)

    lbp = lbp_ref[0:DEPTH, :]
    e = jnp.exp(lbp - jnp.max(lbp, axis=0, keepdims=True))
    sm = e / jnp.sum(e, axis=0, keepdims=True)
    lb = jnp.zeros((1, sm.shape[1]), F32)
    for r in range(layer + 1):
        lb = lb + sm[r:r + 1, :]
    lb = lb - sm[0:1, :]

    nk, nv = C_HEADS * C_DK, C_HEADS * C_DV
    for c in range(z_ref.shape[0] // CHUNK):
        r0 = c * CHUNK
        q = _silu(z_ref[r0:r0 + CHUNK, 0:nk])
        fg = lb + (1.0 - lb) * _sigmoid(z_ref[r0:r0 + CHUNK, nk:2 * nk])
        v = z_ref[r0:r0 + CHUNK, 2 * nk:2 * nk + nv]
        g = z_ref[r0:r0 + CHUNK, 2 * nk + nv:2 * nk + 2 * nv]
        outs = _decay_attention_chunk(q, 1.0 - fg, v, jnp.log(fg), st_ref, qkb_ref.at[c], vo_ref.at[c],
                                      C_HEADS, C_DV)
        _head_norm_store(outs, y_ref, r0, ng_ref, _sigmoid(g), C_DV)


def _hgrn_mix(z, hgrn_lb, layer, norm_g, tt=256):
    t = z.shape[0]
    nk, nv = C_HEADS * C_DK, C_HEADS * C_DV
    lbp = jnp.zeros((8, nk), F32).at[0:DEPTH].set(hgrn_lb.astype(F32))
    return pl.pallas_call(
        functools.partial(_hgrn_kernel, layer), grid=(t // tt,),
        in_specs=[pl.BlockSpec((tt, C_PROJ), lambda i: (i, 0)),
                  _const_spec((8, nk)), _const_spec((1, nv))],
        out_specs=pl.BlockSpec((tt, nv), lambda i: (i, 0)),
        out_shape=jax.ShapeDtypeStruct((t, nv), F32),
        scratch_shapes=[pltpu.VMEM((C_HEADS, C_DV, C_DK), F32),
                        pltpu.VMEM((tt // CHUNK, 3, nk // LANES, CHUNK, LANES), F32),
                        pltpu.VMEM((tt // CHUNK, 2, nv // LANES, CHUNK, LANES), F32)],
        compiler_params=_cparams("arbitrary"), name="hgrn_mix")(
            z, lbp, norm_g.astype(F32).reshape(1, nv))


def _route(lt, rb_ref):
    rows = [lt[e:e + 1, :] for e in range(N_EXPERTS)]
    mx = rows[0]
    for r in rows[1:]:
        mx = jnp.maximum(mx, r)
    ex = [jnp.exp(r - mx) for r in rows]
    tot = ex[0]
    for r in ex[1:]:
        tot = tot + r
    scores = [r / tot for r in ex]
    sel = [scores[e] + rb_ref[e:e + 1, 0:1] for e in range(N_EXPERTS)]

    gscore = []
    for g in range(N_GROUPS):
        m = sel[g * EXPERTS_PER_GROUP:(g + 1) * EXPERTS_PER_GROUP]
        best = None
        for a in range(EXPERTS_PER_GROUP):
            for b2 in range(a + 1, EXPERTS_PER_GROUP):
                pair = m[a] + m[b2]
                best = pair if best is None else jnp.maximum(best, pair)
        gscore.append(best)
    g_best = gscore[0]
    g_idx = jnp.zeros_like(g_best, dtype=jnp.int32)
    for g in range(1, N_GROUPS):
        better = gscore[g] > g_best
        g_best = jnp.where(better, gscore[g], g_best)
        g_idx = jnp.where(better, g, g_idx)

    masked = [jnp.where(g_idx == (e // EXPERTS_PER_GROUP), sel[e], -jnp.inf) for e in range(N_EXPERTS)]

    def first_argmax(vals):
        best_v, best_i = vals[0], jnp.zeros_like(g_idx)
        for e in range(1, N_EXPERTS):
            better = vals[e] > best_v
            best_v = jnp.where(better, vals[e], best_v)
            best_i = jnp.where(better, e, best_i)
        return best_i

    i1 = first_argmax(masked)
    i2 = first_argmax([jnp.where(i1 == e, -jnp.inf, masked[e]) for e in range(N_EXPERTS)])
    w1 = jnp.zeros_like(mx)
    w2 = jnp.zeros_like(mx)
    for e in range(N_EXPERTS):
        w1 = jnp.where(i1 == e, scores[e], w1)
        w2 = jnp.where(i2 == e, scores[e], w2)
    wsum = w1 + w2
    return i1, i2, w1 / wsum, w2 / wsum


def _outproj_kernel(y_ref, x_ref, w_ref, g_ref, b_ref, rwh_ref, rwl_ref, rb_ref,
                    x1g_ref, meta_ref, cnt_ref, gt_scr, carry_ref):
    tm = x_ref.shape[0]

    @pl.when(pl.program_id(0) == 0)
    def _():
        carry_ref[...] = jnp.zeros(carry_ref.shape, F32)

    mix = jnp.dot(y_ref[...].astype(BF16), w_ref[...], preferred_element_type=F32)
    x1 = _layer_norm(DEEPNORM_ALPHA * x_ref[...] + mix, g_ref[...], b_ref[...])
    x1g_ref[:, 0:D_MODEL] = x1
    x_hi = x1.astype(BF16)
    x_lo = (x1 - x_hi.astype(F32)).astype(BF16)
    logits = (jnp.dot(x_hi, rwh_ref[...], preferred_element_type=F32)
              + (jnp.dot(x_lo, rwh_ref[...], preferred_element_type=F32)
                 + jnp.dot(x_hi, rwl_ref[...], preferred_element_type=F32)))
    i1, i2, w1, w2 = _route(logits.T, rb_ref)

    lo = jnp.minimum(i1, i2)
    hi = jnp.maximum(i1, i2)
    a = lo & (EXPERTS_PER_GROUP - 1)
    b2 = hi & (EXPERTS_PER_GROUP - 1)
    bucket = (lo >> 2) * PAIRS_PER_GROUP + ((a * (7 - a)) >> 1) + (b2 - a - 1)

    onehot = lax.broadcasted_iota(jnp.int32, (BUCKET_ROWS, tm), 0) == bucket
    r = lax.broadcasted_iota(jnp.int32, (tm, tm), 0)
    c = lax.broadcasted_iota(jnp.int32, (tm, tm), 1)
    before = jnp.dot(onehot.astype(BF16), (r < c).astype(BF16), preferred_element_type=F32)
    rank = jnp.sum(jnp.where(onehot, before + carry_ref[:, 0:1], 0.0), axis=0, keepdims=True)
    carry_ref[...] = carry_ref[...] + jnp.sum(onehot.astype(F32), axis=1, keepdims=True)
    cnt_ref[...] = carry_ref[...].astype(jnp.int32)

    meta_ref[...] = jnp.zeros(meta_ref.shape, jnp.int32)
    meta_ref[0:1, :] = bucket
    meta_ref[1:2, :] = rank.astype(jnp.int32)

    gt_scr[...] = jnp.zeros(gt_scr.shape, F32)
    gt_scr[0:1, :] = jnp.where(i1 == lo, w1, w2)
    gt_scr[1:2, :] = jnp.where(i1 == lo, w2, w1)
    x1g_ref[:, D_MODEL:] = gt_scr[...].T


def _outproj_ln_route(y, x, w_out, ln_g, ln_b, router_w, router_b, tm=512):
    t, d = x.shape
    rw = jnp.zeros((d, LANES), F32).at[:, 0:N_EXPERTS].set(router_w.astype(F32))
    rw_hi = rw.astype(BF16)
    rw_lo = (rw - rw_hi.astype(F32)).astype(BF16)
    rb = jnp.broadcast_to(router_b.astype(F32)[:, None], (N_EXPERTS, LANES))
    return pl.pallas_call(
        _outproj_kernel, grid=(t // tm,),
        in_specs=[pl.BlockSpec((tm, y.shape[1]), lambda i: (i, 0)), pl.BlockSpec((tm, d), lambda i: (i, 0)),
                  _const_spec(w_out.shape), _const_spec((1, d)), _const_spec((1, d)),
                  _const_spec((d, LANES)), _const_spec((d, LANES)), _const_spec((N_EXPERTS, LANES))],
        out_specs=[pl.BlockSpec((tm, d + LANES), lambda i: (i, 0)),
                   pl.BlockSpec((8, tm), lambda i: (0, i)), _const_spec((BUCKET_ROWS, LANES))],
        out_shape=[jax.ShapeDtypeStruct((t, d + LANES), F32),
                   jax.ShapeDtypeStruct((8, t), jnp.int32),
                   jax.ShapeDtypeStruct((BUCKET_ROWS, LANES), jnp.int32)],
        scratch_shapes=[pltpu.VMEM((LANES, tm), F32), pltpu.VMEM((BUCKET_ROWS, LANES), F32)],
        compiler_params=_cparams("arbitrary"), name="outproj_ln_route")(
            y, x, w_out, ln_g.astype(F32).reshape(1, d), ln_b.astype(F32).reshape(1, d), rw_hi, rw_lo, rb)


def _moe_plan(meta, cnt, tile_rows, n_tiles):
    counts = cnt[0:N_BUCKETS, 0]
    tiles = (counts + tile_rows - 1) // tile_rows
    tile_end = jnp.cumsum(tiles)
    offs = (tile_end - tiles) * tile_rows
    dest = (offs[meta[0]] + meta[1]).astype(jnp.int32)
    n_used = tile_end[-1]
    j = jnp.minimum(jnp.arange(n_tiles, dtype=jnp.int32), n_used - 1)
    bucket_of_tile = jnp.minimum(jnp.sum(tile_end[None, :] <= j[:, None], axis=1), N_BUCKETS - 1)
    pairs = [(a, b) for a in range(EXPERTS_PER_GROUP) for b in range(a + 1, EXPERTS_PER_GROUP)]
    lo_tab = jnp.array([g * EXPERTS_PER_GROUP + a for g in range(N_GROUPS) for a, _ in pairs], jnp.int32)
    hi_tab = jnp.array([g * EXPERTS_PER_GROUP + b for g in range(N_GROUPS) for _, b in pairs], jnp.int32)
    return dest, lo_tab[bucket_of_tile], hi_tab[bucket_of_tile], n_used.astype(jnp.int32).reshape(1)


def _rows_to_tiles(ref, x, row_tiles, first=0):
    n = x.shape[0]
    for c in range(x.shape[1] // LANES):
        ref[pl.ds(first + c, n, stride=row_tiles), :] = x[:, c * LANES:(c + 1) * LANES]


def _tiles_to_rows(ref, n, row_tiles, first=0, count=D_MODEL // LANES):
    return jnp.concatenate([ref[pl.ds(first + c, n, stride=row_tiles), :] for c in range(count)], axis=1)


def _dispatch_kernel(dest_ref, x1g_ref, xs_in_ref, xs_ref, comb_ref, sem):
    del xs_in_ref
    tm = x1g_ref.shape[0]
    base = pl.program_id(0) * tm
    _rows_to_tiles(comb_ref, x1g_ref[:, 0:D_MODEL], X_ROW_TILES)

    def row_copy(r):
        src = pl.multiple_of(r * X_ROW_TILES, X_ROW_TILES)
        dst = pl.multiple_of(dest_ref[base + r] * X_ROW_TILES, X_ROW_TILES)
        return pltpu.make_async_copy(comb_ref.at[pl.ds(src, X_ROW_TILES)], xs_ref.at[pl.ds(dst, X_ROW_TILES)], sem)

    def issue(grp, carry):
        for u in range(DMA_ISSUE_UNROLL):
            row_copy(grp * DMA_ISSUE_UNROLL + u).start(priority=u % 2)
        return carry

    lax.fori_loop(0, tm // DMA_ISSUE_UNROLL, issue, 0)
    pltpu.make_async_copy(comb_ref, xs_ref.at[pl.ds(0, tm * X_ROW_TILES)], sem).wait()


def _dispatch(dest, x1g, xs, tm=2048):
    t = x1g.shape[0]
    return pl.pallas_call(
        _dispatch_kernel,
        grid_spec=pltpu.PrefetchScalarGridSpec(
            num_scalar_prefetch=1, grid=(t // tm,),
            in_specs=[pl.BlockSpec((tm, x1g.shape[1]), lambda i, dest: (i, 0)),
                      pl.BlockSpec(memory_space=pl.ANY)],
            out_specs=pl.BlockSpec(memory_space=pl.ANY),
            scratch_shapes=[pltpu.VMEM((tm * X_ROW_TILES, LANES), F32), pltpu.SemaphoreType.DMA(())]),
        out_shape=jax.ShapeDtypeStruct(xs.shape, F32), input_output_aliases={2: 0},
        compiler_params=_cparams("arbitrary"), name="moe_dispatch")(dest, x1g, xs)


def _expert(x, wg_ref, wu_ref, wd_ref):
    hg = jnp.dot(x, wg_ref[0, 0], preferred_element_type=F32)
    hu = jnp.dot(x, wu_ref[0, 0], preferred_element_type=F32)
    return jnp.dot((_silu(hg) * hu).astype(BF16), wd_ref[0, 0], preferred_element_type=F32)


def _experts_kernel(lo_ref, hi_ref, nused_ref, xs_ref, wg0, wu0, wd0, wg1, wu1, wd1, ys_ref):
    del lo_ref, hi_ref
    tm = xs_ref.shape[0] // X_ROW_TILES

    @pl.when(pl.program_id(0) < nused_ref[0])
    def _():
        x = _tiles_to_rows(xs_ref, tm, X_ROW_TILES).astype(BF16)
        _rows_to_tiles(ys_ref, _expert(x, wg0, wu0, wd0), Y_ROW_TILES)
        _rows_to_tiles(ys_ref, _expert(x, wg1, wu1, wd1), Y_ROW_TILES, first=X_ROW_TILES)

    @pl.when(pl.program_id(0) >= nused_ref[0])
    def _():
        ys_ref[...] = jnp.zeros(ys_ref.shape, F32)


def _experts(layer, tile_lo, tile_hi, n_used, xs, wg, wu, wd, tm):
    d = D_MODEL
    n_tiles = xs.shape[0] // (tm * X_ROW_TILES)
    used = lambda j, n: jnp.maximum(jnp.minimum(j, n[0] - 1), 0)
    w_in = lambda which: pl.BlockSpec(
        (1, 1, d, D_EXPERT), lambda j, lo, hi, n: (layer, (lo, hi)[which][used(j, n)], 0, 0))
    w_out = lambda which: pl.BlockSpec(
        (1, 1, D_EXPERT, d), lambda j, lo, hi, n: (layer, (lo, hi)[which][used(j, n)], 0, 0))
    return pl.pallas_call(
        _experts_kernel,
        grid_spec=pltpu.PrefetchScalarGridSpec(
            num_scalar_prefetch=3, grid=(n_tiles,),
            in_specs=[pl.BlockSpec((tm * X_ROW_TILES, LANES), lambda j, lo, hi, n: (used(j, n), 0)),
                      w_in(0), w_in(0), w_out(0), w_in(1), w_in(1), w_out(1)],
            out_specs=pl.BlockSpec((tm * Y_ROW_TILES, LANES), lambda j, lo, hi, n: (j, 0))),
        out_shape=jax.ShapeDtypeStruct((n_tiles * tm * Y_ROW_TILES, LANES), F32),
        compiler_params=pltpu.CompilerParams(
            dimension_semantics=("arbitrary",), vmem_limit_bytes=VMEM_LIMIT,
            allow_input_fusion=[False] * 4 + [True] * 6), name="moe_experts")(
            tile_lo, tile_hi, n_used, xs, wg, wu, wd, wg, wu, wd)


def _combine_kernel(dest_ref, x1g_ref, p_ref, ys_ref, g_ref, b_ref, wpg_ref, wpp_ref, o_ref, ybuf, sems):
    i = pl.program_id(0)
    tm = x1g_ref.shape[0]
    slot = i % 2

    def row_copy(tile, s, r):
        src = pl.multiple_of(dest_ref[tile * tm + r] * Y_ROW_TILES, Y_ROW_TILES)
        dst = pl.multiple_of(r * Y_ROW_TILES, Y_ROW_TILES)
        return pltpu.make_async_copy(ys_ref.at[pl.ds(src, Y_ROW_TILES)],
                                     ybuf.at[s].at[pl.ds(dst, Y_ROW_TILES)], sems.at[s])

    def start_tile(tile, s):
        def issue(grp, carry):
            for u in range(DMA_ISSUE_UNROLL):
                row_copy(tile, s, grp * DMA_ISSUE_UNROLL + u).start(priority=u % 2)
            return carry
        lax.fori_loop(0, tm // DMA_ISSUE_UNROLL, issue, 0)

    @pl.when(i == 0)
    def _():
        start_tile(0, 0)

    @pl.when(i + 1 < pl.num_programs(0))
    def _():
        start_tile(i + 1, 1 - slot)

    pltpu.make_async_copy(ys_ref.at[pl.ds(0, tm * Y_ROW_TILES)], ybuf.at[slot], sems.at[slot]).wait()
    g = x1g_ref[:, D_MODEL:]
    y = (g[:, 0:1] * _tiles_to_rows(ybuf.at[slot], tm, Y_ROW_TILES)
         + g[:, 1:2] * _tiles_to_rows(ybuf.at[slot], tm, Y_ROW_TILES, first=X_ROW_TILES))
    x2 = _layer_norm(DEEPNORM_ALPHA * x1g_ref[:, 0:D_MODEL] + y, g_ref[...], b_ref[...])
    gate = _sigmoid(jnp.dot(x2.astype(BF16), wpg_ref[...], preferred_element_type=F32))
    o_ref[...] = x2 + gate * jnp.dot(p_ref[...].astype(BF16), wpp_ref[...], preferred_element_type=F32)


def _combine_ln_ple(layer, dest, x1g, p, ys, ln_g, ln_b, w_gate, w_proj, tm=512):
    t = x1g.shape[0]
    d = D_MODEL
    const = lambda shape: pl.BlockSpec(shape, lambda i, dest: (0,) * len(shape))
    return pl.pallas_call(
        _combine_kernel,
        grid_spec=pltpu.PrefetchScalarGridSpec(
            num_scalar_prefetch=1, grid=(t // tm,),
            in_specs=[pl.BlockSpec((tm, x1g.shape[1]), lambda i, dest: (i, 0)),
                      pl.BlockSpec((tm, PLE_DIM), lambda i, dest: (layer * (t // tm) + i, 0)),
                      pl.BlockSpec(memory_space=pl.ANY), const((1, d)), const((1, d)),
                      const((d, d)), const((PLE_DIM, d))],
            out_specs=pl.BlockSpec((tm, d), lambda i, dest: (i, 0)),
            scratch_shapes=[pltpu.VMEM((2, tm * Y_ROW_TILES, LANES), F32), pltpu.SemaphoreType.DMA((2,))]),
        out_shape=jax.ShapeDtypeStruct((t, d), F32),
        compiler_params=_cparams("arbitrary"), name="moe_combine_ln_ple")(
            dest, x1g, p, ys, ln_g.astype(F32).reshape(1, d), ln_b.astype(F32).reshape(1, d), w_gate, w_proj)


def _pad_cols(w, n):
    return jnp.zeros((w.shape[0], n), w.dtype).at[:, 0:w.shape[1]].set(w)


def kernel(x, p, ln1_g, ln1_b, ln2_g, ln2_b, mlstm_w_in, mlstm_conv, mlstm_gate_b, mlstm_norm_g, mlstm_w_out, gla_w_in, gla_w_a2, gla_b_a, gla_norm_g, gla_w_out, hgrn_w_in, hgrn_lb, hgrn_norm_g, hgrn_w_out, router_w, router_b, exp_w_gate, exp_w_up, exp_w_down, ple_w_proj, ple_w_gate):
    bsz, seq, d = x.shape
    assert bsz == 1 and d == D_MODEL and seq % 2048 == 0
    xt = x.reshape(seq, d).astype(F32)
    n_tiles = seq // MOE_TILE + N_BUCKETS
    xs = jnp.zeros((n_tiles * MOE_TILE * X_ROW_TILES, LANES), F32)
    wg_all, wu_all, wd_all = exp_w_gate.astype(BF16), exp_w_up.astype(BF16), exp_w_down.astype(BF16)
    for i in range(DEPTH):
        kind, slot = i % N_MIXERS, i // N_MIXERS
        if kind == 0:
            w_in = mlstm_w_in[slot]
            w = _pad_cols(w_in.astype(BF16), A_PROJ_PAD)
            wgt = jnp.zeros((16, d), BF16).at[0:2 * A_HEADS].set(w_in[:, A_MAIN:].T.astype(BF16))
            z, gt = _inproj(xt, w, wgt)
            y = _mlstm_mix(z, gt, mlstm_conv[slot], mlstm_gate_b[slot], mlstm_norm_g[slot])
            w_out = mlstm_w_out[slot]
        elif kind == 1:
            z = _inproj(xt, _pad_cols(gla_w_in[slot].astype(BF16), B_PROJ_PAD))
            y = _gla_mix(z, gla_w_a2[slot], gla_b_a[slot], gla_norm_g[slot])
            w_out = gla_w_out[slot]
        else:
            z = _inproj(xt, hgrn_w_in[slot].astype(BF16))
            y = _hgrn_mix(z, hgrn_lb, i, hgrn_norm_g[slot])
            w_out = hgrn_w_out[slot]
        x1g, meta, cnt = _outproj_ln_route(y, xt, w_out.astype(BF16), ln1_g[i], ln1_b[i], router_w, router_b)
        dest, tile_lo, tile_hi, n_used = _moe_plan(meta, cnt, MOE_TILE, n_tiles)
        xs = _dispatch(dest, x1g, xs)
        ys = _experts(i, tile_lo, tile_hi, n_used, xs, wg_all, wu_all, wd_all, MOE_TILE)
        xt = _combine_ln_ple(i, dest, x1g, p.reshape(DEPTH * seq, PLE_DIM), ys, ln2_g[i], ln2_b[i],
                             ple_w_gate[i].astype(BF16), ple_w_proj[i].astype(BF16))
    return xt.reshape(bsz, seq, d)
```
